```python
import math
import jax, jax.numpy as jnp
from jax import lax
import numpy as np

D_MODEL = 2048
BATCH = 4
SEQ = 2048
DEPTH = 2
DEC_BATCH = 128
DEC_SEQ = 8
PAST_LEN = 16384
PAGE_SIZE = 128

D_MIX = D_MODEL
C_POOL = D_MIX // 2
C_SSM = D_MIX - C_POOL
POOL_WINDOWS = (2, 4, 8, 16)
N_POOL_GROUPS = len(POOL_WINDOWS)
POOL_GROUP = C_POOL // N_POOL_GROUPS
POOL_BUF = max(POOL_WINDOWS) - 1
SSM_H = 16
SSM_G = C_SSM // SSM_H
SSM_P = 64
D_FF = ((8 * D_MODEL // 3 + 127) // 128) * 128
N_EXPERTS = 8
TOP_K = 2
D_EXPERT = 7 * D_MODEL // 2
N_DENSE = (DEPTH + 1) // 2
N_MOE = DEPTH // 2
EPS = 1e-6

kernel_name = 'hybrid_pool_s5_adaln_decoder_step'


def rmsnorm(x, g):
    xf = x.astype(jnp.float32)
    r = lax.rsqrt(jnp.mean(xf * xf, axis=-1, keepdims=True) + EPS)
    return (xf * r).astype(x.dtype) * g


def swiglu(h, wg, wu, wd):
    return (jax.nn.silu(h @ wg) * (h @ wu)) @ wd


def moe_ffn(h, w_router, b_router, wg, wu, wd):
    B, T, D = h.shape
    hf = h.reshape(B * T, D)
    logits = hf.astype(jnp.float32) @ w_router.astype(jnp.float32) + b_router.astype(jnp.float32)
    top_v, top_i = lax.top_k(logits, TOP_K)
    probs = jax.nn.softmax(top_v, axis=-1)
    comb = jnp.sum(jax.nn.one_hot(top_i, N_EXPERTS, dtype=jnp.float32) * probs[..., None], axis=1).astype(h.dtype)
    out = jnp.zeros_like(hf)
    for e in range(N_EXPERTS):
        out = out + comb[:, e:e + 1] * swiglu(hf, wg[e], wu[e], wd[e])
    return out.reshape(B, T, D)


def pool_mixer(u, buf, w_pool, pool_scale, start_pos):
    B, T, _ = u.shape
    ext = jnp.concatenate([buf.astype(u.dtype), u], axis=1)
    extf = ext.astype(jnp.float32)
    cs = jnp.cumsum(extf, axis=1)
    cs = jnp.concatenate([jnp.zeros((B, 1, C_POOL), jnp.float32), cs], axis=1)
    uf = extf[:, POOL_BUF:]
    pos = start_pos + jnp.arange(T, dtype=jnp.int32)
    outs = []
    for g, w in enumerate(POOL_WINDOWS):
        sl = slice(g * POOL_GROUP, (g + 1) * POOL_GROUP)
        win_sum = cs[:, POOL_BUF + 1:POOL_BUF + 1 + T, sl] - cs[:, POOL_BUF + 1 - w:POOL_BUF + 1 - w + T, sl]
        count = jnp.minimum(pos + 1, w).astype(jnp.float32)[None, :, None]
        d = win_sum / count - uf[..., sl]
        outs.append(d @ w_pool[g].astype(jnp.float32))
    y = jnp.concatenate(outs, axis=-1) * pool_scale.astype(jnp.float32)
    return y.astype(u.dtype), ext[:, -POOL_BUF:]


def s5_mixer(u, h0_re, h0_im, lam_re, lam_im, log_step, b_re, b_im, c_re, c_im, d_skip, w_glu):
    B, T, _ = u.shape
    f32 = jnp.float32
    uf = u.astype(f32).reshape(B, T, SSM_G, SSM_H)
    lr, li = lam_re.astype(f32), lam_im.astype(f32)
    step = jnp.exp(log_step.astype(f32))[:, None]
    mag = jnp.exp(lr * step)
    a_re = mag * jnp.cos(li * step)
    a_im = mag * jnp.sin(li * step)
    den = lr * lr + li * li
    nr = a_re - 1.0
    f_re = (nr * lr + a_im * li) / den
    f_im = (a_im * lr - nr * li) / den
    br, bi = b_re.astype(f32), b_im.astype(f32)
    bb_re = f_re[..., None] * br - f_im[..., None] * bi
    bb_im = f_re[..., None] * bi + f_im[..., None] * br
    bu_re = jnp.einsum('btgh,gph->btgp', uf, bb_re)
    bu_im = jnp.einsum('btgh,gph->btgp', uf, bb_im)
    hr0, hi0 = h0_re.astype(f32), h0_im.astype(f32)
    bu_re = bu_re.at[:, 0].add(a_re * hr0 - a_im * hi0)
    bu_im = bu_im.at[:, 0].add(a_re * hi0 + a_im * hr0)
    ar = jnp.broadcast_to(a_re, bu_re.shape)
    ai = jnp.broadcast_to(a_im, bu_im.shape)

    def combine(e1, e2):
        ar1, ai1, br1, bi1 = e1
        ar2, ai2, br2, bi2 = e2
        return (ar2 * ar1 - ai2 * ai1,
                ar2 * ai1 + ai2 * ar1,
                ar2 * br1 - ai2 * bi1 + br2,
                ar2 * bi1 + ai2 * br1 + bi2)

    _, _, h_re, h_im = lax.associative_scan(combine, (ar, ai, bu_re, bu_im), axis=1)
    y = (jnp.einsum('btgp,ghp->btgh', h_re, c_re.astype(f32))
         - jnp.einsum('btgp,ghp->btgh', h_im, c_im.astype(f32)))
    y = (y + d_skip.astype(f32).reshape(SSM_G, SSM_H) * uf).reshape(B, T, C_SSM)
    g = jax.nn.gelu(y)
    out = g * jax.nn.sigmoid(g @ w_glu.astype(f32))
    return out.astype(u.dtype), h_re[:, -1], h_im[:, -1]


def trunk(x, c, pool_state, ssm_re, ssm_im, start_pos,
          w_ada, b_ada, norm_mix, norm_ffn, w_in, w_pool, pool_scale,
          ssm_lam_re, ssm_lam_im, ssm_log_step, ssm_b_re, ssm_b_im, ssm_c_re, ssm_c_im,
          ssm_d, ssm_w_glu, w_out, ffn_w_gate, ffn_w_up, ffn_w_down,
          moe_w_router, moe_b_router, moe_w_gate, moe_w_up, moe_w_down, norm_final):
    sc = jax.nn.silu(c)
    new_pool, new_re, new_im = [], [], []
    for l in range(DEPTH):
        mod = (sc @ w_ada[l] + b_ada[l])[:, None, :]
        sh1, s1, g1, sh2, s2, g2 = jnp.split(mod, 6, axis=-1)
        h = rmsnorm(x, norm_mix[l]) * (1 + s1) + sh1
        u = h @ w_in[l]
        y_pool, buf = pool_mixer(u[..., :C_POOL], pool_state[l], w_pool[l], pool_scale[l], start_pos)
        y_ssm, hr, hi = s5_mixer(u[..., C_POOL:], ssm_re[l], ssm_im[l], ssm_lam_re[l], ssm_lam_im[l],
                                 ssm_log_step[l], ssm_b_re[l], ssm_b_im[l], ssm_c_re[l], ssm_c_im[l],
                                 ssm_d[l], ssm_w_glu[l])
        x = x + g1 * (jnp.concatenate([y_pool, y_ssm], axis=-1) @ w_out[l])
        new_pool.append(buf)
        new_re.append(hr)
        new_im.append(hi)
        h = rmsnorm(x, norm_ffn[l]) * (1 + s2) + sh2
        if l % 2 == 0:
            i = l // 2
            f = swiglu(h, ffn_w_gate[i], ffn_w_up[i], ffn_w_down[i])
        else:
            i = l // 2
            f = moe_ffn(h, moe_w_router[i], moe_b_router[i], moe_w_gate[i], moe_w_up[i], moe_w_down[i])
        x = x + g2 * f
    y = rmsnorm(x, norm_final)
    return y, jnp.stack(new_pool), jnp.stack(new_re), jnp.stack(new_im)


def setup_inputs(seed: int = 0) -> dict:
    key = jax.random.key(seed)
    ks = iter(jax.random.split(key, 64))
    f32 = jnp.float32

    def nrm(shape, scale):
        return jax.random.normal(next(ks), shape, f32) * scale

    d = {}
    d['x_prompt'] = nrm((BATCH, SEQ, D_MODEL), 1.0)
    d['x_sample'] = nrm((DEC_BATCH, DEC_SEQ, D_MODEL), 1.0)
    d['c_prompt'] = nrm((BATCH, D_MODEL), 1.0)
    d['c_sample'] = nrm((DEC_BATCH, D_MODEL), 1.0)
    d['state_pool'] = nrm((DEPTH, DEC_BATCH, POOL_BUF, C_POOL), 1.0)
    d['state_ssm_re'] = nrm((DEPTH, DEC_BATCH, SSM_G, SSM_P), 0.1)
    d['state_ssm_im'] = nrm((DEPTH, DEC_BATCH, SSM_G, SSM_P), 0.1)
    d['w_ada'] = nrm((DEPTH, D_MODEL, 6 * D_MODEL), D_MODEL ** -0.5)
    d['b_ada'] = nrm((DEPTH, 6 * D_MODEL), 0.02)
    d['norm_mix'] = 1.0 + nrm((DEPTH, D_MODEL), 0.02)
    d['norm_ffn'] = 1.0 + nrm((DEPTH, D_MODEL), 0.02)
    d['w_in'] = nrm((DEPTH, D_MODEL, D_MIX), D_MODEL ** -0.5)
    d['w_pool'] = nrm((DEPTH, N_POOL_GROUPS, POOL_GROUP, POOL_GROUP), POOL_GROUP ** -0.5)
    d['pool_scale'] = 1.0 + nrm((DEPTH, C_POOL), 0.02)
    n = jnp.arange(SSM_P, dtype=f32)
    d['ssm_lam_re'] = -0.5 + nrm((DEPTH, SSM_G, SSM_P), 0.01)
    d['ssm_lam_im'] = math.pi * n + nrm((DEPTH, SSM_G, SSM_P), 0.01)
    d['ssm_log_step'] = jax.random.uniform(next(ks), (DEPTH, SSM_G), f32,
                                           minval=math.log(0.001), maxval=math.log(0.1))
    d['ssm_b_re'] = nrm((DEPTH, SSM_G, SSM_P, SSM_H), (2.0 * SSM_H) ** -0.5)
    d['ssm_b_im'] = nrm((DEPTH, SSM_G, SSM_P, SSM_H), (2.0 * SSM_H) ** -0.5)
    d['ssm_c_re'] = nrm((DEPTH, SSM_G, SSM_H, SSM_P), (2.0 * SSM_P) ** -0.5 * 4.0)
    d['ssm_c_im'] = nrm((DEPTH, SSM_G, SSM_H, SSM_P), (2.0 * SSM_P) ** -0.5 * 4.0)
    d['ssm_d'] = nrm((DEPTH, C_SSM), 1.0)
    d['ssm_w_glu'] = nrm((DEPTH, C_SSM, C_SSM), C_SSM ** -0.5)
    d['w_out'] = nrm((DEPTH, D_MIX, D_MODEL), D_MIX ** -0.5)
    d['ffn_w_gate'] = nrm((N_DENSE, D_MODEL, D_FF), D_MODEL ** -0.5)
    d['ffn_w_up'] = nrm((N_DENSE, D_MODEL, D_FF), D_MODEL ** -0.5)
    d['ffn_w_down'] = nrm((N_DENSE, D_FF, D_MODEL), D_FF ** -0.5)
    d['moe_w_router'] = nrm((N_MOE, D_MODEL, N_EXPERTS), D_MODEL ** -0.5)
    d['moe_b_router'] = nrm((N_MOE, N_EXPERTS), 0.01)
    d['moe_w_gate'] = nrm((N_MOE, N_EXPERTS, D_MODEL, D_EXPERT), D_MODEL ** -0.5)
    d['moe_w_up'] = nrm((N_MOE, N_EXPERTS, D_MODEL, D_EXPERT), D_MODEL ** -0.5)
    d['moe_w_down'] = nrm((N_MOE, N_EXPERTS, D_EXPERT, D_MODEL), D_EXPERT ** -0.5)
    d['norm_final'] = 1.0 + nrm((D_MODEL,), 0.02)
    return d


def reference(x_prompt, x_sample, c_prompt, c_sample, state_pool, state_ssm_re, state_ssm_im,
              w_ada, b_ada, norm_mix, norm_ffn, w_in, w_pool, pool_scale,
              ssm_lam_re, ssm_lam_im, ssm_log_step, ssm_b_re, ssm_b_im, ssm_c_re, ssm_c_im,
              ssm_d, ssm_w_glu, w_out, ffn_w_gate, ffn_w_up, ffn_w_down,
              moe_w_router, moe_b_router, moe_w_gate, moe_w_up, moe_w_down, norm_final):
    weights = (w_ada, b_ada, norm_mix, norm_ffn, w_in, w_pool, pool_scale,
               ssm_lam_re, ssm_lam_im, ssm_log_step, ssm_b_re, ssm_b_im, ssm_c_re, ssm_c_im,
               ssm_d, ssm_w_glu, w_out, ffn_w_gate, ffn_w_up, ffn_w_down,
               moe_w_router, moe_b_router, moe_w_gate, moe_w_up, moe_w_down, norm_final)
    B = x_prompt.shape[0]
    pool0 = jnp.zeros((DEPTH, B, POOL_BUF, C_POOL), x_prompt.dtype)
    re0 = jnp.zeros((DEPTH, B, SSM_G, SSM_P), jnp.float32)
    im0 = jnp.zeros((DEPTH, B, SSM_G, SSM_P), jnp.float32)
    y_prompt, pool_p, re_p, im_p = trunk(x_prompt, c_prompt, pool0, re0, im0, 0, *weights)
    y_sample, pool_s, re_s, im_s = trunk(x_sample, c_sample, state_pool, state_ssm_re, state_ssm_im,
                                         PAST_LEN, *weights)
    return (y_prompt, y_sample, pool_p, re_p, im_p, pool_s, re_s, im_s)
```

```python
import functools
import math

import jax
import jax.numpy as jnp
from jax import lax
from jax.experimental import pallas as pl
from jax.experimental.pallas import tpu as pltpu

F32 = jnp.float32
BF16 = jnp.bfloat16
I32 = jnp.int32

EPS = 1e-6
POOL_WINDOWS = (2, 4, 8, 16)
POOL_BUF = max(POOL_WINDOWS) - 1
SUBLANES_V7X = 8
LANES_V7X = 128
HALO = -(-POOL_BUF // SUBLANES_V7X) * SUBLANES_V7X
SSM_H = 16
SSM_P = 64
CHUNK = 16
CW = CHUNK * SSM_H
SW = 2 * SSM_P
N_EXPERTS = 8
TOP_K = 2
PAST_LEN = 16384

VMEM_LIMIT_V7X = 56 * 1024 * 1024
ROWS_IN = 512
ROWS_MIX = 256
ROWS_FFN = 512
FF_TILE = 512
ROWS_ROUTE = 512
MOE_TILE_ROWS = 1024
MOE_SUB = 256
MOE_FF_TILE = 256
ROWS_COMBINE = 256
S5_GROUPS_PER_STEP = 4
ADA_TN = 1024


def _cparams(sem):
    return pltpu.CompilerParams(dimension_semantics=sem, vmem_limit_bytes=VMEM_LIMIT_V7X)


def _tile(b, t, rows):
    if t >= rows:
        assert t % rows == 0
        return 1, rows
    nb = max(1, min(b, rows // t))
    assert b % nb == 0
    return nb, t


def _modnorm(x, g, s, sh):
    r = lax.rsqrt(jnp.mean(x * x, axis=-1, keepdims=True) + EPS)
    return (x * r) * g * (1.0 + s) + sh


def _dot(a, b):
    return jnp.dot(a, b, preferred_element_type=F32)


def _mod_spec(nb, d, k):
    return pl.BlockSpec((nb, 1, d), lambda b, i, k=k: (b, 0, k))


def _ada_kernel(c_ref, w_ref, b_ref, o_ref):
    c = c_ref[...]
    sc = c * jax.nn.sigmoid(c)
    o_ref[0] = _dot(sc.astype(BF16), w_ref[0].astype(BF16)) + b_ref[0]


def _ada(c_all, w_ada, b_ada):
    m, d = c_all.shape
    depth, _, n6 = w_ada.shape
    tn = math.gcd(ADA_TN, n6)
    return pl.pallas_call(
        _ada_kernel,
        grid=(depth, n6 // tn),
        in_specs=[pl.BlockSpec((m, d), lambda l, n: (0, 0)),
                  pl.BlockSpec((1, d, tn), lambda l, n: (l, 0, n)),
                  pl.BlockSpec((1, 1, tn), lambda l, n: (l, 0, n))],
        out_specs=pl.BlockSpec((1, m, tn), lambda l, n: (l, 0, n)),
        out_shape=jax.ShapeDtypeStruct((depth, m, n6), F32),
        compiler_params=_cparams(("arbitrary", "arbitrary")),
        name="adaln_mod",
    )(c_all, w_ada, b_ada.reshape(depth, 1, n6))


def _in_kernel(x_ref, g_ref, s_ref, sh_ref, w_ref, u_ref):
    nb, tt, d = x_ref.shape
    h = _modnorm(x_ref[...], g_ref[...], s_ref[...], sh_ref[...])
    u = _dot(h.reshape(nb * tt, d).astype(BF16), w_ref[...])
    u_ref[...] = u.reshape(nb, tt, u_ref.shape[-1])


def _proj_in(x, mod, g, w_bf16):
    b, t, d = x.shape
    n = w_bf16.shape[1]
    nb, tt = _tile(b, t, ROWS_IN)
    return pl.pallas_call(
        _in_kernel,
        grid=(b // nb, t // tt),
        in_specs=[pl.BlockSpec((nb, tt, d), lambda bi, i: (bi, i, 0)),
                  pl.BlockSpec((1, d), lambda bi, i: (0, 0)),
                  _mod_spec(nb, d, 1), _mod_spec(nb, d, 0),
                  pl.BlockSpec((d, n), lambda bi, i: (0, 0))],
        out_specs=pl.BlockSpec((nb, tt, n), lambda bi, i: (bi, i, 0)),
        out_shape=jax.ShapeDtypeStruct((b, t, n), F32),
        compiler_params=_cparams(("arbitrary", "arbitrary")),
        name="norm_proj_in",
    )(x, g, mod, mod, w_bf16)


def _s5_params(lam_re, lam_im, log_step, b_re, b_im, c_re, c_im):
    hp = lax.Precision.HIGHEST
    lr, li = lam_re.astype(F32), lam_im.astype(F32)
    step = jnp.exp(log_step.astype(F32))[:, None]
    mag = jnp.exp(lr * step)
    a_re = mag * jnp.cos(li * step)
    a_im = mag * jnp.sin(li * step)
    den = lr * lr + li * li
    nr = a_re - 1.0
    f_re = (nr * lr + a_im * li) / den
    f_im = (a_im * lr - nr * li) / den
    br, bi = b_re.astype(F32), b_im.astype(F32)
    bb_re = f_re[..., None] * br - f_im[..., None] * bi
    bb_im = f_re[..., None] * bi + f_im[..., None] * br
    cr, ci = c_re.astype(F32), c_im.astype(F32)
    g = lr.shape[0]

    pw_re, pw_im = [jnp.ones_like(a_re)], [jnp.zeros_like(a_im)]
    for _ in range(CHUNK):
        pr, pi = pw_re[-1], pw_im[-1]
        pw_re.append(pr * a_re - pi * a_im)
        pw_im.append(pr * a_im + pi * a_re)
    pw_re, pw_im = jnp.stack(pw_re), jnp.stack(pw_im)

    w_re = pw_re[:CHUNK, :, :, None] * bb_re - pw_im[:CHUNK, :, :, None] * bb_im
    w_im = pw_re[:CHUNK, :, :, None] * bb_im + pw_im[:CHUNK, :, :, None] * bb_re
    m_lag = (jnp.einsum('gop,lgpi->lgio', cr, w_re, precision=hp)
             - jnp.einsum('gop,lgpi->lgio', ci, w_im, precision=hp))
    s_idx = jnp.arange(CHUNK)[:, None]
    t_idx = jnp.arange(CHUNK)[None, :]
    lag = t_idx - s_idx
    kblk = jnp.where((lag >= 0)[:, :, None, None, None], m_lag[jnp.maximum(lag, 0)], 0.0)
    kmat = kblk.transpose(2, 0, 3, 1, 4).reshape(g, CW, CW)

    rev_re, rev_im = pw_re[:CHUNK][::-1], pw_im[:CHUNK][::-1]
    bm_re = rev_re[..., None] * bb_re - rev_im[..., None] * bb_im
    bm_im = rev_re[..., None] * bb_im + rev_im[..., None] * bb_re
    bm_re = bm_re.transpose(1, 0, 3, 2).reshape(g, CW, SSM_P)
    bm_im = bm_im.transpose(1, 0, 3, 2).reshape(g, CW, SSM_P)
    bmat = jnp.concatenate([bm_re, bm_im, bm_im, bm_re], axis=-1)

    qr, qi = pw_re[1:], pw_im[1:]
    cm_re = cr[None] * qr[:, :, None, :] - ci[None] * qi[:, :, None, :]
    cm_im = -(cr[None] * qi[:, :, None, :] + ci[None] * qr[:, :, None, :])
    cm_re = cm_re.transpose(1, 3, 0, 2).reshape(g, SSM_P, CW)
    cm_im = cm_im.transpose(1, 3, 0, 2).reshape(g, SSM_P, CW)
    cmat = jnp.concatenate([cm_re, cm_im], axis=1)

    def coef(k):
        pr, pi = pw_re[k], pw_im[k]
        return (jnp.concatenate([pr, pr, pr, pr], axis=-1),
                jnp.concatenate([-pi, pi, pi, -pi], axis=-1))
    return kmat.astype(BF16), bmat.astype(BF16), cmat.astype(BF16), coef


def _s5_kernel(up_ref, us_ref, x0_ref, k_ref, b_ref, c_ref, coef_ref,
               yp_ref, ys_ref, xe_ref, x1_ref, buz_scr, s_scr, *, bp, ws):
    gb, rows, _ = up_ref.shape
    nchunk = rows // bp
    for gi in range(gb):
        buz_scr[gi] = _dot(up_ref[gi], b_ref[gi])
    pb = [jnp.broadcast_to(coef_ref[gi, 0:1, :], (bp, 2 * SW)) for gi in range(gb)]
    qb = [jnp.broadcast_to(coef_ref[gi, 1:2, :], (bp, 2 * SW)) for gi in range(gb)]
    zs = [jnp.zeros((bp, 2 * SW), F32) for _ in range(gb)]
    for c in range(nchunk):
        lo, hi = c * bp, (c + 1) * bp
        for gi in range(gb):
            z = zs[gi]
            s_scr[gi, lo:hi, :] = z[:, :SW]
            zsw = jnp.concatenate([z[:, SW:], z[:, :SW]], axis=1)
            zs[gi] = pb[gi] * z + qb[gi] * zsw + buz_scr[gi, lo:hi, :]
    for gi in range(gb):
        yp_ref[gi] = _dot(up_ref[gi], k_ref[gi]) + _dot(s_scr[gi].astype(BF16), c_ref[gi])
        xe_ref[gi] = zs[gi][:, :SW]
        us = us_ref[gi]
        x0 = x0_ref[gi]
        ys_ref[gi] = _dot(us, k_ref[gi, :ws, :ws]) + _dot(x0.astype(BF16), c_ref[gi, :, :ws])
        x0s = jnp.concatenate([x0[:, SSM_P:], x0[:, :SSM_P]], axis=1)
        x1_ref[gi] = (coef_ref[gi, 2:3, :SW] * x0 + coef_ref[gi, 3:4, :SW] * x0s
                      + _dot(us, b_ref[gi, CW - ws:, :SW]))


def _s5(up, us, x0, kmat, bmat, cmat, coef, bp):
    g, rows_p, _ = up.shape
    _, bs, ws = us.shape
    gb = min(S5_GROUPS_PER_STEP, g)
    assert g % gb == 0

    def spec(*shape):
        return pl.BlockSpec((gb,) + shape, lambda i: (i,) + (0,) * len(shape))
    return pl.pallas_call(
        functools.partial(_s5_kernel, bp=bp, ws=ws),
        grid=(g // gb,),
        in_specs=[spec(rows_p, CW), spec(bs, ws), spec(bs, SW), spec(CW, CW), spec(CW, 2 * SW),
                  spec(SW, CW), spec(4, 2 * SW)],
        out_specs=[spec(rows_p, CW), spec(bs, ws), spec(bp, SW), spec(bs, SW)],
        out_shape=[jax.ShapeDtypeStruct((g, rows_p, CW), F32), jax.ShapeDtypeStruct((g, bs, ws), F32),
                   jax.ShapeDtypeStruct((g, bp, SW), F32), jax.ShapeDtypeStruct((g, bs, SW), F32)],
        scratch_shapes=[pltpu.VMEM((gb, rows_p, 2 * SW), F32), pltpu.VMEM((gb, rows_p, SW), F32)],
        compiler_params=_cparams(("arbitrary",)),
        name="s5_chunked",
    )(up, us, x0, kmat, bmat, cmat, coef)


def _mix_kernel(x_ref, up_ref, us_ref, ys_ref, st_ref, g1_ref, wp_ref, ps_ref, dsk_ref, wglu_ref,
                wout_ref, o_ref, ext, *, start_pos, carry):
    nb, tt, d = x_ref.shape
    c = up_ref.shape[-1]
    pg = c // len(POOL_WINDOWS)
    rows = nb * tt
    i = pl.program_id(1)

    @pl.when(i == 0)
    def _():
        ext[:, 0:HALO, :] = st_ref[...]
    ext[:, HALO:HALO + tt, :] = up_ref[...]

    pos = start_pos + i * tt + lax.broadcasted_iota(I32, (1, tt, 1), 1)
    outs = []
    for gidx, w in enumerate(POOL_WINDOWS):
        c0 = gidx * pg
        cur = ext[:, HALO:HALO + tt, c0:c0 + pg]
        s = cur
        for k in range(1, w):
            s = s + ext[:, HALO - k:HALO - k + tt, c0:c0 + pg]
        cnt = jnp.minimum(pos + 1, w).astype(F32)
        dd = s / cnt - cur
        yg = _dot(dd.reshape(rows, pg).astype(BF16), wp_ref[gidx])
        outs.append(yg * ps_ref[:, c0:c0 + pg])
    y_pool = jnp.concatenate(outs, axis=-1)

    if carry:
        ext[:, 0:HALO, :] = ext[:, tt:tt + HALO, :]

    us = us_ref[...].reshape(rows, c)
    yf = ys_ref[...].reshape(rows, c) + dsk_ref[...] * us
    gl = jax.nn.gelu(yf, approximate=True)
    o = gl * jax.nn.sigmoid(_dot(gl.astype(BF16), wglu_ref[...]))
    m = jnp.concatenate([y_pool, o], axis=-1).astype(BF16)
    mo = _dot(m, wout_ref[...])
    o_ref[...] = x_ref[...] + g1_ref[...] * mo.reshape(nb, tt, d)


def _mix(x, u, y_ssm, pool_state16, mod, wp, ps, dsk, wglu, wout, start_pos):
    b, t, d = x.shape
    c = y_ssm.shape[-1]
    nb, tt = _tile(b, t, ROWS_MIX)
    nt = t // tt
    assert nt == 1 or tt >= HALO
    const2 = lambda bi, i: (0, 0)
    return pl.pallas_call(
        functools.partial(_mix_kernel, start_pos=start_pos, carry=nt > 1),
        grid=(b // nb, nt),
        in_specs=[pl.BlockSpec((nb, tt, d), lambda bi, i: (bi, i, 0)),
                  pl.BlockSpec((nb, tt, c), lambda bi, i: (bi, i, 0)),
                  pl.BlockSpec((nb, tt, c), lambda bi, i: (bi, i, 1)),
                  pl.BlockSpec((nb, tt, c), lambda bi, i: (bi, i, 0)),
                  pl.BlockSpec((nb, HALO, c), lambda bi, i: (bi, 0, 0)),
                  _mod_spec(nb, d, 2),
                  pl.BlockSpec(wp.shape, lambda bi, i: (0, 0, 0)),
                  pl.BlockSpec((1, c), const2), pl.BlockSpec((1, c), const2),
                  pl.BlockSpec((c, c), const2), pl.BlockSpec((d, d), const2)],
        out_specs=pl.BlockSpec((nb, tt, d), lambda bi, i: (bi, i, 0)),
        out_shape=jax.ShapeDtypeStruct((b, t, d), F32),
        scratch_shapes=[pltpu.VMEM((nb, HALO + tt, c), F32)],
        compiler_params=_cparams(("arbitrary", "arbitrary")),
        name="pool_glu_proj_out",
    )(x, u, u, y_ssm, pool_state16, mod, wp, ps, dsk, wglu, wout)


def _ffn_kernel(x_ref, g_ref, s_ref, sh_ref, g2_ref, wg_ref, wu_ref, wd_ref, o_ref, hb, acc):
    nb, tt, d = x_ref.shape
    j = pl.program_id(2)

    @pl.when(j == 0)
    def _():
        h = _modnorm(x_ref[...], g_ref[...], s_ref[...], sh_ref[...])
        hb[...] = h.reshape(nb * tt, d).astype(BF16)
        acc[...] = jnp.zeros_like(acc)

    h = hb[...]
    gate = _dot(h, wg_ref[...])
    up = _dot(h, wu_ref[...])
    a = (gate * jax.nn.sigmoid(gate) * up).astype(BF16)
    acc[...] += _dot(a, wd_ref[...])

    @pl.when(j == pl.num_programs(2) - 1)
    def _():
        o_ref[...] = x_ref[...] + g2_ref[...] * acc[...].reshape(nb, tt, d)


def _ffn(x, mod, g, wg, wu, wd):
    b, t, d = x.shape
    ff = wg.shape[1]
    nb, tt = _tile(b, t, ROWS_FFN)
    tf = min(FF_TILE, ff)
    assert ff % tf == 0
    return pl.pallas_call(
        _ffn_kernel,
        grid=(b // nb, t // tt, ff // tf),
        in_specs=[pl.BlockSpec((nb, tt, d), lambda bi, i, j: (bi, i, 0)),
                  pl.BlockSpec((1, d), lambda bi, i, j: (0, 0)),
                  pl.BlockSpec((nb, 1, d), lambda bi, i, j: (bi, 0, 4)),
                  pl.BlockSpec((nb, 1, d), lambda bi, i, j: (bi, 0, 3)),
                  pl.BlockSpec((nb, 1, d), lambda bi, i, j: (bi, 0, 5)),
                  pl.BlockSpec((d, tf), lambda bi, i, j: (0, j)),
                  pl.BlockSpec((d, tf), lambda bi, i, j: (0, j)),
                  pl.BlockSpec((tf, d), lambda bi, i, j: (j, 0))],
        out_specs=pl.BlockSpec((nb, tt, d), lambda bi, i, j: (bi, i, 0)),
        out_shape=jax.ShapeDtypeStruct((b, t, d), F32),
        scratch_shapes=[pltpu.VMEM((nb * tt, d), BF16), pltpu.VMEM((nb * tt, d), F32)],
        compiler_params=_cparams(("arbitrary", "arbitrary", "arbitrary")),
        name="dense_swiglu",
    )(x, g, mod, mod, mod, wg, wu, wd)


ROUTE_LANES = LANES_V7X


def _route_kernel(x_ref, g_ref, s_ref, sh_ref, wr_ref, br_ref, cin_ref, h_ref, r_ref, cout_ref, carry):
    nb, tt, d = x_ref.shape
    tm = nb * tt
    first = jnp.logical_and(pl.program_id(0) == 0, pl.program_id(1) == 0)

    @pl.when(first)
    def _():
        carry[...] = cin_ref[...]

    h = _modnorm(x_ref[...], g_ref[...], s_ref[...], sh_ref[...]).reshape(tm, d)
    h_ref[...] = h
    logits = jnp.dot(h, wr_ref[...], preferred_element_type=F32, precision=lax.Precision.HIGHEST) + br_ref[...]
    lane = lax.broadcasted_iota(I32, (tm, ROUTE_LANES), 1)
    neg = jnp.float32(-jnp.inf)
    l1 = jnp.where(lane < N_EXPERTS, logits, neg)
    m1 = jnp.max(l1, axis=-1, keepdims=True)
    i1 = jnp.min(jnp.where(l1 == m1, lane, ROUTE_LANES), axis=-1, keepdims=True)
    l2 = jnp.where(lane == i1, neg, l1)
    m2 = jnp.max(l2, axis=-1, keepdims=True)
    i2 = jnp.min(jnp.where(l2 == m2, lane, ROUTE_LANES), axis=-1, keepdims=True)
    e2 = jnp.exp(m2 - m1)
    den = 1.0 + e2
    p1 = 1.0 / den
    p2 = e2 / den

    onehot = jnp.logical_or(lane == i1, lane == i2).astype(F32)
    rr = lax.broadcasted_iota(I32, (tm, tm), 0)
    cc = lax.broadcasted_iota(I32, (tm, tm), 1)
    before = (cc < rr).astype(BF16)
    rank = _dot(before, onehot.astype(BF16)) + carry[...]
    r1 = jnp.sum(jnp.where(lane == i1, rank, 0.0), axis=-1, keepdims=True)
    r2 = jnp.sum(jnp.where(lane == i2, rank, 0.0), axis=-1, keepdims=True)
    carry[...] += jnp.sum(onehot, axis=0, keepdims=True)

    out = jnp.where(lane == 0, i1.astype(F32), 0.0)
    out = jnp.where(lane == 1, i2.astype(F32), out)
    out = jnp.where(lane == 2, p1, out)
    out = jnp.where(lane == 3, p2, out)
    out = jnp.where(lane == 4, r1, out)
    out = jnp.where(lane == 5, r2, out)
    r_ref[...] = out
    cout_ref[...] = carry[...]


def _route(x, mod, g, wr_pad, br_pad, counts_in):
    b, t, d = x.shape
    nb, tt = _tile(b, t, ROWS_ROUTE)
    tm = nb * tt
    nt = t // tt
    const2 = lambda bi, i: (0, 0)
    return pl.pallas_call(
        _route_kernel,
        grid=(b // nb, nt),
        in_specs=[pl.BlockSpec((nb, tt, d), lambda bi, i: (bi, i, 0)),
                  pl.BlockSpec((1, d), const2),
                  _mod_spec(nb, d, 4), _mod_spec(nb, d, 3),
                  pl.BlockSpec((d, ROUTE_LANES), const2),
                  pl.BlockSpec((1, ROUTE_LANES), const2),
                  pl.BlockSpec((1, ROUTE_LANES), const2)],
        out_specs=[pl.BlockSpec((tm, d), lambda bi, i: (bi * nt + i, 0)),
                   pl.BlockSpec((tm, ROUTE_LANES), lambda bi, i: (bi * nt + i, 0)),
                   pl.BlockSpec((1, ROUTE_LANES), const2)],
        out_shape=[jax.ShapeDtypeStruct((b * t, d), F32),
                   jax.ShapeDtypeStruct((b * t, ROUTE_LANES), F32),
                   jax.ShapeDtypeStruct((1, ROUTE_LANES), F32)],
        scratch_shapes=[pltpu.VMEM((1, ROUTE_LANES), F32)],
        compiler_params=_cparams(("arbitrary", "arbitrary")),
        name="moe_route",
    )(x, g, mod, mod, wr_pad, br_pad, counts_in)


def _row_gather_copy(src_hbm, tok, dst, r, sem):
    return pltpu.make_async_copy(src_hbm.at[pl.ds(tok, 1), :], dst.at[pl.ds(r, 1), :], sem)


def _moe_kernel(te_ref, tr_ref, src_ref, h_hbm, wg_ref, wu_ref, wd_ref, o_ref,
                xb, stage, wgb, wub, wdb, sem):
    s = pl.program_id(0)
    j = pl.program_id(1)
    rows = tr_ref[s]
    nsub = (rows + MOE_SUB - 1) // MOE_SUB
    tile_rows = xb.shape[0]

    @pl.when(j == 0)
    def _():
        def zero_body(sub, carry):
            r0 = pl.multiple_of(sub * MOE_SUB, MOE_SUB)
            o_ref[pl.ds(r0, MOE_SUB), :] = jnp.zeros((MOE_SUB, o_ref.shape[1]), F32)
            return carry
        lax.fori_loop(nsub, tile_rows // MOE_SUB, zero_body, 0)

    @pl.when(rows > 0)
    def _():
        @pl.when(j == 0)
        def _gather():
            def issue(sub, slot):
                base = s * tile_rows + sub * MOE_SUB

                def body(r, carry):
                    _row_gather_copy(h_hbm, src_ref[base + r], stage.at[slot], r, sem.at[slot]).start()
                    return carry
                lax.fori_loop(0, MOE_SUB, body, 0, unroll=8)

            issue(0, 0)

            def sub_body(sub, carry):
                slot = sub % 2

                @pl.when(sub + 1 < nsub)
                def _():
                    issue(sub + 1, 1 - slot)
                pltpu.make_async_copy(h_hbm.at[pl.ds(0, MOE_SUB), :], stage.at[slot], sem.at[slot]).wait()
                r0 = pl.multiple_of(sub * MOE_SUB, MOE_SUB)
                xb[pl.ds(r0, MOE_SUB), :] = stage[slot].astype(BF16)
                return carry
            lax.fori_loop(0, nsub, sub_body, 0)

        wgb[...] = wg_ref[...].astype(BF16)
        wub[...] = wu_ref[...].astype(BF16)
        wdb[...] = wd_ref[...].astype(BF16)

        def expert_rows(sub, accumulate):
            r0 = pl.multiple_of(sub * MOE_SUB, MOE_SUB)
            xs = xb[pl.ds(r0, MOE_SUB), :]
            gate = _dot(xs, wgb[...])
            up = _dot(xs, wub[...])
            a = (gate * jax.nn.sigmoid(gate) * up).astype(BF16)
            c = _dot(a, wdb[...])
            if accumulate:
                o_ref[pl.ds(r0, MOE_SUB), :] += c
            else:
                o_ref[pl.ds(r0, MOE_SUB), :] = c

        @pl.when(j == 0)
        def _():
            def body(sub, carry):
                expert_rows(sub, False)
                return carry
            lax.fori_loop(0, nsub, body, 0)

        @pl.when(j > 0)
        def _():
            def body(sub, carry):
                expert_rows(sub, True)
                return carry
            lax.fori_loop(0, nsub, body, 0)


def _moe_experts(h_all, te, tr, src, wg, wu, wd, n_tiles):
    n, d = h_all.shape
    e, _, ff = wg.shape
    tf = min(MOE_FF_TILE, ff)
    assert ff % tf == 0
    nj = ff // tf
    r = MOE_TILE_ROWS

    def jj(s, j, tr_ref):
        return jnp.where(tr_ref[s] > 0, j, nj - 1)

    grid_spec = pltpu.PrefetchScalarGridSpec(
        num_scalar_prefetch=3,
        grid=(n_tiles, nj),
        in_specs=[pl.BlockSpec(memory_space=pl.ANY),
                  pl.BlockSpec((None, d, tf), lambda s, j, te_r, tr_r, src_r: (te_r[s], 0, jj(s, j, tr_r))),
                  pl.BlockSpec((None, d, tf), lambda s, j, te_r, tr_r, src_r: (te_r[s], 0, jj(s, j, tr_r))),
                  pl.BlockSpec((None, tf, d), lambda s, j, te_r, tr_r, src_r: (te_r[s], jj(s, j, tr_r), 0))],
        out_specs=pl.BlockSpec((r, d), lambda s, j, te_r, tr_r, src_r: (s, 0)),
        scratch_shapes=[pltpu.VMEM((r, d), BF16), pltpu.VMEM((2, MOE_SUB, d), F32),
                        pltpu.VMEM((d, tf), BF16), pltpu.VMEM((d, tf), BF16), pltpu.VMEM((tf, d), BF16),
                        pltpu.SemaphoreType.DMA((2,))],
    )
    return pl.pallas_call(
        _moe_kernel,
        grid_spec=grid_spec,
        out_shape=jax.ShapeDtypeStruct((n_tiles * r, d), F32),
        compiler_params=_cparams(("arbitrary", "arbitrary")),
        name="moe_experts",
    )(te, tr, src, h_all, wg, wu, wd)


def _combine_kernel(s1_ref, s2_ref, x_ref, g2_ref, r_ref, gf_ref, e_hbm, o_ref, st1, st2, sem):
    nb, tt, d = x_ref.shape
    tm = nb * tt
    base = (pl.program_id(0) * pl.num_programs(1) + pl.program_id(1)) * tm

    def body(r, carry):
        _row_gather_copy(e_hbm, s1_ref[base + r], st1, r, sem.at[0]).start()
        _row_gather_copy(e_hbm, s2_ref[base + r], st2, r, sem.at[1]).start()
        return carry
    lax.fori_loop(0, tm, body, 0, unroll=8)
    pltpu.make_async_copy(e_hbm.at[pl.ds(0, tm), :], st1, sem.at[0]).wait()
    pltpu.make_async_copy(e_hbm.at[pl.ds(0, tm), :], st2, sem.at[1]).wait()

    f = r_ref[:, 2:3] * st1[...] + r_ref[:, 3:4] * st2[...]
    x2 = x_ref[...] + g2_ref[...] * f.reshape(nb, tt, d)
    rs = lax.rsqrt(jnp.mean(x2 * x2, axis=-1, keepdims=True) + EPS)
    o_ref[...] = (x2 * rs) * gf_ref[...]


def _combine(x, mod, route, slot1, slot2, e_out, gf):
    b, t, d = x.shape
    nb, tt = _tile(b, t, ROWS_COMBINE)
    tm = nb * tt
    nt = t // tt
    grid_spec = pltpu.PrefetchScalarGridSpec(
        num_scalar_prefetch=2,
        grid=(b // nb, nt),
        in_specs=[pl.BlockSpec((nb, tt, d), lambda bi, i, a, c: (bi, i, 0)),
                  pl.BlockSpec((nb, 1, d), lambda bi, i, a, c: (bi, 0, 5)),
                  pl.BlockSpec((tm, ROUTE_LANES), lambda bi, i, a, c: (bi * nt + i, 0)),
                  pl.BlockSpec((1, d), lambda bi, i, a, c: (0, 0)),
                  pl.BlockSpec(memory_space=pl.ANY)],
        out_specs=pl.BlockSpec((nb, tt, d), lambda bi, i, a, c: (bi, i, 0)),
        scratch_shapes=[pltpu.VMEM((tm, d), F32), pltpu.VMEM((tm, d), F32), pltpu.SemaphoreType.DMA((2,))],
    )
    return pl.pallas_call(
        _combine_kernel,
        grid_spec=grid_spec,
        out_shape=jax.ShapeDtypeStruct((b, t, d), F32),
        compiler_params=_cparams(("arbitrary", "arbitrary")),
        name="moe_combine_norm",
    )(slot1, slot2, x, mod, route, gf, e_out)


def _moe_layout(route, counts_f):
    n = route.shape[0]
    r, sub = MOE_TILE_ROWS, MOE_SUB
    n_tiles = -(-TOP_K * n // r) + N_EXPERTS
    i1 = route[:, 0].astype(I32)
    i2 = route[:, 1].astype(I32)
    r1 = route[:, 4].astype(I32)
    r2 = route[:, 5].astype(I32)
    counts = counts_f[0, :N_EXPERTS].astype(I32)
    nt = (counts + r - 1) // r
    ntc = jnp.maximum(nt, 1)
    sz = (((counts + ntc - 1) // ntc + sub - 1) // sub) * sub
    cum = jnp.cumsum(nt)
    tstart = cum - nt
    total = cum[-1]
    s_idx = jnp.arange(n_tiles, dtype=I32)
    te_raw = jnp.minimum(jnp.searchsorted(cum, s_idx, side='right').astype(I32), N_EXPERTS - 1)
    used = s_idx < total
    last_e = te_raw[jnp.maximum(total - 1, 0)]
    te = jnp.where(used, te_raw, last_e)
    k_in = s_idx - tstart[te]
    tr = jnp.where(used, jnp.clip(counts[te] - k_in * sz[te], 0, sz[te]), 0).astype(I32)

    def slot(ei, ri):
        return (tstart[ei] + ri // sz[ei]) * r + ri % sz[ei]
    slot1, slot2 = slot(i1, r1), slot(i2, r2)
    tok = jnp.arange(n, dtype=I32)
    src = jnp.zeros((n_tiles * r,), I32).at[jnp.concatenate([slot1, slot2])].set(jnp.concatenate([tok, tok]))
    return te, tr, src, slot1, slot2, n_tiles


def _chunk_flat(us, bp_major):
    b, t, c = us.shape
    g = c // SSM_H
    if bp_major:
        nc = t // CHUNK
        v = us.reshape(b, nc, CHUNK, g, SSM_H).transpose(3, 1, 0, 2, 4)
        return v.reshape(g, nc * b, CW)
    return us.reshape(b, t, g, SSM_H).transpose(2, 0, 1, 3).reshape(g, b, t * SSM_H)


def _chunk_unflat(y, b, t, bp_major):
    g = y.shape[0]
    if bp_major:
        nc = t // CHUNK
        return y.reshape(g, nc, b, CHUNK, SSM_H).transpose(2, 1, 3, 0, 4).reshape(b, t, g * SSM_H)
    return y.reshape(g, b, t, SSM_H).transpose(1, 2, 0, 3).reshape(b, t, g * SSM_H)


def kernel(x_prompt, x_sample, c_prompt, c_sample, state_pool, state_ssm_re, state_ssm_im, w_ada, b_ada, norm_mix, norm_ffn, w_in, w_pool, pool_scale, ssm_lam_re, ssm_lam_im, ssm_log_step, ssm_b_re, ssm_b_im, ssm_c_re, ssm_c_im, ssm_d, ssm_w_glu, w_out, ffn_w_gate, ffn_w_up, ffn_w_down, moe_w_router, moe_b_router, moe_w_gate, moe_w_up, moe_w_down, norm_final):
    bp, tp, d = x_prompt.shape
    bs, ts, _ = x_sample.shape
    depth = w_ada.shape[0]
    c_pool = w_pool.shape[1] * w_pool.shape[2]
    assert tp % CHUNK == 0 and ts <= CHUNK

    m_rows = -(-(bp + bs) // SUBLANES_V7X) * SUBLANES_V7X
    c_all = jnp.concatenate([c_prompt, c_sample, jnp.zeros((m_rows - bp - bs, d), F32)], axis=0)
    mod_all = _ada(c_all, w_ada, b_ada)

    xp, xs = x_prompt, x_sample
    pool_p, re_p, im_p, pool_s, re_s, im_s = [], [], [], [], [], []
    y_p = y_s = None
    for l in range(depth):
        mod_p = mod_all[l, :bp].reshape(bp, 1, 6 * d)
        mod_s = mod_all[l, bp:bp + bs].reshape(bs, 1, 6 * d)
        gm = norm_mix[l].reshape(1, d)
        gn = norm_ffn[l].reshape(1, d)

        w_in_b = w_in[l].astype(BF16)
        up_all = _proj_in(xp, mod_p, gm, w_in_b)
        us_all = _proj_in(xs, mod_s, gm, w_in_b)
        kmat, bmat, cmat, coef = _s5_params(ssm_lam_re[l], ssm_lam_im[l], ssm_log_step[l],
                                            ssm_b_re[l], ssm_b_im[l], ssm_c_re[l], ssm_c_im[l])
        p16, q16 = coef(CHUNK)
        p8, q8 = coef(ts)
        coef_arr = jnp.stack([p16, q16, p8, q8], axis=1)
        x0 = jnp.concatenate([state_ssm_re[l], state_ssm_im[l]], axis=-1).astype(F32).transpose(1, 0, 2)
        u_cf_p = _chunk_flat(up_all[..., c_pool:], True).astype(BF16)
        u_cf_s = _chunk_flat(us_all[..., c_pool:], False).astype(BF16)
        yp_cf, ys_cf, xend, x1 = _s5(u_cf_p, u_cf_s, x0, kmat, bmat, cmat, coef_arr, bp)
        y_ssm_p = _chunk_unflat(yp_cf, bp, tp, True)
        y_ssm_s = _chunk_unflat(ys_cf, bs, ts, False)

        wp_b = w_pool[l].astype(BF16)
        ps = pool_scale[l].reshape(1, c_pool).astype(F32)
        dsk = ssm_d[l].reshape(1, -1).astype(F32)
        wglu_b = ssm_w_glu[l].astype(BF16)
        wout_b = w_out[l].astype(BF16)
        st_p = jnp.zeros((bp, HALO, c_pool), F32)
        st_s = jnp.concatenate([jnp.zeros((bs, HALO - POOL_BUF, c_pool), F32), state_pool[l].astype(F32)], axis=1)
        xp = _mix(xp, up_all, y_ssm_p, st_p, mod_p, wp_b, ps, dsk, wglu_b, wout_b, 0)
        xs = _mix(xs, us_all, y_ssm_s, st_s, mod_s, wp_b, ps, dsk, wglu_b, wout_b, PAST_LEN)

        pool_p.append(up_all[:, tp - POOL_BUF:, :c_pool])
        pool_s.append(jnp.concatenate([state_pool[l], us_all[:, :, :c_pool]], axis=1)[:, -POOL_BUF:])
        re_p.append(xend[..., :SSM_P].transpose(1, 0, 2))
        im_p.append(xend[..., SSM_P:].transpose(1, 0, 2))
        re_s.append(x1[..., :SSM_P].transpose(1, 0, 2))
        im_s.append(x1[..., SSM_P:].transpose(1, 0, 2))

        i = l // 2
        if l % 2 == 0:
            ff = ffn_w_gate.shape[-1]
            ffp = -(-ff // FF_TILE) * FF_TILE
            wg = jnp.pad(ffn_w_gate[i].astype(BF16), ((0, 0), (0, ffp - ff)))
            wu = jnp.pad(ffn_w_up[i].astype(BF16), ((0, 0), (0, ffp - ff)))
            wd = jnp.pad(ffn_w_down[i].astype(BF16), ((0, ffp - ff), (0, 0)))
            xp = _ffn(xp, mod_p, gn, wg, wu, wd)
            xs = _ffn(xs, mod_s, gn, wg, wu, wd)
        else:
            ne = moe_w_router.shape[-1]
            wr = jnp.pad(moe_w_router[i].astype(F32), ((0, 0), (0, ROUTE_LANES - ne)))
            br = jnp.pad(moe_b_router[i].astype(F32), (0, ROUTE_LANES - ne)).reshape(1, ROUTE_LANES)
            zero_counts = jnp.zeros((1, ROUTE_LANES), F32)
            h_p, route_p, cnt_p = _route(xp, mod_p, gn, wr, br, zero_counts)
            h_s, route_s, cnt_s = _route(xs, mod_s, gn, wr, br, cnt_p)
            h_all = jnp.concatenate([h_p, h_s], axis=0)
            route = jnp.concatenate([route_p, route_s], axis=0)
            te, tr, src, slot1, slot2, n_tiles = _moe_layout(route, cnt_s)
            e_out = _moe_experts(h_all, te, tr, src, moe_w_gate[i], moe_w_up[i], moe_w_down[i], n_tiles)
            n_p = bp * tp
            last = l == depth - 1
            assert last, "the combine kernel also applies the final norm"
            gf = norm_final.reshape(1, d)
            y_p = _combine(xp, mod_p, route_p, slot1[:n_p], slot2[:n_p], e_out, gf)
            y_s = _combine(xs, mod_s, route_s, slot1[n_p:], slot2[n_p:], e_out, gf)

    return (y_p, y_s, jnp.stack(pool_p), jnp.stack(re_p), jnp.stack(im_p),
            jnp.stack(pool_s), jnp.stack(re_s), jnp.stack(im_s))
```

```python
import functools
import math

import jax
import jax.numpy as jnp
import numpy as np
from jax import lax
from jax.experimental import pallas as pl
from jax.experimental.pallas import tpu as pltpu

F32 = jnp.float32
BF16 = jnp.bfloat16
I32 = jnp.int32

EPS = 1e-6
POOL_WINDOWS = (2, 4, 8, 16)
POOL_BUF = max(POOL_WINDOWS) - 1
SUBLANES_V7X = 8
LANES_V7X = 128
HALO = -(-POOL_BUF // SUBLANES_V7X) * SUBLANES_V7X
SSM_H = 16
SSM_P = 64
CHUNK = 16
CW = CHUNK * SSM_H
SW = 2 * SSM_P
N_EXPERTS = 8
TOP_K = 2
PAST_LEN = 16384

VMEM_LIMIT_V7X = 56 * 1024 * 1024
ROWS_IN = 512
ROWS_MIX = 256
ROWS_FFN = 512
FF_TILE = 512
ROWS_ROUTE = 512
MOE_TILE_ROWS = 1536
MOE_SUB = 128
MOE_BLK = 512
MOE_FF_TILE = 256
ROWS_COMBINE = 256
ADA_TN = 1024


def _cparams(sem):
    return pltpu.CompilerParams(dimension_semantics=sem, vmem_limit_bytes=VMEM_LIMIT_V7X)


def _tile(b, t, rows):
    if t >= rows:
        assert t % rows == 0
        return 1, rows
    nb = max(1, min(b, rows // t))
    assert b % nb == 0
    return nb, t


def _modnorm(x, g, s, sh):
    r = lax.rsqrt(jnp.mean(x * x, axis=-1, keepdims=True) + EPS)
    return (x * r) * g * (1.0 + s) + sh


def _dot(a, b):
    return jnp.dot(a, b, preferred_element_type=F32)


def _mod_spec(nb, d, k):
    return pl.BlockSpec((nb, 1, d), lambda b, i, k=k: (b, 0, k))


def _ada_kernel(c_ref, w_ref, b_ref, o_ref):
    c = c_ref[...]
    sc = c * jax.nn.sigmoid(c)
    o_ref[0] = _dot(sc.astype(BF16), w_ref[0].astype(BF16)) + b_ref[0]


def _ada(c_all, w_ada, b_ada):
    m, d = c_all.shape
    depth, _, n6 = w_ada.shape
    tn = math.gcd(ADA_TN, n6)
    return pl.pallas_call(
        _ada_kernel,
        grid=(depth, n6 // tn),
        in_specs=[pl.BlockSpec((m, d), lambda l, n: (0, 0)),
                  pl.BlockSpec((1, d, tn), lambda l, n: (l, 0, n)),
                  pl.BlockSpec((1, 1, tn), lambda l, n: (l, 0, n))],
        out_specs=pl.BlockSpec((1, m, tn), lambda l, n: (l, 0, n)),
        out_shape=jax.ShapeDtypeStruct((depth, m, n6), F32),
        compiler_params=_cparams(("arbitrary", "arbitrary")),
        name="adaln_mod",
    )(c_all, w_ada, b_ada.reshape(depth, 1, n6))


def _in_kernel(x_ref, g_ref, s_ref, sh_ref, w_ref, u_ref):
    nb, tt, d = x_ref.shape
    h = _modnorm(x_ref[...], g_ref[...], s_ref[...], sh_ref[...])
    u = _dot(h.reshape(nb * tt, d).astype(BF16), w_ref[...])
    u_ref[...] = u.reshape(nb, tt, u_ref.shape[-1])


def _proj_in(x, mod, g, w_bf16):
    b, t, d = x.shape
    n = w_bf16.shape[1]
    nb, tt = _tile(b, t, ROWS_IN)
    return pl.pallas_call(
        _in_kernel,
        grid=(b // nb, t // tt),
        in_specs=[pl.BlockSpec((nb, tt, d), lambda bi, i: (bi, i, 0)),
                  pl.BlockSpec((1, d), lambda bi, i: (0, 0)),
                  _mod_spec(nb, d, 1), _mod_spec(nb, d, 0),
                  pl.BlockSpec((d, n), lambda bi, i: (0, 0))],
        out_specs=pl.BlockSpec((nb, tt, n), lambda bi, i: (bi, i, 0)),
        out_shape=jax.ShapeDtypeStruct((b, t, n), F32),
        compiler_params=_cparams(("arbitrary", "arbitrary")),
        name="norm_proj_in",
    )(x, g, mod, mod, w_bf16)


GROUPS_PER_LANE_TILE = LANES_V7X // SSM_H


def _expanders():
    toe = np.zeros((CHUNK, CW, CW), np.float32)
    e_s = np.zeros((CHUNK, CW), np.float32)
    e_h = np.zeros((SSM_H, CW), np.float32)
    for s in range(CHUNK):
        for h in range(SSM_H):
            e_s[s, s * SSM_H + h] = 1.0
            e_h[h, s * SSM_H + h] = 1.0
            for t in range(s, CHUNK):
                toe[t, (t - s) * SSM_H + h, s * SSM_H + h] = 1.0
    return jnp.asarray(toe), jnp.asarray(e_s), jnp.asarray(e_h)


def _s5_params(lam_re, lam_im, log_step, b_re, b_im, c_re, c_im, ts, bs):
    hp = lax.Precision.HIGHEST
    toe, e_s, e_h = _expanders()
    lr, li = lam_re.astype(F32), lam_im.astype(F32)
    step = jnp.exp(log_step.astype(F32))[:, None]
    mag = jnp.exp(lr * step)
    a_re = mag * jnp.cos(li * step)
    a_im = mag * jnp.sin(li * step)
    den = lr * lr + li * li
    nr = a_re - 1.0
    f_re = (nr * lr + a_im * li) / den
    f_im = (a_im * lr - nr * li) / den
    br, bi = b_re.astype(F32), b_im.astype(F32)
    bb_re = f_re[..., None] * br - f_im[..., None] * bi
    bb_im = f_re[..., None] * bi + f_im[..., None] * br
    cr, ci = c_re.astype(F32), c_im.astype(F32)
    g = lr.shape[0]

    pw_re, pw_im = [jnp.ones_like(a_re)], [jnp.zeros_like(a_im)]
    for _ in range(CHUNK):
        pr, pi = pw_re[-1], pw_im[-1]
        pw_re.append(pr * a_re - pi * a_im)
        pw_im.append(pr * a_im + pi * a_re)
    pw_re, pw_im = jnp.stack(pw_re), jnp.stack(pw_im)

    bbt_re, bbt_im = bb_re.transpose(0, 2, 1), bb_im.transpose(0, 2, 1)
    w_re = pw_re[:CHUNK, :, None, :] * bbt_re - pw_im[:CHUNK, :, None, :] * bbt_im
    w_im = pw_re[:CHUNK, :, None, :] * bbt_im + pw_im[:CHUNK, :, None, :] * bbt_re
    m = (jnp.einsum('gop,lgip->goli', cr, w_re, precision=hp)
         - jnp.einsum('gop,lgip->goli', ci, w_im, precision=hp)).reshape(g, SSM_H, CW)
    k_t = jnp.einsum('gok,tkn->gton', m, toe, precision=hp).reshape(g, CW, CW)

    rev_re, rev_im = pw_re[:CHUNK][::-1], pw_im[:CHUNK][::-1]
    ar = jnp.einsum('sgp,sn->gpn', rev_re, e_s, precision=hp)
    ai = jnp.einsum('sgp,sn->gpn', rev_im, e_s, precision=hp)
    xr = jnp.einsum('gpi,in->gpn', bb_re, e_h, precision=hp)
    xi = jnp.einsum('gpi,in->gpn', bb_im, e_h, precision=hp)
    bt_re = ar * xr - ai * xi
    bt_im = ar * xi + ai * xr
    b_t = jnp.concatenate([bt_re, bt_im, bt_im, bt_re], axis=1)

    qr = pw_re[1:].transpose(1, 0, 2)[:, :, None, :]
    qi = pw_im[1:].transpose(1, 0, 2)[:, :, None, :]
    cm_re = cr[:, None] * qr - ci[:, None] * qi
    cm_im = -(cr[:, None] * qi + ci[:, None] * qr)
    c_t = jnp.concatenate([cm_re, cm_im], axis=-1).reshape(g, CW, SW)

    def coef(k):
        pr, pi = pw_re[k], pw_im[k]
        return (jnp.concatenate([pr, pr, pr, pr], axis=-1),
                jnp.concatenate([-pi, pi, pi, -pi], axis=-1))
    p16, q16 = coef(CHUNK)
    p8, q8 = coef(ts)
    p8b = jnp.broadcast_to(p8[:, :SW, None], (g, SW, bs))
    q8b = jnp.broadcast_to(q8[:, :SW, None], (g, SW, bs))
    return k_t.astype(BF16), b_t.astype(BF16), c_t.astype(BF16), p16, q16, p8b, q8b


def _s5_prompt_kernel(us_ref, kt_ref, bt_ref, ct_ref, p_ref, q_ref, y_ref, xe_ref, ut, yt, scan_x, scan_xs, s_scr):
    gt = ut.shape[0]
    nchunk = ut.shape[-1]
    for t in range(CHUNK):
        slab = us_ref[0, pl.ds(t, nchunk, stride=CHUNK), :]
        ut[:, t] = slab.T.reshape(gt, SSM_H, nchunk).astype(BF16)
    for gi in range(gt):
        buz = _dot(bt_ref[gi], ut[gi].reshape(CW, nchunk))
        buz_t = buz.T
        scan_x[pl.ds(gi, nchunk, stride=gt), :] = buz_t[:, :SW]
        scan_xs[pl.ds(gi, nchunk, stride=gt), :] = buz_t[:, SW:]
    px, pxs = p_ref[:, :SW], p_ref[:, SW:]
    qx, qxs = q_ref[:, :SW], q_ref[:, SW:]
    x = jnp.zeros((gt, SW), F32)
    xs = jnp.zeros((gt, SW), F32)
    for c in range(nchunk):
        lo, hi = c * gt, (c + 1) * gt
        s_scr[lo:hi, :] = x
        x, xs = (px * x + qx * xs + scan_x[lo:hi, :],
                 pxs * xs + qxs * x + scan_xs[lo:hi, :])
    xe_ref[0] = x
    for gi in range(gt):
        st = s_scr[pl.ds(gi, nchunk, stride=gt), :].T.astype(BF16)
        y_t = _dot(kt_ref[gi], ut[gi].reshape(CW, nchunk)) + _dot(ct_ref[gi], st)
        yt[gi] = y_t.reshape(CHUNK, SSM_H, nchunk)
    for t in range(CHUNK):
        y_ref[0, pl.ds(t, nchunk, stride=CHUNK), :] = yt[:, t].reshape(gt * SSM_H, nchunk).T


def _s5_prompt(u, k_t, b_t, c_t, p16, q16, c_off):
    b, t, _ = u.shape
    g = k_t.shape[0]
    gt = GROUPS_PER_LANE_TILE
    lt = gt * SSM_H
    nchunk = t // CHUNK
    off = c_off // lt

    def wspec(r, c):
        return pl.BlockSpec((gt, r, c), lambda bi, j: (j, 0, 0))
    return pl.pallas_call(
        _s5_prompt_kernel,
        grid=(b, g // gt),
        in_specs=[pl.BlockSpec((1, t, lt), lambda bi, j: (bi, 0, off + j)),
                  wspec(CW, CW), wspec(2 * SW, CW), wspec(CW, SW),
                  pl.BlockSpec((gt, 2 * SW), lambda bi, j: (j, 0)),
                  pl.BlockSpec((gt, 2 * SW), lambda bi, j: (j, 0))],
        out_specs=[pl.BlockSpec((1, t, lt), lambda bi, j: (bi, 0, j)),
                   pl.BlockSpec((1, gt, SW), lambda bi, j: (bi, j, 0))],
        out_shape=[jax.ShapeDtypeStruct((b, t, g * SSM_H), F32), jax.ShapeDtypeStruct((b, g, SW), F32)],
        scratch_shapes=[pltpu.VMEM((gt, CHUNK, SSM_H, nchunk), BF16), pltpu.VMEM((gt, CHUNK, SSM_H, nchunk), F32),
                        pltpu.VMEM((nchunk * gt, SW), F32), pltpu.VMEM((nchunk * gt, SW), F32),
                        pltpu.VMEM((nchunk * gt, SW), F32)],
        compiler_params=_cparams(("arbitrary", "arbitrary")),
        name="s5_prompt",
    )(u, k_t, b_t, c_t, p16, q16)


def _s5_sample_kernel(us_ref, x0_ref, kt_ref, bt_ref, ct_ref, p_ref, q_ref, y_ref, x1_ref, ut, yt, *, ts):
    gt = ut.shape[0]
    bs = ut.shape[-1]
    w = ts * SSM_H
    for t in range(ts):
        slab = us_ref[pl.ds(t, bs, stride=ts), :]
        ut[:, t] = slab.T.reshape(gt, SSM_H, bs).astype(BF16)
    for gi in range(gt):
        ug = ut[gi].reshape(w, bs)
        x0 = x0_ref[gi]
        x0s = jnp.concatenate([x0[SSM_P:], x0[:SSM_P]], axis=0)
        y_t = _dot(kt_ref[gi, :w, :w], ug) + _dot(ct_ref[gi, :w, :], x0.astype(BF16))
        x1_ref[gi] = p_ref[gi] * x0 + q_ref[gi] * x0s + _dot(bt_ref[gi, :SW, CW - w:], ug)
        yt[gi] = y_t.reshape(ts, SSM_H, bs)
    for t in range(ts):
        y_ref[pl.ds(t, bs, stride=ts), :] = yt[:, t].reshape(gt * SSM_H, bs).T


def _s5_sample(u2, x0_t, k_t, b_t, c_t, p8b, q8b, ts, c_off):
    rows, _ = u2.shape
    g, _, bs = x0_t.shape
    gt = GROUPS_PER_LANE_TILE
    lt = gt * SSM_H
    off = c_off // lt

    def wspec(r, c):
        return pl.BlockSpec((gt, r, c), lambda j: (j, 0, 0))
    return pl.pallas_call(
        functools.partial(_s5_sample_kernel, ts=ts),
        grid=(g // gt,),
        in_specs=[pl.BlockSpec((rows, lt), lambda j: (0, off + j)),
                  wspec(SW, bs), wspec(CW, CW), wspec(2 * SW, CW), wspec(CW, SW), wspec(SW, bs), wspec(SW, bs)],
        out_specs=[pl.BlockSpec((rows, lt), lambda j: (0, j)), wspec(SW, bs)],
        out_shape=[jax.ShapeDtypeStruct((rows, g * SSM_H), F32), jax.ShapeDtypeStruct((g, SW, bs), F32)],
        scratch_shapes=[pltpu.VMEM((gt, ts, SSM_H, bs), BF16), pltpu.VMEM((gt, ts, SSM_H, bs), F32)],
        compiler_params=_cparams(("arbitrary",)),
        name="s5_sample",
    )(u2, x0_t, k_t, b_t, c_t, p8b, q8b)


def _mix_kernel(x_ref, up_ref, us_ref, ys_ref, st_ref, g1_ref, wp_ref, ps_ref, dsk_ref, wglu_ref,
                wout_ref, o_ref, ext, *, start_pos, carry):
    nb, tt, d = x_ref.shape
    c = up_ref.shape[-1]
    pg = c // len(POOL_WINDOWS)
    rows = nb * tt
    i = pl.program_id(1)

    @pl.when(i == 0)
    def _():
        ext[:, 0:HALO, :] = st_ref[...]
    ext[:, HALO:HALO + tt, :] = up_ref[...]

    pos = start_pos + i * tt + lax.broadcasted_iota(I32, (1, tt, 1), 1)
    outs = []
    for gidx, w in enumerate(POOL_WINDOWS):
        c0 = gidx * pg
        cur = ext[:, HALO:HALO + tt, c0:c0 + pg]
        s = cur
        for k in range(1, w):
            s = s + ext[:, HALO - k:HALO - k + tt, c0:c0 + pg]
        cnt = jnp.minimum(pos + 1, w).astype(F32)
        dd = s / cnt - cur
        yg = _dot(dd.reshape(rows, pg).astype(BF16), wp_ref[gidx])
        outs.append(yg * ps_ref[:, c0:c0 + pg])
    y_pool = jnp.concatenate(outs, axis=-1)

    if carry:
        ext[:, 0:HALO, :] = ext[:, tt:tt + HALO, :]

    us = us_ref[...].reshape(rows, c)
    yf = ys_ref[...].reshape(rows, c) + dsk_ref[...] * us
    gl = jax.nn.gelu(yf, approximate=True)
    o = gl * jax.nn.sigmoid(_dot(gl.astype(BF16), wglu_ref[...]))
    m = jnp.concatenate([y_pool, o], axis=-1).astype(BF16)
    mo = _dot(m, wout_ref[...])
    o_ref[...] = x_ref[...] + g1_ref[...] * mo.reshape(nb, tt, d)


def _mix(x, u, y_ssm, pool_state16, mod, wp, ps, dsk, wglu, wout, start_pos):
    b, t, d = x.shape
    c = y_ssm.shape[-1]
    nb, tt = _tile(b, t, ROWS_MIX)
    nt = t // tt
    assert nt == 1 or tt >= HALO
    const2 = lambda bi, i: (0, 0)
    return pl.pallas_call(
        functools.partial(_mix_kernel, start_pos=start_pos, carry=nt > 1),
        grid=(b // nb, nt),
        in_specs=[pl.BlockSpec((nb, tt, d), lambda bi, i: (bi, i, 0)),
                  pl.BlockSpec((nb, tt, c), lambda bi, i: (bi, i, 0)),
                  pl.BlockSpec((nb, tt, c), lambda bi, i: (bi, i, 1)),
                  pl.BlockSpec((nb, tt, c), lambda bi, i: (bi, i, 0)),
                  pl.BlockSpec((nb, HALO, c), lambda bi, i: (bi, 0, 0)),
                  _mod_spec(nb, d, 2),
                  pl.BlockSpec(wp.shape, lambda bi, i: (0, 0, 0)),
                  pl.BlockSpec((1, c), const2), pl.BlockSpec((1, c), const2),
                  pl.BlockSpec((c, c), const2), pl.BlockSpec((d, d), const2)],
        out_specs=pl.BlockSpec((nb, tt, d), lambda bi, i: (bi, i, 0)),
        out_shape=jax.ShapeDtypeStruct((b, t, d), F32),
        scratch_shapes=[pltpu.VMEM((nb, HALO + tt, c), F32)],
        compiler_params=_cparams(("arbitrary", "arbitrary")),
        name="pool_glu_proj_out",
    )(x, u, u, y_ssm, pool_state16, mod, wp, ps, dsk, wglu, wout)


def _ffn_kernel(x_ref, g_ref, s_ref, sh_ref, g2_ref, wg_ref, wu_ref, wd_ref, o_ref, hb, acc):
    nb, tt, d = x_ref.shape
    j = pl.program_id(2)

    @pl.when(j == 0)
    def _():
        h = _modnorm(x_ref[...], g_ref[...], s_ref[...], sh_ref[...])
        hb[...] = h.reshape(nb * tt, d).astype(BF16)
        acc[...] = jnp.zeros_like(acc)

    h = hb[...]
    gate = _dot(h, wg_ref[...])
    up = _dot(h, wu_ref[...])
    a = (gate * jax.nn.sigmoid(gate) * up).astype(BF16)
    acc[...] += _dot(a, wd_ref[...])

    @pl.when(j == pl.num_programs(2) - 1)
    def _():
        o_ref[...] = x_ref[...] + g2_ref[...] * acc[...].reshape(nb, tt, d)


def _ffn(x, mod, g, wg, wu, wd):
    b, t, d = x.shape
    ff = wg.shape[1]
    nb, tt = _tile(b, t, ROWS_FFN)
    tf = min(FF_TILE, ff)
    assert ff % tf == 0
    return pl.pallas_call(
        _ffn_kernel,
        grid=(b // nb, t // tt, ff // tf),
        in_specs=[pl.BlockSpec((nb, tt, d), lambda bi, i, j: (bi, i, 0)),
                  pl.BlockSpec((1, d), lambda bi, i, j: (0, 0)),
                  pl.BlockSpec((nb, 1, d), lambda bi, i, j: (bi, 0, 4)),
                  pl.BlockSpec((nb, 1, d), lambda bi, i, j: (bi, 0, 3)),
                  pl.BlockSpec((nb, 1, d), lambda bi, i, j: (bi, 0, 5)),
                  pl.BlockSpec((d, tf), lambda bi, i, j: (0, j)),
                  pl.BlockSpec((d, tf), lambda bi, i, j: (0, j)),
                  pl.BlockSpec((tf, d), lambda bi, i, j: (j, 0))],
        out_specs=pl.BlockSpec((nb, tt, d), lambda bi, i, j: (bi, i, 0)),
        out_shape=jax.ShapeDtypeStruct((b, t, d), F32),
        scratch_shapes=[pltpu.VMEM((nb * tt, d), BF16), pltpu.VMEM((nb * tt, d), F32)],
        compiler_params=_cparams(("arbitrary", "arbitrary", "arbitrary")),
        name="dense_swiglu",
    )(x, g, mod, mod, mod, wg, wu, wd)


ROUTE_LANES = LANES_V7X


def _route_kernel(x_ref, g_ref, s_ref, sh_ref, wr_ref, br_ref, cin_ref, h_ref, r_ref, cout_ref, carry):
    nb, tt, d = x_ref.shape
    tm = nb * tt
    first = jnp.logical_and(pl.program_id(0) == 0, pl.program_id(1) == 0)

    @pl.when(first)
    def _():
        carry[...] = cin_ref[...]

    h = _modnorm(x_ref[...], g_ref[...], s_ref[...], sh_ref[...]).reshape(tm, d)
    h_ref[...] = h
    logits = jnp.dot(h, wr_ref[...], preferred_element_type=F32, precision=lax.Precision.HIGHEST) + br_ref[...]
    lane = lax.broadcasted_iota(I32, (tm, ROUTE_LANES), 1)
    neg = jnp.float32(-jnp.inf)
    l1 = jnp.where(lane < N_EXPERTS, logits, neg)
    m1 = jnp.max(l1, axis=-1, keepdims=True)
    i1 = jnp.min(jnp.where(l1 == m1, lane, ROUTE_LANES), axis=-1, keepdims=True)
    l2 = jnp.where(lane == i1, neg, l1)
    m2 = jnp.max(l2, axis=-1, keepdims=True)
    i2 = jnp.min(jnp.where(l2 == m2, lane, ROUTE_LANES), axis=-1, keepdims=True)
    e2 = jnp.exp(m2 - m1)
    den = 1.0 + e2
    p1 = 1.0 / den
    p2 = e2 / den

    onehot = jnp.logical_or(lane == i1, lane == i2).astype(F32)
    rr = lax.broadcasted_iota(I32, (tm, tm), 0)
    cc = lax.broadcasted_iota(I32, (tm, tm), 1)
    before = (cc < rr).astype(BF16)
    rank = _dot(before, onehot.astype(BF16)) + carry[...]
    r1 = jnp.sum(jnp.where(lane == i1, rank, 0.0), axis=-1, keepdims=True)
    r2 = jnp.sum(jnp.where(lane == i2, rank, 0.0), axis=-1, keepdims=True)
    carry[...] += jnp.sum(onehot, axis=0, keepdims=True)

    out = jnp.where(lane == 0, i1.astype(F32), 0.0)
    out = jnp.where(lane == 1, i2.astype(F32), out)
    out = jnp.where(lane == 2, p1, out)
    out = jnp.where(lane == 3, p2, out)
    out = jnp.where(lane == 4, r1, out)
    out = jnp.where(lane == 5, r2, out)
    r_ref[...] = out
    cout_ref[...] = carry[...]


def _route(x, mod, g, wr_pad, br_pad, counts_in):
    b, t, d = x.shape
    nb, tt = _tile(b, t, ROWS_ROUTE)
    tm = nb * tt
    nt = t // tt
    const2 = lambda bi, i: (0, 0)
    return pl.pallas_call(
        _route_kernel,
        grid=(b // nb, nt),
        in_specs=[pl.BlockSpec((nb, tt, d), lambda bi, i: (bi, i, 0)),
                  pl.BlockSpec((1, d), const2),
                  _mod_spec(nb, d, 4), _mod_spec(nb, d, 3),
                  pl.BlockSpec((d, ROUTE_LANES), const2),
                  pl.BlockSpec((1, ROUTE_LANES), const2),
                  pl.BlockSpec((1, ROUTE_LANES), const2)],
        out_specs=[pl.BlockSpec((tm, d), lambda bi, i: (bi * nt + i, 0)),
                   pl.BlockSpec((tm, ROUTE_LANES), lambda bi, i: (bi * nt + i, 0)),
                   pl.BlockSpec((1, ROUTE_LANES), const2)],
        out_shape=[jax.ShapeDtypeStruct((b * t, d), F32),
                   jax.ShapeDtypeStruct((b * t, ROUTE_LANES), F32),
                   jax.ShapeDtypeStruct((1, ROUTE_LANES), F32)],
        scratch_shapes=[pltpu.VMEM((1, ROUTE_LANES), F32)],
        compiler_params=_cparams(("arbitrary", "arbitrary")),
        name="moe_route",
    )(x, g, mod, mod, wr_pad, br_pad, counts_in)


def _row_gather_copy(src_hbm, tok, dst, r, sem):
    return pltpu.make_async_copy(src_hbm.at[pl.ds(tok, 1), :], dst.at[pl.ds(r, 1), :], sem)


def _moe_kernel(te_ref, tr_ref, src_ref, h_hbm, wg_ref, wu_ref, wd_ref, o_hbm,
                xb, acc, stage, wgb, wub, wdb, gsem, osem):
    s = pl.program_id(0)
    j = pl.program_id(1)
    ns = pl.num_programs(0)
    nj = pl.num_programs(1)
    rows = tr_ref[s]
    nsub = (rows + MOE_SUB - 1) // MOE_SUB
    tile_rows = xb.shape[0]

    def out_copy():
        return pltpu.make_async_copy(acc, o_hbm.at[pl.ds(pl.multiple_of(s * tile_rows, MOE_SUB), tile_rows), :], osem)

    @pl.when(j == 0)
    def _():
        @pl.when(rows > 0)
        def _gather():
            def issue(sub, slot):
                base = s * tile_rows + sub * MOE_SUB

                def body(r, carry):
                    _row_gather_copy(h_hbm, src_ref[base + r], stage.at[slot], r, gsem.at[slot]).start()
                    return carry
                lax.fori_loop(0, MOE_SUB, body, 0, unroll=8)

            issue(0, 0)

            def sub_body(sub, carry):
                slot = sub % 2

                @pl.when(sub + 1 < nsub)
                def _():
                    issue(sub + 1, 1 - slot)
                pltpu.make_async_copy(h_hbm.at[pl.ds(0, MOE_SUB), :], stage.at[slot], gsem.at[slot]).wait()
                r0 = pl.multiple_of(sub * MOE_SUB, MOE_SUB)
                xb[pl.ds(r0, MOE_SUB), :] = stage[slot].astype(BF16)
                return carry
            lax.fori_loop(0, nsub, sub_body, 0)

        @pl.when(s > 0)
        def _():
            out_copy().wait()
        acc[...] = jnp.zeros_like(acc)

    @pl.when(rows > 0)
    def _():
        wgb[...] = wg_ref[...].astype(BF16)
        wub[...] = wu_ref[...].astype(BF16)
        wdb[...] = wd_ref[...].astype(BF16)

        def block(r0, m):
            xs = xb[pl.ds(r0, m), :]
            gate = _dot(xs, wgb[...])
            up = _dot(xs, wub[...])
            a = (gate * jax.nn.sigmoid(gate) * up).astype(BF16)
            acc[pl.ds(r0, m), :] += _dot(a, wdb[...])

        per_blk = MOE_BLK // MOE_SUB
        nblk = nsub // per_blk
        rem = nsub - nblk * per_blk

        def body(blk, carry):
            block(pl.multiple_of(blk * MOE_BLK, MOE_BLK), MOE_BLK)
            return carry
        lax.fori_loop(0, nblk, body, 0)
        for k in range(1, per_blk):
            @pl.when(rem == k)
            def _(k=k):
                block(pl.multiple_of(nblk * MOE_BLK, MOE_BLK), k * MOE_SUB)

    @pl.when(j == nj - 1)
    def _():
        out_copy().start()

        @pl.when(s == ns - 1)
        def _():
            out_copy().wait()


def _moe_experts(h_all, te, tr, src, wg, wu, wd, n_tiles):
    n, d = h_all.shape
    e, _, ff = wg.shape
    tf = min(MOE_FF_TILE, ff)
    assert ff % tf == 0 and MOE_TILE_ROWS % MOE_BLK == 0 and MOE_BLK % MOE_SUB == 0
    nj = ff // tf
    r = MOE_TILE_ROWS

    def jj(s, j, tr_ref):
        return jnp.where(tr_ref[s] > 0, j, nj - 1)

    grid_spec = pltpu.PrefetchScalarGridSpec(
        num_scalar_prefetch=3,
        grid=(n_tiles, nj),
        in_specs=[pl.BlockSpec(memory_space=pl.ANY),
                  pl.BlockSpec((None, d, tf), lambda s, j, te_r, tr_r, src_r: (te_r[s], 0, jj(s, j, tr_r))),
                  pl.BlockSpec((None, d, tf), lambda s, j, te_r, tr_r, src_r: (te_r[s], 0, jj(s, j, tr_r))),
                  pl.BlockSpec((None, tf, d), lambda s, j, te_r, tr_r, src_r: (te_r[s], jj(s, j, tr_r), 0))],
        out_specs=pl.BlockSpec(memory_space=pl.ANY),
        scratch_shapes=[pltpu.VMEM((r, d), BF16), pltpu.VMEM((r, d), F32), pltpu.VMEM((2, MOE_SUB, d), F32),
                        pltpu.VMEM((d, tf), BF16), pltpu.VMEM((d, tf), BF16), pltpu.VMEM((tf, d), BF16),
                        pltpu.SemaphoreType.DMA((2,)), pltpu.SemaphoreType.DMA(())],
    )
    return pl.pallas_call(
        _moe_kernel,
        grid_spec=grid_spec,
        out_shape=jax.ShapeDtypeStruct((n_tiles * r, d), F32),
        compiler_params=_cparams(("arbitrary", "arbitrary")),
        name="moe_experts",
    )(te, tr, src, h_all, wg, wu, wd)


def _combine_kernel(s1_ref, s2_ref, x_ref, g2_ref, r_ref, gf_ref, e_hbm, o_ref, st1, st2, sem):
    nb, tt, d = x_ref.shape
    tm = nb * tt
    base = (pl.program_id(0) * pl.num_programs(1) + pl.program_id(1)) * tm

    def body(r, carry):
        _row_gather_copy(e_hbm, s1_ref[base + r], st1, r, sem.at[0]).start()
        _row_gather_copy(e_hbm, s2_ref[base + r], st2, r, sem.at[1]).start()
        return carry
    lax.fori_loop(0, tm, body, 0, unroll=8)
    pltpu.make_async_copy(e_hbm.at[pl.ds(0, tm), :], st1, sem.at[0]).wait()
    pltpu.make_async_copy(e_hbm.at[pl.ds(0, tm), :], st2, sem.at[1]).wait()

    f = r_ref[:, 2:3] * st1[...] + r_ref[:, 3:4] * st2[...]
    x2 = x_ref[...] + g2_ref[...] * f.reshape(nb, tt, d)
    rs = lax.rsqrt(jnp.mean(x2 * x2, axis=-1, keepdims=True) + EPS)
    o_ref[...] = (x2 * rs) * gf_ref[...]


def _combine(x, mod, route, slot1, slot2, e_out, gf):
    b, t, d = x.shape
    nb, tt = _tile(b, t, ROWS_COMBINE)
    tm = nb * tt
    nt = t // tt
    grid_spec = pltpu.PrefetchScalarGridSpec(
        num_scalar_prefetch=2,
        grid=(b // nb, nt),
        in_specs=[pl.BlockSpec((nb, tt, d), lambda bi, i, a, c: (bi, i, 0)),
                  pl.BlockSpec((nb, 1, d), lambda bi, i, a, c: (bi, 0, 5)),
                  pl.BlockSpec((tm, ROUTE_LANES), lambda bi, i, a, c: (bi * nt + i, 0)),
                  pl.BlockSpec((1, d), lambda bi, i, a, c: (0, 0)),
                  pl.BlockSpec(memory_space=pl.ANY)],
        out_specs=pl.BlockSpec((nb, tt, d), lambda bi, i, a, c: (bi, i, 0)),
        scratch_shapes=[pltpu.VMEM((tm, d), F32), pltpu.VMEM((tm, d), F32), pltpu.SemaphoreType.DMA((2,))],
    )
    return pl.pallas_call(
        _combine_kernel,
        grid_spec=grid_spec,
        out_shape=jax.ShapeDtypeStruct((b, t, d), F32),
        compiler_params=_cparams(("arbitrary", "arbitrary")),
        name="moe_combine_norm",
    )(slot1, slot2, x, mod, route, gf, e_out)


def _moe_layout(route, counts_f):
    n = route.shape[0]
    r, sub = MOE_TILE_ROWS, MOE_SUB
    n_tiles = -(-TOP_K * n // r) + N_EXPERTS
    i1 = route[:, 0].astype(I32)
    i2 = route[:, 1].astype(I32)
    r1 = route[:, 4].astype(I32)
    r2 = route[:, 5].astype(I32)
    counts = counts_f[0, :N_EXPERTS].astype(I32)
    nt = (counts + r - 1) // r
    ntc = jnp.maximum(nt, 1)
    sz = (((counts + ntc - 1) // ntc + sub - 1) // sub) * sub
    cum = jnp.cumsum(nt)
    tstart = cum - nt
    total = cum[-1]
    s_idx = jnp.arange(n_tiles, dtype=I32)
    te_raw = jnp.minimum(jnp.sum((s_idx[:, None] >= cum[None, :]).astype(I32), axis=1), N_EXPERTS - 1)
    used = s_idx < total
    last_e = te_raw[jnp.maximum(total - 1, 0)]
    te = jnp.where(used, te_raw, last_e)
    k_in = s_idx - tstart[te]
    tr = jnp.where(used, jnp.clip(counts[te] - k_in * sz[te], 0, sz[te]), 0).astype(I32)

    def slot(ei, ri):
        return (tstart[ei] + ri // sz[ei]) * r + ri % sz[ei]
    slot1, slot2 = slot(i1, r1), slot(i2, r2)
    tok = jnp.arange(n, dtype=I32)
    src = jnp.zeros((n_tiles * r,), I32).at[jnp.concatenate([slot1, slot2])].set(jnp.concatenate([tok, tok]))
    return te, tr, src, slot1, slot2, n_tiles


def kernel(x_prompt, x_sample, c_prompt, c_sample, state_pool, state_ssm_re, state_ssm_im, w_ada, b_ada, norm_mix, norm_ffn, w_in, w_pool, pool_scale, ssm_lam_re, ssm_lam_im, ssm_log_step, ssm_b_re, ssm_b_im, ssm_c_re, ssm_c_im, ssm_d, ssm_w_glu, w_out, ffn_w_gate, ffn_w_up, ffn_w_down, moe_w_router, moe_b_router, moe_w_gate, moe_w_up, moe_w_down, norm_final):
    bp, tp, d = x_prompt.shape
    bs, ts, _ = x_sample.shape
    depth = w_ada.shape[0]
    c_pool = w_pool.shape[1] * w_pool.shape[2]
    assert tp % CHUNK == 0 and ts <= CHUNK

    m_rows = -(-(bp + bs) // SUBLANES_V7X) * SUBLANES_V7X
    c_all = jnp.concatenate([c_prompt, c_sample, jnp.zeros((m_rows - bp - bs, d), F32)], axis=0)
    mod_all = _ada(c_all, w_ada, b_ada)

    xp, xs = x_prompt, x_sample
    pool_p, re_p, im_p, pool_s, re_s, im_s = [], [], [], [], [], []
    y_p = y_s = None
    for l in range(depth):
        mod_p = mod_all[l, :bp].reshape(bp, 1, 6 * d)
        mod_s = mod_all[l, bp:bp + bs].reshape(bs, 1, 6 * d)
        gm = norm_mix[l].reshape(1, d)
        gn = norm_ffn[l].reshape(1, d)

        w_in_b = w_in[l].astype(BF16)
        up_all = _proj_in(xp, mod_p, gm, w_in_b)
        us_all = _proj_in(xs, mod_s, gm, w_in_b)
        k_t, b_t, c_t, p16, q16, p8b, q8b = _s5_params(ssm_lam_re[l], ssm_lam_im[l], ssm_log_step[l], ssm_b_re[l],
                                                       ssm_b_im[l], ssm_c_re[l], ssm_c_im[l], ts, bs)
        x0_t = jnp.concatenate([state_ssm_re[l], state_ssm_im[l]], axis=-1).astype(F32).transpose(1, 2, 0)
        y_ssm_p, xend = _s5_prompt(up_all, k_t, b_t, c_t, p16, q16, c_pool)
        ys2, x1_t = _s5_sample(us_all.reshape(bs * ts, d), x0_t, k_t, b_t, c_t, p8b, q8b, ts, c_pool)
        y_ssm_s = ys2.reshape(bs, ts, d - c_pool)

        wp_b = w_pool[l].astype(BF16)
        ps = pool_scale[l].reshape(1, c_pool).astype(F32)
        dsk = ssm_d[l].reshape(1, -1).astype(F32)
        wglu_b = ssm_w_glu[l].astype(BF16)
        wout_b = w_out[l].astype(BF16)
        st_p = jnp.zeros((bp, HALO, c_pool), F32)
        st_s = jnp.concatenate([jnp.zeros((bs, HALO - POOL_BUF, c_pool), F32), state_pool[l].astype(F32)], axis=1)
        xp = _mix(xp, up_all, y_ssm_p, st_p, mod_p, wp_b, ps, dsk, wglu_b, wout_b, 0)
        xs = _mix(xs, us_all, y_ssm_s, st_s, mod_s, wp_b, ps, dsk, wglu_b, wout_b, PAST_LEN)

        pool_p.append(up_all[:, tp - POOL_BUF:, :c_pool])
        pool_s.append(jnp.concatenate([state_pool[l], us_all[:, :, :c_pool]], axis=1)[:, -POOL_BUF:])
        re_p.append(xend[..., :SSM_P])
        im_p.append(xend[..., SSM_P:])
        re_s.append(x1_t[:, :SSM_P, :].transpose(2, 0, 1))
        im_s.append(x1_t[:, SSM_P:, :].transpose(2, 0, 1))

        i = l // 2
        if l % 2 == 0:
            ff = ffn_w_gate.shape[-1]
            ffp = -(-ff // FF_TILE) * FF_TILE
            wg = jnp.pad(ffn_w_gate[i].astype(BF16), ((0, 0), (0, ffp - ff)))
            wu = jnp.pad(ffn_w_up[i].astype(BF16), ((0, 0), (0, ffp - ff)))
            wd = jnp.pad(ffn_w_down[i].astype(BF16), ((0, ffp - ff), (0, 0)))
            xp = _ffn(xp, mod_p, gn, wg, wu, wd)
            xs = _ffn(xs, mod_s, gn, wg, wu, wd)
        else:
            ne = moe_w_router.shape[-1]
            wr = jnp.pad(moe_w_router[i].astype(F32), ((0, 0), (0, ROUTE_LANES - ne)))
            br = jnp.pad(moe_b_router[i].astype(F32), (0, ROUTE_LANES - ne)).reshape(1, ROUTE_LANES)
            zero_counts = jnp.zeros((1, ROUTE_LANES), F32)
            h_p, route_p, cnt_p = _route(xp, mod_p, gn, wr, br, zero_counts)
            h_s, route_s, cnt_s = _route(xs, mod_s, gn, wr, br, cnt_p)
            h_all = jnp.concatenate([h_p, h_s], axis=0)
            route = jnp.concatenate([route_p, route_s], axis=0)
            te, tr, src, slot1, slot2, n_tiles = _moe_layout(route, cnt_s)
            e_out = _moe_experts(h_all, te, tr, src, moe_w_gate[i], moe_w_up[i], moe_w_down[i], n_tiles)
            n_p = bp * tp
            last = l == depth - 1
            assert last, "the combine kernel also applies the final norm"
            gf = norm_final.reshape(1, d)
            y_p = _combine(xp, mod_p, route_p, slot1[:n_p], slot2[:n_p], e_out, gf)
            y_s = _combine(xs, mod_s, route_s, slot1[n_p:], slot2[n_p:], e_out, gf)

    return (y_p, y_s, jnp.stack(pool_p), jnp.stack(re_p), jnp.stack(im_p),
            jnp.stack(pool_s), jnp.stack(re_s), jnp.stack(im_s))
```

```python
import functools
import math

import jax
import jax.numpy as jnp
import numpy as np
from jax import lax
from jax.experimental import pallas as pl
from jax.experimental.pallas import tpu as pltpu

F32 = jnp.float32
BF16 = jnp.bfloat16
I32 = jnp.int32

EPS = 1e-6
POOL_WINDOWS = (2, 4, 8, 16)
POOL_BUF = max(POOL_WINDOWS) - 1
SUBLANES_V7X = 8
LANES_V7X = 128
HALO = -(-POOL_BUF // SUBLANES_V7X) * SUBLANES_V7X
SSM_H = 16
SSM_P = 64
CHUNK = 16
CW = CHUNK * SSM_H
SW = 2 * SSM_P
N_EXPERTS = 8
TOP_K = 2
PAST_LEN = 16384

VMEM_LIMIT_V7X = 56 * 1024 * 1024
ROWS_IN = 512
ROWS_MIX = 256
ROWS_FFN = 512
FF_TILE = 512
ROWS_ROUTE = 512
MOE_TILE_ROWS = 2560
MOE_SUB = 128
MOE_BLK = 512
MOE_FF_TILE = 256
ROWS_COMBINE = 256
ADA_TN = 1024


def _cparams(sem):
    return pltpu.CompilerParams(dimension_semantics=sem, vmem_limit_bytes=VMEM_LIMIT_V7X)


def _tile(b, t, rows):
    if t >= rows:
        assert t % rows == 0
        return 1, rows
    nb = max(1, min(b, rows // t))
    assert b % nb == 0
    return nb, t


def _modnorm(x, g, s, sh):
    r = lax.rsqrt(jnp.mean(x * x, axis=-1, keepdims=True) + EPS)
    return (x * r) * g * (1.0 + s) + sh


def _dot(a, b):
    return jnp.dot(a, b, preferred_element_type=F32)


def _mod_spec(nb, d, k):
    return pl.BlockSpec((nb, 1, d), lambda b, i, k=k: (b, 0, k))


def _ada_kernel(c_ref, w_ref, b_ref, o_ref):
    c = c_ref[...]
    sc = c * jax.nn.sigmoid(c)
    o_ref[0] = _dot(sc.astype(BF16), w_ref[0].astype(BF16)) + b_ref[0]


def _ada(c_all, w_ada, b_ada):
    m, d = c_all.shape
    depth, _, n6 = w_ada.shape
    tn = math.gcd(ADA_TN, n6)
    return pl.pallas_call(
        _ada_kernel,
        grid=(depth, n6 // tn),
        in_specs=[pl.BlockSpec((m, d), lambda l, n: (0, 0)),
                  pl.BlockSpec((1, d, tn), lambda l, n: (l, 0, n)),
                  pl.BlockSpec((1, 1, tn), lambda l, n: (l, 0, n))],
        out_specs=pl.BlockSpec((1, m, tn), lambda l, n: (l, 0, n)),
        out_shape=jax.ShapeDtypeStruct((depth, m, n6), F32),
        compiler_params=_cparams(("arbitrary", "arbitrary")),
        name="adaln_mod",
    )(c_all, w_ada, b_ada.reshape(depth, 1, n6))


def _in_kernel(x_ref, g_ref, s_ref, sh_ref, w_ref, u_ref):
    nb, tt, d = x_ref.shape
    h = _modnorm(x_ref[...], g_ref[...], s_ref[...], sh_ref[...])
    u = _dot(h.reshape(nb * tt, d).astype(BF16), w_ref[...])
    u_ref[...] = u.reshape(nb, tt, u_ref.shape[-1])


def _proj_in(x, mod, g, w_bf16):
    b, t, d = x.shape
    n = w_bf16.shape[1]
    nb, tt = _tile(b, t, ROWS_IN)
    return pl.pallas_call(
        _in_kernel,
        grid=(b // nb, t // tt),
        in_specs=[pl.BlockSpec((nb, tt, d), lambda bi, i: (bi, i, 0)),
                  pl.BlockSpec((1, d), lambda bi, i: (0, 0)),
                  _mod_spec(nb, d, 1), _mod_spec(nb, d, 0),
                  pl.BlockSpec((d, n), lambda bi, i: (0, 0))],
        out_specs=pl.BlockSpec((nb, tt, n), lambda bi, i: (bi, i, 0)),
        out_shape=jax.ShapeDtypeStruct((b, t, n), F32),
        compiler_params=_cparams(("arbitrary", "arbitrary")),
        name="norm_proj_in",
    )(x, g, mod, mod, w_bf16)


GROUPS_PER_LANE_TILE = LANES_V7X // SSM_H


def _expanders():
    toe = np.zeros((CHUNK, CW, CW), np.float32)
    e_s = np.zeros((CHUNK, CW), np.float32)
    e_h = np.zeros((SSM_H, CW), np.float32)
    for s in range(CHUNK):
        for h in range(SSM_H):
            e_s[s, s * SSM_H + h] = 1.0
            e_h[h, s * SSM_H + h] = 1.0
            for t in range(s, CHUNK):
                toe[t, (t - s) * SSM_H + h, s * SSM_H + h] = 1.0
    return jnp.asarray(toe), jnp.asarray(e_s), jnp.asarray(e_h)


def _s5_params(lam_re, lam_im, log_step, b_re, b_im, c_re, c_im, ts, bs):
    hp = lax.Precision.HIGHEST
    toe, e_s, e_h = _expanders()
    lr, li = lam_re.astype(F32), lam_im.astype(F32)
    step = jnp.exp(log_step.astype(F32))[:, None]
    mag = jnp.exp(lr * step)
    a_re = mag * jnp.cos(li * step)
    a_im = mag * jnp.sin(li * step)
    den = lr * lr + li * li
    nr = a_re - 1.0
    f_re = (nr * lr + a_im * li) / den
    f_im = (a_im * lr - nr * li) / den
    br, bi = b_re.astype(F32), b_im.astype(F32)
    bb_re = f_re[..., None] * br - f_im[..., None] * bi
    bb_im = f_re[..., None] * bi + f_im[..., None] * br
    cr, ci = c_re.astype(F32), c_im.astype(F32)
    g = lr.shape[0]

    pw_re, pw_im = [jnp.ones_like(a_re)], [jnp.zeros_like(a_im)]
    for _ in range(CHUNK):
        pr, pi = pw_re[-1], pw_im[-1]
        pw_re.append(pr * a_re - pi * a_im)
        pw_im.append(pr * a_im + pi * a_re)
    pw_re, pw_im = jnp.stack(pw_re), jnp.stack(pw_im)

    bbt_re, bbt_im = bb_re.transpose(0, 2, 1), bb_im.transpose(0, 2, 1)
    w_re = pw_re[:CHUNK, :, None, :] * bbt_re - pw_im[:CHUNK, :, None, :] * bbt_im
    w_im = pw_re[:CHUNK, :, None, :] * bbt_im + pw_im[:CHUNK, :, None, :] * bbt_re
    m = (jnp.einsum('gop,lgip->goli', cr, w_re, precision=hp)
         - jnp.einsum('gop,lgip->goli', ci, w_im, precision=hp)).reshape(g, SSM_H, CW)
    k_t = jnp.einsum('gok,tkn->gton', m, toe, precision=hp).reshape(g, CW, CW)

    rev_re, rev_im = pw_re[:CHUNK][::-1], pw_im[:CHUNK][::-1]
    ar = jnp.einsum('sgp,sn->gpn', rev_re, e_s, precision=hp)
    ai = jnp.einsum('sgp,sn->gpn', rev_im, e_s, precision=hp)
    xr = jnp.einsum('gpi,in->gpn', bb_re, e_h, precision=hp)
    xi = jnp.einsum('gpi,in->gpn', bb_im, e_h, precision=hp)
    bt_re = ar * xr - ai * xi
    bt_im = ar * xi + ai * xr
    b_t = jnp.concatenate([bt_re, bt_im, bt_im, bt_re], axis=1)

    qr = pw_re[1:].transpose(1, 0, 2)[:, :, None, :]
    qi = pw_im[1:].transpose(1, 0, 2)[:, :, None, :]
    cm_re = cr[:, None] * qr - ci[:, None] * qi
    cm_im = -(cr[:, None] * qi + ci[:, None] * qr)
    c_t = jnp.concatenate([cm_re, cm_im], axis=-1).reshape(g, CW, SW)

    def coef(k):
        pr, pi = pw_re[k], pw_im[k]
        return (jnp.concatenate([pr, pr, pr, pr], axis=-1),
                jnp.concatenate([-pi, pi, pi, -pi], axis=-1))
    p16, q16 = coef(CHUNK)
    p8, q8 = coef(ts)
    p8b = jnp.broadcast_to(p8[:, :SW, None], (g, SW, bs))
    q8b = jnp.broadcast_to(q8[:, :SW, None], (g, SW, bs))
    return k_t.astype(BF16), b_t.astype(BF16), c_t.astype(BF16), p16, q16, p8b, q8b


def _s5_prompt_kernel(us_ref, kt_ref, bt_ref, ct_ref, p_ref, q_ref, y_ref, xe_ref, ut, yt, scan_x, scan_xs, s_scr):
    gt = ut.shape[0]
    nchunk = ut.shape[-1]
    for t in range(CHUNK):
        slab = us_ref[0, pl.ds(t, nchunk, stride=CHUNK), :]
        ut[:, t] = slab.T.reshape(gt, SSM_H, nchunk).astype(BF16)
    for gi in range(gt):
        buz = _dot(bt_ref[gi], ut[gi].reshape(CW, nchunk))
        buz_t = buz.T
        scan_x[pl.ds(gi, nchunk, stride=gt), :] = buz_t[:, :SW]
        scan_xs[pl.ds(gi, nchunk, stride=gt), :] = buz_t[:, SW:]
    px, pxs = p_ref[:, :SW], p_ref[:, SW:]
    qx, qxs = q_ref[:, :SW], q_ref[:, SW:]
    x = jnp.zeros((gt, SW), F32)
    xs = jnp.zeros((gt, SW), F32)
    for c in range(nchunk):
        lo, hi = c * gt, (c + 1) * gt
        s_scr[lo:hi, :] = x
        x, xs = (px * x + qx * xs + scan_x[lo:hi, :],
                 pxs * xs + qxs * x + scan_xs[lo:hi, :])
    xe_ref[0] = x
    for gi in range(gt):
        st = s_scr[pl.ds(gi, nchunk, stride=gt), :].T.astype(BF16)
        y_t = _dot(kt_ref[gi], ut[gi].reshape(CW, nchunk)) + _dot(ct_ref[gi], st)
        yt[gi] = y_t.reshape(CHUNK, SSM_H, nchunk)
    for t in range(CHUNK):
        y_ref[0, pl.ds(t, nchunk, stride=CHUNK), :] = yt[:, t].reshape(gt * SSM_H, nchunk).T


def _s5_prompt(u, k_t, b_t, c_t, p16, q16, c_off):
    b, t, _ = u.shape
    g = k_t.shape[0]
    gt = GROUPS_PER_LANE_TILE
    lt = gt * SSM_H
    nchunk = t // CHUNK
    off = c_off // lt

    def wspec(r, c):
        return pl.BlockSpec((gt, r, c), lambda bi, j: (j, 0, 0))
    return pl.pallas_call(
        _s5_prompt_kernel,
        grid=(b, g // gt),
        in_specs=[pl.BlockSpec((1, t, lt), lambda bi, j: (bi, 0, off + j)),
                  wspec(CW, CW), wspec(2 * SW, CW), wspec(CW, SW),
                  pl.BlockSpec((gt, 2 * SW), lambda bi, j: (j, 0)),
                  pl.BlockSpec((gt, 2 * SW), lambda bi, j: (j, 0))],
        out_specs=[pl.BlockSpec((1, t, lt), lambda bi, j: (bi, 0, j)),
                   pl.BlockSpec((1, gt, SW), lambda bi, j: (bi, j, 0))],
        out_shape=[jax.ShapeDtypeStruct((b, t, g * SSM_H), F32), jax.ShapeDtypeStruct((b, g, SW), F32)],
        scratch_shapes=[pltpu.VMEM((gt, CHUNK, SSM_H, nchunk), BF16), pltpu.VMEM((gt, CHUNK, SSM_H, nchunk), F32),
                        pltpu.VMEM((nchunk * gt, SW), F32), pltpu.VMEM((nchunk * gt, SW), F32),
                        pltpu.VMEM((nchunk * gt, SW), F32)],
        compiler_params=_cparams(("arbitrary", "arbitrary")),
        name="s5_prompt",
    )(u, k_t, b_t, c_t, p16, q16)


def _s5_sample_kernel(us_ref, x0_ref, kt_ref, bt_ref, ct_ref, p_ref, q_ref, y_ref, x1_ref, ut, yt, *, ts):
    gt = ut.shape[0]
    bs = ut.shape[-1]
    w = ts * SSM_H
    for t in range(ts):
        slab = us_ref[pl.ds(t, bs, stride=ts), :]
        ut[:, t] = slab.T.reshape(gt, SSM_H, bs).astype(BF16)
    for gi in range(gt):
        ug = ut[gi].reshape(w, bs)
        x0 = x0_ref[gi]
        x0s = jnp.concatenate([x0[SSM_P:], x0[:SSM_P]], axis=0)
        y_t = _dot(kt_ref[gi, :w, :w], ug) + _dot(ct_ref[gi, :w, :], x0.astype(BF16))
        x1_ref[gi] = p_ref[gi] * x0 + q_ref[gi] * x0s + _dot(bt_ref[gi, :SW, CW - w:], ug)
        yt[gi] = y_t.reshape(ts, SSM_H, bs)
    for t in range(ts):
        y_ref[pl.ds(t, bs, stride=ts), :] = yt[:, t].reshape(gt * SSM_H, bs).T


def _s5_sample(u2, x0_t, k_t, b_t, c_t, p8b, q8b, ts, c_off):
    rows, _ = u2.shape
    g, _, bs = x0_t.shape
    gt = GROUPS_PER_LANE_TILE
    lt = gt * SSM_H
    off = c_off // lt

    def wspec(r, c):
        return pl.BlockSpec((gt, r, c), lambda j: (j, 0, 0))
    return pl.pallas_call(
        functools.partial(_s5_sample_kernel, ts=ts),
        grid=(g // gt,),
        in_specs=[pl.BlockSpec((rows, lt), lambda j: (0, off + j)),
                  wspec(SW, bs), wspec(CW, CW), wspec(2 * SW, CW), wspec(CW, SW), wspec(SW, bs), wspec(SW, bs)],
        out_specs=[pl.BlockSpec((rows, lt), lambda j: (0, j)), wspec(SW, bs)],
        out_shape=[jax.ShapeDtypeStruct((rows, g * SSM_H), F32), jax.ShapeDtypeStruct((g, SW, bs), F32)],
        scratch_shapes=[pltpu.VMEM((gt, ts, SSM_H, bs), BF16), pltpu.VMEM((gt, ts, SSM_H, bs), F32)],
        compiler_params=_cparams(("arbitrary",)),
        name="s5_sample",
    )(u2, x0_t, k_t, b_t, c_t, p8b, q8b)


def _mix_kernel(x_ref, up_ref, us_ref, ys_ref, st_ref, g1_ref, wp_ref, ps_ref, dsk_ref, wglu_ref,
                wout_ref, o_ref, np_ref, ext, *, start_pos, carry):
    nb, tt, d = x_ref.shape
    c = up_ref.shape[-1]
    pg = c // len(POOL_WINDOWS)
    rows = nb * tt
    i = pl.program_id(1)

    @pl.when(i == 0)
    def _():
        ext[:, HALO - POOL_BUF:HALO, :] = st_ref[...]
    ext[:, HALO:HALO + tt, :] = up_ref[...]
    np_ref[...] = ext[:, HALO + tt - POOL_BUF:HALO + tt, :]

    pos = start_pos + i * tt + lax.broadcasted_iota(I32, (1, tt, 1), 1)
    outs = []
    for gidx, w in enumerate(POOL_WINDOWS):
        c0 = gidx * pg
        cur = ext[:, HALO:HALO + tt, c0:c0 + pg]
        s = cur
        for k in range(1, w):
            s = s + ext[:, HALO - k:HALO - k + tt, c0:c0 + pg]
        cnt = jnp.minimum(pos + 1, w).astype(F32)
        dd = s / cnt - cur
        yg = _dot(dd.reshape(rows, pg).astype(BF16), wp_ref[gidx])
        outs.append(yg * ps_ref[:, c0:c0 + pg])
    y_pool = jnp.concatenate(outs, axis=-1)

    if carry:
        ext[:, 0:HALO, :] = ext[:, tt:tt + HALO, :]

    us = us_ref[...].reshape(rows, c)
    yf = ys_ref[...].reshape(rows, c) + dsk_ref[...] * us
    gl = jax.nn.gelu(yf, approximate=True)
    o = gl * jax.nn.sigmoid(_dot(gl.astype(BF16), wglu_ref[...]))
    m = jnp.concatenate([y_pool, o], axis=-1).astype(BF16)
    mo = _dot(m, wout_ref[...])
    o_ref[...] = x_ref[...] + g1_ref[...] * mo.reshape(nb, tt, d)


def _mix(x, u, y_ssm, pool_state, mod, wp, ps, dsk, wglu, wout, start_pos):
    b, t, d = x.shape
    c = y_ssm.shape[-1]
    nb, tt = _tile(b, t, ROWS_MIX)
    nt = t // tt
    assert nt == 1 or tt >= HALO
    const2 = lambda bi, i: (0, 0)
    return pl.pallas_call(
        functools.partial(_mix_kernel, start_pos=start_pos, carry=nt > 1),
        grid=(b // nb, nt),
        in_specs=[pl.BlockSpec((nb, tt, d), lambda bi, i: (bi, i, 0)),
                  pl.BlockSpec((nb, tt, c), lambda bi, i: (bi, i, 0)),
                  pl.BlockSpec((nb, tt, c), lambda bi, i: (bi, i, 1)),
                  pl.BlockSpec((nb, tt, c), lambda bi, i: (bi, i, 0)),
                  pl.BlockSpec((nb, POOL_BUF, c), lambda bi, i: (bi, 0, 0)),
                  _mod_spec(nb, d, 2),
                  pl.BlockSpec(wp.shape, lambda bi, i: (0, 0, 0)),
                  pl.BlockSpec((1, c), const2), pl.BlockSpec((1, c), const2),
                  pl.BlockSpec((c, c), const2), pl.BlockSpec((d, d), const2)],
        out_specs=[pl.BlockSpec((nb, tt, d), lambda bi, i: (bi, i, 0)),
                   pl.BlockSpec((nb, POOL_BUF, c), lambda bi, i: (bi, 0, 0))],
        out_shape=[jax.ShapeDtypeStruct((b, t, d), F32), jax.ShapeDtypeStruct((b, POOL_BUF, c), F32)],
        scratch_shapes=[pltpu.VMEM((nb, HALO + tt, c), F32)],
        compiler_params=_cparams(("arbitrary", "arbitrary")),
        name="pool_glu_proj_out",
    )(x, u, u, y_ssm, pool_state, mod, wp, ps, dsk, wglu, wout)


def _ffn_kernel(x_ref, g_ref, s_ref, sh_ref, g2_ref, wg_ref, wu_ref, wd_ref, o_ref, hb, acc, *, last_valid):
    nb, tt, d = x_ref.shape
    tf = wg_ref.shape[1]
    j = pl.program_id(2)
    nj = pl.num_programs(2)

    @pl.when(j == 0)
    def _():
        h = _modnorm(x_ref[...], g_ref[...], s_ref[...], sh_ref[...])
        hb[...] = h.reshape(nb * tt, d).astype(BF16)
        acc[...] = jnp.zeros_like(acc)

    def step(valid):
        h = hb[...]
        gate = _dot(h, wg_ref[...])
        up = _dot(h, wu_ref[...])
        a = gate * jax.nn.sigmoid(gate) * up
        wd = wd_ref[...]
        if valid < tf:
            a = jnp.where(lax.broadcasted_iota(I32, (1, tf), 1) < valid, a, 0.0)
            wd = jnp.where(lax.broadcasted_iota(I32, (tf, 1), 0) < valid, wd, jnp.zeros_like(wd))
        acc[...] += _dot(a.astype(BF16), wd)

    if last_valid == tf:
        step(tf)
    else:
        pl.when(j < nj - 1)(lambda: step(tf))
        pl.when(j == nj - 1)(lambda: step(last_valid))

    @pl.when(j == nj - 1)
    def _():
        o_ref[...] = x_ref[...] + g2_ref[...] * acc[...].reshape(nb, tt, d)


def _ffn(x, mod, g, wg, wu, wd):
    b, t, d = x.shape
    ff = wg.shape[1]
    nb, tt = _tile(b, t, ROWS_FFN)
    tf = min(FF_TILE, ff)
    nj = -(-ff // tf)
    return pl.pallas_call(
        functools.partial(_ffn_kernel, last_valid=ff - (nj - 1) * tf),
        grid=(b // nb, t // tt, nj),
        in_specs=[pl.BlockSpec((nb, tt, d), lambda bi, i, j: (bi, i, 0)),
                  pl.BlockSpec((1, d), lambda bi, i, j: (0, 0)),
                  pl.BlockSpec((nb, 1, d), lambda bi, i, j: (bi, 0, 4)),
                  pl.BlockSpec((nb, 1, d), lambda bi, i, j: (bi, 0, 3)),
                  pl.BlockSpec((nb, 1, d), lambda bi, i, j: (bi, 0, 5)),
                  pl.BlockSpec((d, tf), lambda bi, i, j: (0, j)),
                  pl.BlockSpec((d, tf), lambda bi, i, j: (0, j)),
                  pl.BlockSpec((tf, d), lambda bi, i, j: (j, 0))],
        out_specs=pl.BlockSpec((nb, tt, d), lambda bi, i, j: (bi, i, 0)),
        out_shape=jax.ShapeDtypeStruct((b, t, d), F32),
        scratch_shapes=[pltpu.VMEM((nb * tt, d), BF16), pltpu.VMEM((nb * tt, d), F32)],
        compiler_params=_cparams(("arbitrary", "arbitrary", "arbitrary")),
        name="dense_swiglu",
    )(x, g, mod, mod, mod, wg, wu, wd)


ROUTE_LANES = LANES_V7X
ROUTE_ROWS = SUBLANES_V7X


def _route_kernel(x_ref, g_ref, s_ref, sh_ref, wr_ref, br_ref, cin_ref, h_ref, r_ref, rt_ref, cout_ref, carry):
    nb, tt, d = x_ref.shape
    tm = nb * tt
    first = jnp.logical_and(pl.program_id(0) == 0, pl.program_id(1) == 0)

    @pl.when(first)
    def _():
        carry[...] = cin_ref[...]

    h = _modnorm(x_ref[...], g_ref[...], s_ref[...], sh_ref[...]).reshape(tm, d)
    h_ref[...] = h
    logits = jnp.dot(h, wr_ref[...], preferred_element_type=F32, precision=lax.Precision.HIGHEST) + br_ref[...]
    lane = lax.broadcasted_iota(I32, (tm, ROUTE_LANES), 1)
    neg = jnp.float32(-jnp.inf)
    l1 = jnp.where(lane < N_EXPERTS, logits, neg)
    m1 = jnp.max(l1, axis=-1, keepdims=True)
    i1 = jnp.min(jnp.where(l1 == m1, lane, ROUTE_LANES), axis=-1, keepdims=True)
    l2 = jnp.where(lane == i1, neg, l1)
    m2 = jnp.max(l2, axis=-1, keepdims=True)
    i2 = jnp.min(jnp.where(l2 == m2, lane, ROUTE_LANES), axis=-1, keepdims=True)
    e2 = jnp.exp(m2 - m1)
    den = 1.0 + e2
    p1 = 1.0 / den
    p2 = e2 / den

    onehot = jnp.logical_or(lane == i1, lane == i2).astype(F32)
    rr = lax.broadcasted_iota(I32, (tm, tm), 0)
    cc = lax.broadcasted_iota(I32, (tm, tm), 1)
    before = (cc < rr).astype(BF16)
    rank = _dot(before, onehot.astype(BF16)) + carry[...]
    r1 = jnp.sum(jnp.where(lane == i1, rank, 0.0), axis=-1, keepdims=True)
    r2 = jnp.sum(jnp.where(lane == i2, rank, 0.0), axis=-1, keepdims=True)
    carry[...] += jnp.sum(onehot, axis=0, keepdims=True)

    out = jnp.where(lane == 0, i1.astype(F32), 0.0)
    out = jnp.where(lane == 1, i2.astype(F32), out)
    out = jnp.where(lane == 2, p1, out)
    out = jnp.where(lane == 3, p2, out)
    out = jnp.where(lane == 4, r1, out)
    out = jnp.where(lane == 5, r2, out)
    r_ref[...] = out
    rt_ref[...] = out.T[:ROUTE_ROWS, :]
    cout_ref[...] = carry[...]


def _route(x, mod, g, wr_pad, br_pad, counts_in):
    b, t, d = x.shape
    nb, tt = _tile(b, t, ROWS_ROUTE)
    tm = nb * tt
    nt = t // tt
    const2 = lambda bi, i: (0, 0)
    return pl.pallas_call(
        _route_kernel,
        grid=(b // nb, nt),
        in_specs=[pl.BlockSpec((nb, tt, d), lambda bi, i: (bi, i, 0)),
                  pl.BlockSpec((1, d), const2),
                  _mod_spec(nb, d, 4), _mod_spec(nb, d, 3),
                  pl.BlockSpec((d, ROUTE_LANES), const2),
                  pl.BlockSpec((1, ROUTE_LANES), const2),
                  pl.BlockSpec((1, ROUTE_LANES), const2)],
        out_specs=[pl.BlockSpec((tm, d), lambda bi, i: (bi * nt + i, 0)),
                   pl.BlockSpec((tm, ROUTE_LANES), lambda bi, i: (bi * nt + i, 0)),
                   pl.BlockSpec((ROUTE_ROWS, tm), lambda bi, i: (0, bi * nt + i)),
                   pl.BlockSpec((1, ROUTE_LANES), const2)],
        out_shape=[jax.ShapeDtypeStruct((b * t, d), F32),
                   jax.ShapeDtypeStruct((b * t, ROUTE_LANES), F32),
                   jax.ShapeDtypeStruct((ROUTE_ROWS, b * t), F32),
                   jax.ShapeDtypeStruct((1, ROUTE_LANES), F32)],
        scratch_shapes=[pltpu.VMEM((1, ROUTE_LANES), F32)],
        compiler_params=_cparams(("arbitrary", "arbitrary")),
        name="moe_route",
    )(x, g, mod, mod, wr_pad, br_pad, counts_in)


def _row_gather_copy(src_hbm, tok, dst, r, sem):
    return pltpu.make_async_copy(src_hbm.at[pl.ds(tok, 1), :], dst.at[pl.ds(r, 1), :], sem)


def _moe_kernel(te_ref, tr_ref, src_ref, h_hbm, wg_ref, wu_ref, wd_ref, o_hbm,
                xb, acc, stage, wgb, wub, wdb, gsem, osem):
    s = pl.program_id(0)
    j = pl.program_id(1)
    ns = pl.num_programs(0)
    nj = pl.num_programs(1)
    rows = tr_ref[s]
    nsub = (rows + MOE_SUB - 1) // MOE_SUB
    tile_rows = xb.shape[0]

    def out_copy():
        return pltpu.make_async_copy(acc, o_hbm.at[pl.ds(pl.multiple_of(s * tile_rows, MOE_SUB), tile_rows), :], osem)

    @pl.when(j == 0)
    def _():
        @pl.when(rows > 0)
        def _gather():
            def issue(sub, slot):
                base = s * tile_rows + sub * MOE_SUB

                def body(r, carry):
                    _row_gather_copy(h_hbm, src_ref[base + r], stage.at[slot], r, gsem.at[slot]).start()
                    return carry
                lax.fori_loop(0, MOE_SUB, body, 0, unroll=8)

            issue(0, 0)

            def sub_body(sub, carry):
                slot = sub % 2

                @pl.when(sub + 1 < nsub)
                def _():
                    issue(sub + 1, 1 - slot)
                pltpu.make_async_copy(h_hbm.at[pl.ds(0, MOE_SUB), :], stage.at[slot], gsem.at[slot]).wait()
                r0 = pl.multiple_of(sub * MOE_SUB, MOE_SUB)
                xb[pl.ds(r0, MOE_SUB), :] = stage[slot].astype(BF16)
                return carry
            lax.fori_loop(0, nsub, sub_body, 0)

        @pl.when(s > 0)
        def _():
            out_copy().wait()
        acc[...] = jnp.zeros_like(acc)

    @pl.when(rows > 0)
    def _():
        wgb[...] = wg_ref[...].astype(BF16)
        wub[...] = wu_ref[...].astype(BF16)
        wdb[...] = wd_ref[...].astype(BF16)

        def block(r0, m):
            xs = xb[pl.ds(r0, m), :]
            gate = _dot(xs, wgb[...])
            up = _dot(xs, wub[...])
            a = (gate * jax.nn.sigmoid(gate) * up).astype(BF16)
            half = acc.shape[1] // 2
            for c0 in (0, half):
                acc[pl.ds(r0, m), c0:c0 + half] += _dot(a, wdb[:, c0:c0 + half])

        per_blk = MOE_BLK // MOE_SUB
        nblk = nsub // per_blk
        rem = nsub - nblk * per_blk

        def body(blk, carry):
            block(pl.multiple_of(blk * MOE_BLK, MOE_BLK), MOE_BLK)
            return carry
        lax.fori_loop(0, nblk, body, 0)
        for k in range(1, per_blk):
            @pl.when(rem == k)
            def _(k=k):
                block(pl.multiple_of(nblk * MOE_BLK, MOE_BLK), k * MOE_SUB)

    @pl.when(j == nj - 1)
    def _():
        out_copy().start()

        @pl.when(s == ns - 1)
        def _():
            out_copy().wait()


def _moe_experts(h_all, te, tr, src, wg, wu, wd, n_tiles):
    n, d = h_all.shape
    e, _, ff = wg.shape
    tf = min(MOE_FF_TILE, ff)
    assert ff % tf == 0 and MOE_TILE_ROWS % MOE_BLK == 0 and MOE_BLK % MOE_SUB == 0
    nj = ff // tf
    r = MOE_TILE_ROWS

    def jj(s, j, tr_ref):
        return jnp.where(tr_ref[s] > 0, j, nj - 1)

    grid_spec = pltpu.PrefetchScalarGridSpec(
        num_scalar_prefetch=3,
        grid=(n_tiles, nj),
        in_specs=[pl.BlockSpec(memory_space=pl.ANY),
                  pl.BlockSpec((None, d, tf), lambda s, j, te_r, tr_r, src_r: (te_r[s], 0, jj(s, j, tr_r))),
                  pl.BlockSpec((None, d, tf), lambda s, j, te_r, tr_r, src_r: (te_r[s], 0, jj(s, j, tr_r))),
                  pl.BlockSpec((None, tf, d), lambda s, j, te_r, tr_r, src_r: (te_r[s], jj(s, j, tr_r), 0))],
        out_specs=pl.BlockSpec(memory_space=pl.ANY),
        scratch_shapes=[pltpu.VMEM((r, d), BF16), pltpu.VMEM((r, d), F32), pltpu.VMEM((2, MOE_SUB, d), F32),
                        pltpu.VMEM((d, tf), BF16), pltpu.VMEM((d, tf), BF16), pltpu.VMEM((tf, d), BF16),
                        pltpu.SemaphoreType.DMA((2,)), pltpu.SemaphoreType.DMA(())],
    )
    return pl.pallas_call(
        _moe_kernel,
        grid_spec=grid_spec,
        out_shape=jax.ShapeDtypeStruct((n_tiles * r, d), F32),
        compiler_params=_cparams(("arbitrary", "arbitrary")),
        name="moe_experts",
    )(te, tr, src, h_all, wg, wu, wd)


def _combine_kernel(s1_ref, s2_ref, x_ref, g2_ref, r_ref, gf_ref, e_hbm, o_ref, st1, st2, sem):
    nb, tt, d = x_ref.shape
    tm = nb * tt
    base = (pl.program_id(0) * pl.num_programs(1) + pl.program_id(1)) * tm

    def body(r, carry):
        _row_gather_copy(e_hbm, s1_ref[base + r], st1, r, sem.at[0]).start()
        _row_gather_copy(e_hbm, s2_ref[base + r], st2, r, sem.at[1]).start()
        return carry
    lax.fori_loop(0, tm, body, 0, unroll=8)
    pltpu.make_async_copy(e_hbm.at[pl.ds(0, tm), :], st1, sem.at[0]).wait()
    pltpu.make_async_copy(e_hbm.at[pl.ds(0, tm), :], st2, sem.at[1]).wait()

    f = r_ref[:, 2:3] * st1[...] + r_ref[:, 3:4] * st2[...]
    x2 = x_ref[...] + g2_ref[...] * f.reshape(nb, tt, d)
    rs = lax.rsqrt(jnp.mean(x2 * x2, axis=-1, keepdims=True) + EPS)
    o_ref[...] = (x2 * rs) * gf_ref[...]


def _combine(x, mod, route, slot1, slot2, e_out, gf):
    b, t, d = x.shape
    nb, tt = _tile(b, t, ROWS_COMBINE)
    tm = nb * tt
    nt = t // tt
    grid_spec = pltpu.PrefetchScalarGridSpec(
        num_scalar_prefetch=2,
        grid=(b // nb, nt),
        in_specs=[pl.BlockSpec((nb, tt, d), lambda bi, i, a, c: (bi, i, 0)),
                  pl.BlockSpec((nb, 1, d), lambda bi, i, a, c: (bi, 0, 5)),
                  pl.BlockSpec((tm, ROUTE_LANES), lambda bi, i, a, c: (bi * nt + i, 0)),
                  pl.BlockSpec((1, d), lambda bi, i, a, c: (0, 0)),
                  pl.BlockSpec(memory_space=pl.ANY)],
        out_specs=pl.BlockSpec((nb, tt, d), lambda bi, i, a, c: (bi, i, 0)),
        scratch_shapes=[pltpu.VMEM((tm, d), F32), pltpu.VMEM((tm, d), F32), pltpu.SemaphoreType.DMA((2,))],
    )
    return pl.pallas_call(
        _combine_kernel,
        grid_spec=grid_spec,
        out_shape=jax.ShapeDtypeStruct((b, t, d), F32),
        compiler_params=_cparams(("arbitrary", "arbitrary")),
        name="moe_combine_norm",
    )(slot1, slot2, x, mod, route, gf, e_out)


def _lookup(table, idx):
    out = jnp.zeros_like(idx)
    for e in range(N_EXPERTS):
        out = jnp.where(idx == e, table[e], out)
    return out


def _floor_div(a, b):
    q = jnp.floor(a.astype(F32) / b.astype(F32)).astype(I32)
    q = jnp.where((q + 1) * b <= a, q + 1, q)
    return jnp.where(q * b > a, q - 1, q)


def _moe_layout(route_t, counts_f):
    n = route_t.shape[1]
    r, sub = MOE_TILE_ROWS, MOE_SUB
    n_tiles = -(-TOP_K * n // r) + N_EXPERTS
    i1 = route_t[0].astype(I32)
    i2 = route_t[1].astype(I32)
    r1 = route_t[4].astype(I32)
    r2 = route_t[5].astype(I32)
    counts = counts_f[0, :N_EXPERTS].astype(I32)
    nt = (counts + r - 1) // r
    ntc = jnp.maximum(nt, 1)
    sz = jnp.maximum((((counts + ntc - 1) // ntc + sub - 1) // sub) * sub, sub)
    cum = jnp.cumsum(nt)
    tstart = cum - nt
    total = cum[-1]
    s_idx = jnp.arange(n_tiles, dtype=I32)
    te_raw = jnp.minimum(jnp.sum((s_idx[:, None] >= cum[None, :]).astype(I32), axis=1), N_EXPERTS - 1)
    used = s_idx < total
    last_e = jnp.max(jnp.where(used, te_raw, 0))
    te = jnp.where(used, te_raw, last_e)
    sz_t = _lookup(sz, te)
    k_in = s_idx - _lookup(tstart, te)
    tr = jnp.where(used, jnp.clip(_lookup(counts, te) - k_in * sz_t, 0, sz_t), 0).astype(I32)

    def slot(ei, ri):
        sz_i = _lookup(sz, ei)
        k = _floor_div(ri, sz_i)
        return (_lookup(tstart, ei) + k) * r + (ri - k * sz_i)
    slot1, slot2 = slot(i1, r1), slot(i2, r2)
    tok = jnp.arange(n, dtype=I32)
    src = jnp.zeros((n_tiles * r,), I32).at[jnp.concatenate([slot1, slot2])].set(jnp.concatenate([tok, tok]))
    return te, tr, src, slot1, slot2, n_tiles


def kernel(x_prompt, x_sample, c_prompt, c_sample, state_pool, state_ssm_re, state_ssm_im, w_ada, b_ada, norm_mix, norm_ffn, w_in, w_pool, pool_scale, ssm_lam_re, ssm_lam_im, ssm_log_step, ssm_b_re, ssm_b_im, ssm_c_re, ssm_c_im, ssm_d, ssm_w_glu, w_out, ffn_w_gate, ffn_w_up, ffn_w_down, moe_w_router, moe_b_router, moe_w_gate, moe_w_up, moe_w_down, norm_final):
    bp, tp, d = x_prompt.shape
    bs, ts, _ = x_sample.shape
    depth = w_ada.shape[0]
    c_pool = w_pool.shape[1] * w_pool.shape[2]
    assert tp % CHUNK == 0 and ts <= CHUNK

    m_rows = -(-(bp + bs) // SUBLANES_V7X) * SUBLANES_V7X
    c_all = jnp.concatenate([c_prompt, c_sample, jnp.zeros((m_rows - bp - bs, d), F32)], axis=0)
    mod_all = _ada(c_all, w_ada, b_ada)

    xp, xs = x_prompt, x_sample
    pool_p, re_p, im_p, pool_s, re_s, im_s = [], [], [], [], [], []
    y_p = y_s = None
    for l in range(depth):
        mod_p = mod_all[l, :bp].reshape(bp, 1, 6 * d)
        mod_s = mod_all[l, bp:bp + bs].reshape(bs, 1, 6 * d)
        gm = norm_mix[l].reshape(1, d)
        gn = norm_ffn[l].reshape(1, d)

        w_in_b = w_in[l].astype(BF16)
        up_all = _proj_in(xp, mod_p, gm, w_in_b)
        us_all = _proj_in(xs, mod_s, gm, w_in_b)
        k_t, b_t, c_t, p16, q16, p8b, q8b = _s5_params(ssm_lam_re[l], ssm_lam_im[l], ssm_log_step[l], ssm_b_re[l],
                                                       ssm_b_im[l], ssm_c_re[l], ssm_c_im[l], ts, bs)
        x0_t = jnp.concatenate([state_ssm_re[l], state_ssm_im[l]], axis=-1).astype(F32).transpose(1, 2, 0)
        y_ssm_p, xend = _s5_prompt(up_all, k_t, b_t, c_t, p16, q16, c_pool)
        ys2, x1_t = _s5_sample(us_all.reshape(bs * ts, d), x0_t, k_t, b_t, c_t, p8b, q8b, ts, c_pool)
        y_ssm_s = ys2.reshape(bs, ts, d - c_pool)

        wp_b = w_pool[l].astype(BF16)
        ps = pool_scale[l].reshape(1, c_pool).astype(F32)
        dsk = ssm_d[l].reshape(1, -1).astype(F32)
        wglu_b = ssm_w_glu[l].astype(BF16)
        wout_b = w_out[l].astype(BF16)
        st_p = jnp.zeros((bp, POOL_BUF, c_pool), F32)
        xp, new_pool_p = _mix(xp, up_all, y_ssm_p, st_p, mod_p, wp_b, ps, dsk, wglu_b, wout_b, 0)
        xs, new_pool_s = _mix(xs, us_all, y_ssm_s, state_pool[l].astype(F32), mod_s, wp_b, ps, dsk, wglu_b, wout_b,
                              PAST_LEN)
        pool_p.append(new_pool_p)
        pool_s.append(new_pool_s)
        re_p.append(xend[..., :SSM_P])
        im_p.append(xend[..., SSM_P:])
        re_s.append(x1_t[:, :SSM_P, :].transpose(2, 0, 1))
        im_s.append(x1_t[:, SSM_P:, :].transpose(2, 0, 1))

        i = l // 2
        if l % 2 == 0:
            wg = ffn_w_gate[i].astype(BF16)
            wu = ffn_w_up[i].astype(BF16)
            wd = ffn_w_down[i].astype(BF16)
            xp = _ffn(xp, mod_p, gn, wg, wu, wd)
            xs = _ffn(xs, mod_s, gn, wg, wu, wd)
        else:
            ne = moe_w_router.shape[-1]
            wr = jnp.pad(moe_w_router[i].astype(F32), ((0, 0), (0, ROUTE_LANES - ne)))
            br = jnp.pad(moe_b_router[i].astype(F32), (0, ROUTE_LANES - ne)).reshape(1, ROUTE_LANES)
            zero_counts = jnp.zeros((1, ROUTE_LANES), F32)
            h_p, route_p, rt_p, cnt_p = _route(xp, mod_p, gn, wr, br, zero_counts)
            h_s, route_s, rt_s, cnt_s = _route(xs, mod_s, gn, wr, br, cnt_p)
            h_all = jnp.concatenate([h_p, h_s], axis=0)
            te, tr, src, slot1, slot2, n_tiles = _moe_layout(jnp.concatenate([rt_p, rt_s], axis=1), cnt_s)
            e_out = _moe_experts(h_all, te, tr, src, moe_w_gate[i], moe_w_up[i], moe_w_down[i], n_tiles)
            n_p = bp * tp
            last = l == depth - 1
            assert last, "the combine kernel also applies the final norm"
            gf = norm_final.reshape(1, d)
            y_p = _combine(xp, mod_p, route_p, slot1[:n_p], slot2[:n_p], e_out, gf)
            y_s = _combine(xs, mod_s, route_s, slot1[n_p:], slot2[n_p:], e_out, gf)

    return (y_p, y_s, jnp.stack(pool_p), jnp.stack(re_p), jnp.stack(im_p),
            jnp.stack(pool_s), jnp.stack(re_s), jnp.stack(im_s))
```

```python
import functools
import math

import jax
import jax.numpy as jnp
import numpy as np
from jax import lax
from jax.experimental import pallas as pl
from jax.experimental.pallas import tpu as pltpu

F32 = jnp.float32
BF16 = jnp.bfloat16
I32 = jnp.int32

EPS = 1e-6
POOL_WINDOWS = (2, 4, 8, 16)
POOL_BUF = max(POOL_WINDOWS) - 1
SUBLANES_V7X = 8
LANES_V7X = 128
HALO = -(-POOL_BUF // SUBLANES_V7X) * SUBLANES_V7X
SSM_H = 16
SSM_P = 64
CHUNK = 16
CW = CHUNK * SSM_H
SW = 2 * SSM_P
N_EXPERTS = 8
TOP_K = 2
PAST_LEN = 16384

VMEM_LIMIT_V7X = 56 * 1024 * 1024
ROWS_IN = 512
ROWS_MIX = 256
ROWS_FFN = 512
FF_TILE = 512
ROWS_ROUTE = 512
MOE_TILE_ROWS = 2560
MOE_SUB = 128
MOE_BLK_BIG = 1024
MOE_BLK = 512
GATHER_UNROLL = 8
MOE_FF_TILE = 256
ROWS_COMBINE = 256
ADA_TN = 1024


def _cparams(sem):
    return pltpu.CompilerParams(dimension_semantics=sem, vmem_limit_bytes=VMEM_LIMIT_V7X)


def _tile(b, t, rows):
    if t >= rows:
        assert t % rows == 0
        return 1, rows
    nb = max(1, min(b, rows // t))
    assert b % nb == 0
    return nb, t


def _modnorm(x, g, s, sh):
    r = lax.rsqrt(jnp.mean(x * x, axis=-1, keepdims=True) + EPS)
    return (x * r) * g * (1.0 + s) + sh


def _dot(a, b):
    return jnp.dot(a, b, preferred_element_type=F32)


def _mod_spec(nb, d, k, l):
    return pl.BlockSpec((None, nb, 1, d), lambda b, i, *_: (l, b, 0, k))


def _layer_spec(shape, l):
    return pl.BlockSpec((None,) + tuple(shape), lambda *_: (l,) + (0,) * len(shape))


def _ada_kernel(c_ref, w_ref, b_ref, o_ref):
    c = c_ref[...]
    sc = c * jax.nn.sigmoid(c)
    o_ref[0] = _dot(sc.astype(BF16), w_ref[0].astype(BF16)) + b_ref[0]


def _ada(c_all, w_ada, b_ada):
    m, d = c_all.shape
    depth, _, n6 = w_ada.shape
    tn = math.gcd(ADA_TN, n6)
    return pl.pallas_call(
        _ada_kernel,
        grid=(depth, n6 // tn),
        in_specs=[pl.BlockSpec((m, d), lambda l, n: (0, 0)),
                  pl.BlockSpec((1, d, tn), lambda l, n: (l, 0, n)),
                  pl.BlockSpec((1, 1, tn), lambda l, n: (l, 0, n))],
        out_specs=pl.BlockSpec((1, m, tn), lambda l, n: (l, 0, n)),
        out_shape=jax.ShapeDtypeStruct((depth, m, n6), F32),
        compiler_params=_cparams(("arbitrary", "arbitrary")),
        name="adaln_mod",
    )(c_all, w_ada, b_ada.reshape(depth, 1, n6))


def _in_kernel(x_ref, g_ref, s_ref, sh_ref, w_ref, u_ref):
    nb, tt, d = x_ref.shape
    h = _modnorm(x_ref[...], g_ref[...], s_ref[...], sh_ref[...])
    u = _dot(h.reshape(nb * tt, d).astype(BF16), w_ref[...])
    u_ref[...] = u.reshape(nb, tt, u_ref.shape[-1])


def _proj_in(x, mod, l, g, w_bf16):
    b, t, d = x.shape
    n = w_bf16.shape[-1]
    nb, tt = _tile(b, t, ROWS_IN)
    return pl.pallas_call(
        _in_kernel,
        grid=(b // nb, t // tt),
        in_specs=[pl.BlockSpec((nb, tt, d), lambda bi, i: (bi, i, 0)),
                  pl.BlockSpec((1, d), lambda bi, i: (0, 0)),
                  _mod_spec(nb, d, 1, l), _mod_spec(nb, d, 0, l),
                  _layer_spec((d, n), l)],
        out_specs=pl.BlockSpec((nb, tt, n), lambda bi, i: (bi, i, 0)),
        out_shape=jax.ShapeDtypeStruct((b, t, n), F32),
        compiler_params=_cparams(("arbitrary", "arbitrary")),
        name="norm_proj_in",
    )(x, g, mod, mod, w_bf16)


GROUPS_PER_LANE_TILE = LANES_V7X // SSM_H


def _expanders():
    toe = np.zeros((CHUNK, CW, CW), np.float32)
    e_s = np.zeros((CHUNK, CW), np.float32)
    e_h = np.zeros((SSM_H, CW), np.float32)
    for s in range(CHUNK):
        for h in range(SSM_H):
            e_s[s, s * SSM_H + h] = 1.0
            e_h[h, s * SSM_H + h] = 1.0
            for t in range(s, CHUNK):
                toe[t, (t - s) * SSM_H + h, s * SSM_H + h] = 1.0
    return jnp.asarray(toe), jnp.asarray(e_s), jnp.asarray(e_h)


def _s5_params(lam_re, lam_im, log_step, b_re, b_im, c_re, c_im, ts, bs):
    hp = lax.Precision.HIGHEST
    toe, e_s, e_h = _expanders()
    lr, li = lam_re.astype(F32), lam_im.astype(F32)
    step = jnp.exp(log_step.astype(F32))[:, None]
    mag = jnp.exp(lr * step)
    a_re = mag * jnp.cos(li * step)
    a_im = mag * jnp.sin(li * step)
    den = lr * lr + li * li
    nr = a_re - 1.0
    f_re = (nr * lr + a_im * li) / den
    f_im = (a_im * lr - nr * li) / den
    br, bi = b_re.astype(F32), b_im.astype(F32)
    bb_re = f_re[..., None] * br - f_im[..., None] * bi
    bb_im = f_re[..., None] * bi + f_im[..., None] * br
    cr, ci = c_re.astype(F32), c_im.astype(F32)
    g = lr.shape[0]

    pw_re, pw_im = [jnp.ones_like(a_re)], [jnp.zeros_like(a_im)]
    for _ in range(CHUNK):
        pr, pi = pw_re[-1], pw_im[-1]
        pw_re.append(pr * a_re - pi * a_im)
        pw_im.append(pr * a_im + pi * a_re)
    pw_re, pw_im = jnp.stack(pw_re), jnp.stack(pw_im)

    bbt_re, bbt_im = bb_re.transpose(0, 2, 1), bb_im.transpose(0, 2, 1)
    w_re = pw_re[:CHUNK, :, None, :] * bbt_re - pw_im[:CHUNK, :, None, :] * bbt_im
    w_im = pw_re[:CHUNK, :, None, :] * bbt_im + pw_im[:CHUNK, :, None, :] * bbt_re
    m = (jnp.einsum('gop,lgip->goli', cr, w_re, precision=hp)
         - jnp.einsum('gop,lgip->goli', ci, w_im, precision=hp)).reshape(g, SSM_H, CW)
    k_t = jnp.einsum('gok,tkn->gton', m, toe, precision=hp).reshape(g, CW, CW)

    rev_re, rev_im = pw_re[:CHUNK][::-1], pw_im[:CHUNK][::-1]
    ar = jnp.einsum('sgp,sn->gpn', rev_re, e_s, precision=hp)
    ai = jnp.einsum('sgp,sn->gpn', rev_im, e_s, precision=hp)
    xr = jnp.einsum('gpi,in->gpn', bb_re, e_h, precision=hp)
    xi = jnp.einsum('gpi,in->gpn', bb_im, e_h, precision=hp)
    bt_re = ar * xr - ai * xi
    bt_im = ar * xi + ai * xr
    b_t = jnp.concatenate([bt_re, bt_im, bt_im, bt_re], axis=1)

    qr = pw_re[1:].transpose(1, 0, 2)[:, :, None, :]
    qi = pw_im[1:].transpose(1, 0, 2)[:, :, None, :]
    cm_re = cr[:, None] * qr - ci[:, None] * qi
    cm_im = -(cr[:, None] * qi + ci[:, None] * qr)
    c_t = jnp.concatenate([cm_re, cm_im], axis=-1).reshape(g, CW, SW)

    def coef(k):
        pr, pi = pw_re[k], pw_im[k]
        return (jnp.concatenate([pr, pr, pr, pr], axis=-1),
                jnp.concatenate([-pi, pi, pi, -pi], axis=-1))
    p16, q16 = coef(CHUNK)
    p8, q8 = coef(ts)
    p8b = jnp.broadcast_to(p8[:, :SW, None], (g, SW, bs))
    q8b = jnp.broadcast_to(q8[:, :SW, None], (g, SW, bs))
    return k_t.astype(BF16), b_t.astype(BF16), c_t.astype(BF16), p16, q16, p8b, q8b


def _s5_prompt_kernel(us_ref, kt_ref, bt_ref, ct_ref, p_ref, q_ref, y_ref, xe_ref, ut, yt, scan_x, scan_xs, s_scr):
    gt = ut.shape[0]
    nchunk = ut.shape[-1]
    for t in range(CHUNK):
        slab = us_ref[0, pl.ds(t, nchunk, stride=CHUNK), :]
        ut[:, t] = slab.T.reshape(gt, SSM_H, nchunk).astype(BF16)
    for gi in range(gt):
        buz = _dot(bt_ref[gi], ut[gi].reshape(CW, nchunk))
        buz_t = buz.T
        scan_x[pl.ds(gi, nchunk, stride=gt), :] = buz_t[:, :SW]
        scan_xs[pl.ds(gi, nchunk, stride=gt), :] = buz_t[:, SW:]
    px, pxs = p_ref[:, :SW], p_ref[:, SW:]
    qx, qxs = q_ref[:, :SW], q_ref[:, SW:]
    x = jnp.zeros((gt, SW), F32)
    xs = jnp.zeros((gt, SW), F32)
    for c in range(nchunk):
        lo, hi = c * gt, (c + 1) * gt
        s_scr[lo:hi, :] = x
        x, xs = (px * x + qx * xs + scan_x[lo:hi, :],
                 pxs * xs + qxs * x + scan_xs[lo:hi, :])
    xe_ref[0] = x
    for gi in range(gt):
        st = s_scr[pl.ds(gi, nchunk, stride=gt), :].T.astype(BF16)
        y_t = _dot(kt_ref[gi], ut[gi].reshape(CW, nchunk)) + _dot(ct_ref[gi], st)
        yt[gi] = y_t.reshape(CHUNK, SSM_H, nchunk)
    for t in range(CHUNK):
        y_ref[0, pl.ds(t, nchunk, stride=CHUNK), :] = yt[:, t].reshape(gt * SSM_H, nchunk).T


def _s5_prompt(u, l, k_t, b_t, c_t, p16, q16, c_off):
    b, t, _ = u.shape
    g = k_t.shape[1]
    gt = GROUPS_PER_LANE_TILE
    lt = gt * SSM_H
    nchunk = t // CHUNK
    off = c_off // lt

    def wspec(r, c):
        return pl.BlockSpec((None, gt, r, c), lambda bi, j: (l, j, 0, 0))
    return pl.pallas_call(
        _s5_prompt_kernel,
        grid=(b, g // gt),
        in_specs=[pl.BlockSpec((1, t, lt), lambda bi, j: (bi, 0, off + j)),
                  wspec(CW, CW), wspec(2 * SW, CW), wspec(CW, SW),
                  pl.BlockSpec((None, gt, 2 * SW), lambda bi, j: (l, j, 0)),
                  pl.BlockSpec((None, gt, 2 * SW), lambda bi, j: (l, j, 0))],
        out_specs=[pl.BlockSpec((1, t, lt), lambda bi, j: (bi, 0, j)),
                   pl.BlockSpec((1, gt, SW), lambda bi, j: (bi, j, 0))],
        out_shape=[jax.ShapeDtypeStruct((b, t, g * SSM_H), F32), jax.ShapeDtypeStruct((b, g, SW), F32)],
        scratch_shapes=[pltpu.VMEM((gt, CHUNK, SSM_H, nchunk), BF16), pltpu.VMEM((gt, CHUNK, SSM_H, nchunk), F32),
                        pltpu.VMEM((nchunk * gt, SW), F32), pltpu.VMEM((nchunk * gt, SW), F32),
                        pltpu.VMEM((nchunk * gt, SW), F32)],
        compiler_params=_cparams(("arbitrary", "arbitrary")),
        name="s5_prompt",
    )(u, k_t, b_t, c_t, p16, q16)


def _s5_sample_kernel(us_ref, x0_ref, kt_ref, bt_ref, ct_ref, p_ref, q_ref, y_ref, x1_ref, ut, yt, *, ts):
    gt = ut.shape[0]
    bs = ut.shape[-1]
    w = ts * SSM_H
    for t in range(ts):
        slab = us_ref[pl.ds(t, bs, stride=ts), :]
        ut[:, t] = slab.T.reshape(gt, SSM_H, bs).astype(BF16)
    for gi in range(gt):
        ug = ut[gi].reshape(w, bs)
        x0 = x0_ref[gi]
        x0s = jnp.concatenate([x0[SSM_P:], x0[:SSM_P]], axis=0)
        y_t = _dot(kt_ref[gi, :w, :w], ug) + _dot(ct_ref[gi, :w, :], x0.astype(BF16))
        x1_ref[gi] = p_ref[gi] * x0 + q_ref[gi] * x0s + _dot(bt_ref[gi, :SW, CW - w:], ug)
        yt[gi] = y_t.reshape(ts, SSM_H, bs)
    for t in range(ts):
        y_ref[pl.ds(t, bs, stride=ts), :] = yt[:, t].reshape(gt * SSM_H, bs).T


def _s5_sample(u2, l, x0_t, k_t, b_t, c_t, p8b, q8b, ts, c_off):
    rows, _ = u2.shape
    _, g, _, bs = x0_t.shape
    gt = GROUPS_PER_LANE_TILE
    lt = gt * SSM_H
    off = c_off // lt

    def wspec(r, c):
        return pl.BlockSpec((None, gt, r, c), lambda j: (l, j, 0, 0))
    return pl.pallas_call(
        functools.partial(_s5_sample_kernel, ts=ts),
        grid=(g // gt,),
        in_specs=[pl.BlockSpec((rows, lt), lambda j: (0, off + j)),
                  wspec(SW, bs), wspec(CW, CW), wspec(2 * SW, CW), wspec(CW, SW), wspec(SW, bs), wspec(SW, bs)],
        out_specs=[pl.BlockSpec((rows, lt), lambda j: (0, j)), pl.BlockSpec((gt, SW, bs), lambda j: (j, 0, 0))],
        out_shape=[jax.ShapeDtypeStruct((rows, g * SSM_H), F32), jax.ShapeDtypeStruct((g, SW, bs), F32)],
        scratch_shapes=[pltpu.VMEM((gt, ts, SSM_H, bs), BF16), pltpu.VMEM((gt, ts, SSM_H, bs), F32)],
        compiler_params=_cparams(("arbitrary",)),
        name="s5_sample",
    )(u2, x0_t, k_t, b_t, c_t, p8b, q8b)


def _mix_kernel(x_ref, up_ref, us_ref, ys_ref, st_ref, g1_ref, wp_ref, ps_ref, dsk_ref, wglu_ref,
                wout_ref, o_ref, np_ref, ext, *, start_pos, carry):
    nb, tt, d = x_ref.shape
    c = up_ref.shape[-1]
    pg = c // len(POOL_WINDOWS)
    rows = nb * tt
    i = pl.program_id(1)

    @pl.when(i == 0)
    def _():
        ext[:, HALO - POOL_BUF:HALO, :] = st_ref[...]
    ext[:, HALO:HALO + tt, :] = up_ref[...]
    np_ref[...] = ext[:, HALO + tt - POOL_BUF:HALO + tt, :]

    pos = start_pos + i * tt + lax.broadcasted_iota(I32, (1, tt, 1), 1)
    outs = []
    for gidx, w in enumerate(POOL_WINDOWS):
        c0 = gidx * pg
        cur = ext[:, HALO:HALO + tt, c0:c0 + pg]
        s = cur
        for k in range(1, w):
            s = s + ext[:, HALO - k:HALO - k + tt, c0:c0 + pg]
        cnt = jnp.minimum(pos + 1, w).astype(F32)
        dd = s / cnt - cur
        yg = _dot(dd.reshape(rows, pg).astype(BF16), wp_ref[gidx])
        outs.append(yg * ps_ref[:, c0:c0 + pg])
    y_pool = jnp.concatenate(outs, axis=-1)

    if carry:
        ext[:, 0:HALO, :] = ext[:, tt:tt + HALO, :]

    us = us_ref[...].reshape(rows, c)
    yf = ys_ref[...].reshape(rows, c) + dsk_ref[...] * us
    gl = jax.nn.gelu(yf, approximate=True)
    o = gl * jax.nn.sigmoid(_dot(gl.astype(BF16), wglu_ref[...]))
    mo = _dot(y_pool.astype(BF16), wout_ref[:c, :]) + _dot(o.astype(BF16), wout_ref[c:, :])
    o_ref[...] = x_ref[...] + g1_ref[...] * mo.reshape(nb, tt, d)


def _mix(x, u, y_ssm, pool_state, l_state, mod, l, wp, ps, dsk, wglu, wout, start_pos):
    b, t, d = x.shape
    c = y_ssm.shape[-1]
    nb, tt = _tile(b, t, ROWS_MIX)
    nt = t // tt
    assert nt == 1 or tt >= HALO
    return pl.pallas_call(
        functools.partial(_mix_kernel, start_pos=start_pos, carry=nt > 1),
        grid=(b // nb, nt),
        in_specs=[pl.BlockSpec((nb, tt, d), lambda bi, i: (bi, i, 0)),
                  pl.BlockSpec((nb, tt, c), lambda bi, i: (bi, i, 0)),
                  pl.BlockSpec((nb, tt, c), lambda bi, i: (bi, i, 1)),
                  pl.BlockSpec((nb, tt, c), lambda bi, i: (bi, i, 0)),
                  pl.BlockSpec((None, nb, POOL_BUF, c), lambda bi, i: (l_state, bi, 0, 0)),
                  _mod_spec(nb, d, 2, l),
                  _layer_spec(wp.shape[1:], l), _layer_spec((1, c), l), _layer_spec((1, c), l),
                  _layer_spec((c, c), l), _layer_spec((d, d), l)],
        out_specs=[pl.BlockSpec((nb, tt, d), lambda bi, i: (bi, i, 0)),
                   pl.BlockSpec((nb, POOL_BUF, c), lambda bi, i: (bi, 0, 0))],
        out_shape=[jax.ShapeDtypeStruct((b, t, d), F32), jax.ShapeDtypeStruct((b, POOL_BUF, c), F32)],
        scratch_shapes=[pltpu.VMEM((nb, HALO + tt, c), F32)],
        compiler_params=_cparams(("arbitrary", "arbitrary")),
        name="pool_glu_proj_out",
    )(x, u, u, y_ssm, pool_state, mod, wp, ps, dsk, wglu, wout)


def _ffn_kernel(x_ref, g_ref, s_ref, sh_ref, g2_ref, wg_ref, wu_ref, wd_ref, o_ref, hb, acc, *, last_valid):
    nb, tt, d = x_ref.shape
    tf = wg_ref.shape[1]
    j = pl.program_id(2)
    nj = pl.num_programs(2)

    @pl.when(j == 0)
    def _():
        h = _modnorm(x_ref[...], g_ref[...], s_ref[...], sh_ref[...])
        hb[...] = h.reshape(nb * tt, d).astype(BF16)
        acc[...] = jnp.zeros_like(acc)

    def step(valid):
        h = hb[...]
        gate = _dot(h, wg_ref[...])
        up = _dot(h, wu_ref[...])
        a = gate * jax.nn.sigmoid(gate) * up
        wd = wd_ref[...]
        if valid < tf:
            a = jnp.where(lax.broadcasted_iota(I32, (1, tf), 1) < valid, a, 0.0)
            wd = jnp.where(lax.broadcasted_iota(I32, (tf, 1), 0) < valid, wd, jnp.zeros_like(wd))
        acc[...] += _dot(a.astype(BF16), wd)

    if last_valid == tf:
        step(tf)
    else:
        pl.when(j < nj - 1)(lambda: step(tf))
        pl.when(j == nj - 1)(lambda: step(last_valid))

    @pl.when(j == nj - 1)
    def _():
        o_ref[...] = x_ref[...] + g2_ref[...] * acc[...].reshape(nb, tt, d)


def _ffn(x, mod, l, g, wg, wu, wd):
    b, t, d = x.shape
    ff = wg.shape[1]
    nb, tt = _tile(b, t, ROWS_FFN)
    tf = min(FF_TILE, ff)
    nj = -(-ff // tf)
    return pl.pallas_call(
        functools.partial(_ffn_kernel, last_valid=ff - (nj - 1) * tf),
        grid=(b // nb, t // tt, nj),
        in_specs=[pl.BlockSpec((nb, tt, d), lambda bi, i, j: (bi, i, 0)),
                  pl.BlockSpec((1, d), lambda bi, i, j: (0, 0)),
                  _mod_spec(nb, d, 4, l), _mod_spec(nb, d, 3, l), _mod_spec(nb, d, 5, l),
                  pl.BlockSpec((d, tf), lambda bi, i, j: (0, j)),
                  pl.BlockSpec((d, tf), lambda bi, i, j: (0, j)),
                  pl.BlockSpec((tf, d), lambda bi, i, j: (j, 0))],
        out_specs=pl.BlockSpec((nb, tt, d), lambda bi, i, j: (bi, i, 0)),
        out_shape=jax.ShapeDtypeStruct((b, t, d), F32),
        scratch_shapes=[pltpu.VMEM((nb * tt, d), BF16), pltpu.VMEM((nb * tt, d), F32)],
        compiler_params=_cparams(("arbitrary", "arbitrary", "arbitrary")),
        name="dense_swiglu",
    )(x, g, mod, mod, mod, wg, wu, wd)


ROUTE_LANES = LANES_V7X
ROUTE_ROWS = SUBLANES_V7X


def _route_kernel(x_ref, g_ref, s_ref, sh_ref, wr_ref, br_ref, cin_ref, h_ref, r_ref, rt_ref, cout_ref, carry):
    nb, tt, d = x_ref.shape
    tm = nb * tt
    first = jnp.logical_and(pl.program_id(0) == 0, pl.program_id(1) == 0)

    @pl.when(first)
    def _():
        carry[...] = cin_ref[...]

    h = _modnorm(x_ref[...], g_ref[...], s_ref[...], sh_ref[...]).reshape(tm, d)
    h_ref[...] = h
    h_hi = h.astype(BF16)
    h_lo = (h - h_hi.astype(F32)).astype(BF16)
    logits = _dot(h_hi, wr_ref[0]) + (_dot(h_hi, wr_ref[1]) + _dot(h_lo, wr_ref[0])) + br_ref[...]
    lane = lax.broadcasted_iota(I32, (tm, ROUTE_LANES), 1)
    neg = jnp.float32(-jnp.inf)
    l1 = jnp.where(lane < N_EXPERTS, logits, neg)
    m1 = jnp.max(l1, axis=-1, keepdims=True)
    i1 = jnp.min(jnp.where(l1 == m1, lane, ROUTE_LANES), axis=-1, keepdims=True)
    l2 = jnp.where(lane == i1, neg, l1)
    m2 = jnp.max(l2, axis=-1, keepdims=True)
    i2 = jnp.min(jnp.where(l2 == m2, lane, ROUTE_LANES), axis=-1, keepdims=True)
    e2 = jnp.exp(m2 - m1)
    den = 1.0 + e2
    p1 = 1.0 / den
    p2 = e2 / den

    onehot = jnp.logical_or(lane == i1, lane == i2).astype(F32)
    rr = lax.broadcasted_iota(I32, (tm, tm), 0)
    cc = lax.broadcasted_iota(I32, (tm, tm), 1)
    before = (cc < rr).astype(BF16)
    rank = _dot(before, onehot.astype(BF16)) + carry[...]
    r1 = jnp.sum(jnp.where(lane == i1, rank, 0.0), axis=-1, keepdims=True)
    r2 = jnp.sum(jnp.where(lane == i2, rank, 0.0), axis=-1, keepdims=True)
    carry[...] += jnp.sum(onehot, axis=0, keepdims=True)

    out = jnp.where(lane == 0, i1.astype(F32), 0.0)
    out = jnp.where(lane == 1, i2.astype(F32), out)
    out = jnp.where(lane == 2, p1, out)
    out = jnp.where(lane == 3, p2, out)
    out = jnp.where(lane == 4, r1, out)
    out = jnp.where(lane == 5, r2, out)
    r_ref[...] = out
    rt_ref[...] = out.T[:ROUTE_ROWS, :]
    cout_ref[...] = carry[...]


def _route(x, mod, l, g, wr_pad, br_pad, counts_in):
    b, t, d = x.shape
    nb, tt = _tile(b, t, ROWS_ROUTE)
    tm = nb * tt
    nt = t // tt
    const2 = lambda bi, i: (0, 0)
    return pl.pallas_call(
        _route_kernel,
        grid=(b // nb, nt),
        in_specs=[pl.BlockSpec((nb, tt, d), lambda bi, i: (bi, i, 0)),
                  pl.BlockSpec((1, d), const2),
                  _mod_spec(nb, d, 4, l), _mod_spec(nb, d, 3, l),
                  pl.BlockSpec((2, d, ROUTE_LANES), lambda bi, i: (0, 0, 0)),
                  pl.BlockSpec((1, ROUTE_LANES), const2),
                  pl.BlockSpec((1, ROUTE_LANES), const2)],
        out_specs=[pl.BlockSpec((tm, d), lambda bi, i: (bi * nt + i, 0)),
                   pl.BlockSpec((tm, ROUTE_LANES), lambda bi, i: (bi * nt + i, 0)),
                   pl.BlockSpec((ROUTE_ROWS, tm), lambda bi, i: (0, bi * nt + i)),
                   pl.BlockSpec((1, ROUTE_LANES), const2)],
        out_shape=[jax.ShapeDtypeStruct((b * t, d), F32),
                   jax.ShapeDtypeStruct((b * t, ROUTE_LANES), F32),
                   jax.ShapeDtypeStruct((ROUTE_ROWS, b * t), F32),
                   jax.ShapeDtypeStruct((1, ROUTE_LANES), F32)],
        scratch_shapes=[pltpu.VMEM((1, ROUTE_LANES), F32)],
        compiler_params=_cparams(("arbitrary", "arbitrary")),
        name="moe_route",
    )(x, g, mod, mod, wr_pad, br_pad, counts_in)


def _row_gather_copy(src_hbm, tok, dst, r, sem):
    return pltpu.make_async_copy(src_hbm.at[pl.ds(tok, 1), :], dst.at[pl.ds(r, 1), :], sem)


def _moe_kernel(te_ref, tr_ref, src_ref, h_hbm, wg_ref, wu_ref, wd_ref, o_hbm,
                xb, acc, stage, wgu, wdb, gsem, osem):
    s = pl.program_id(0)
    j = pl.program_id(1)
    ns = pl.num_programs(0)
    nj = pl.num_programs(1)
    rows = tr_ref[s]
    nsub = (rows + MOE_SUB - 1) // MOE_SUB
    tile_rows = xb.shape[0]

    def out_copy():
        return pltpu.make_async_copy(acc, o_hbm.at[pl.ds(pl.multiple_of(s * tile_rows, MOE_SUB), tile_rows), :], osem)

    @pl.when(j == 0)
    def _():
        @pl.when(rows > 0)
        def _gather():
            def issue(sub, slot):
                base = s * tile_rows + sub * MOE_SUB

                def body(r, carry):
                    _row_gather_copy(h_hbm, src_ref[base + r], stage.at[slot], r, gsem.at[slot]).start()
                    return carry
                lax.fori_loop(0, MOE_SUB, body, 0, unroll=GATHER_UNROLL)

            issue(0, 0)

            def sub_body(sub, carry):
                slot = sub % 2

                @pl.when(sub + 1 < nsub)
                def _():
                    issue(sub + 1, 1 - slot)
                pltpu.make_async_copy(h_hbm.at[pl.ds(0, MOE_SUB), :], stage.at[slot], gsem.at[slot]).wait()
                r0 = pl.multiple_of(sub * MOE_SUB, MOE_SUB)
                xb[pl.ds(r0, MOE_SUB), :] = stage[slot].astype(BF16)
                return carry
            lax.fori_loop(0, nsub, sub_body, 0)

        @pl.when(s > 0)
        def _():
            out_copy().wait()
        acc[...] = jnp.zeros_like(acc)

    @pl.when(rows > 0)
    def _():
        tf = wg_ref.shape[1]
        wgu[:, :tf] = wg_ref[...].astype(BF16)
        wgu[:, tf:] = wu_ref[...].astype(BF16)
        wdb[...] = wd_ref[...].astype(BF16)

        def block(r0, m):
            gu = _dot(xb[pl.ds(r0, m), :], wgu[...])
            gate, up = gu[:, :tf], gu[:, tf:]
            a = (gate * jax.nn.sigmoid(gate) * up).astype(BF16)
            acc[pl.ds(r0, m), :] += _dot(a, wdb[...])

        per_big = MOE_BLK_BIG // MOE_SUB
        per_blk = MOE_BLK // MOE_SUB
        nbig = nsub // per_big
        rem_big = nsub - nbig * per_big
        has_blk = rem_big >= per_blk
        rem = rem_big - jnp.where(has_blk, per_blk, 0)
        r_blk = pl.multiple_of(nbig * MOE_BLK_BIG, MOE_BLK)
        r_tail = pl.multiple_of(r_blk + jnp.where(has_blk, MOE_BLK, 0), MOE_SUB)

        def body(blk, carry):
            block(pl.multiple_of(blk * MOE_BLK_BIG, MOE_BLK_BIG), MOE_BLK_BIG)
            return carry
        lax.fori_loop(0, nbig, body, 0)
        pl.when(has_blk)(lambda: block(r_blk, MOE_BLK))
        for k in range(1, per_blk):
            @pl.when(rem == k)
            def _(k=k):
                block(r_tail, k * MOE_SUB)

    @pl.when(j == nj - 1)
    def _():
        out_copy().start()

        @pl.when(s == ns - 1)
        def _():
            out_copy().wait()


def _moe_experts(h_all, te, tr, src, wg, wu, wd, n_tiles):
    n, d = h_all.shape
    e, _, ff = wg.shape
    tf = min(MOE_FF_TILE, ff)
    assert ff % tf == 0 and MOE_TILE_ROWS % MOE_BLK == 0 and MOE_BLK % MOE_SUB == 0 and MOE_BLK_BIG == 2 * MOE_BLK
    nj = ff // tf
    r = MOE_TILE_ROWS

    def jj(s, j, tr_ref):
        return jnp.where(tr_ref[s] > 0, j, nj - 1)

    grid_spec = pltpu.PrefetchScalarGridSpec(
        num_scalar_prefetch=3,
        grid=(n_tiles, nj),
        in_specs=[pl.BlockSpec(memory_space=pl.ANY),
                  pl.BlockSpec((None, d, tf), lambda s, j, te_r, tr_r, src_r: (te_r[s], 0, jj(s, j, tr_r))),
                  pl.BlockSpec((None, d, tf), lambda s, j, te_r, tr_r, src_r: (te_r[s], 0, jj(s, j, tr_r))),
                  pl.BlockSpec((None, tf, d), lambda s, j, te_r, tr_r, src_r: (te_r[s], jj(s, j, tr_r), 0))],
        out_specs=pl.BlockSpec(memory_space=pl.ANY),
        scratch_shapes=[pltpu.VMEM((r, d), BF16), pltpu.VMEM((r, d), F32), pltpu.VMEM((2, MOE_SUB, d), F32),
                        pltpu.VMEM((d, 2 * tf), BF16), pltpu.VMEM((tf, d), BF16),
                        pltpu.SemaphoreType.DMA((2,)), pltpu.SemaphoreType.DMA(())],
    )
    return pl.pallas_call(
        _moe_kernel,
        grid_spec=grid_spec,
        out_shape=jax.ShapeDtypeStruct((n_tiles * r, d), F32),
        compiler_params=_cparams(("arbitrary", "arbitrary")),
        name="moe_experts",
    )(te, tr, src, h_all, wg, wu, wd)


def _combine_kernel(s1_ref, s2_ref, x_ref, g2_ref, r_ref, gf_ref, e_hbm, o_ref, st1, st2, sem):
    nb, tt, d = x_ref.shape
    tm = nb * tt
    step = pl.program_id(0) * pl.num_programs(1) + pl.program_id(1)
    nsteps = pl.num_programs(0) * pl.num_programs(1)
    slot = step % 2

    def issue(tile, sl):
        base = tile * tm

        def body(r, carry):
            _row_gather_copy(e_hbm, s1_ref[base + r], st1.at[sl], r, sem.at[0, sl]).start()
            _row_gather_copy(e_hbm, s2_ref[base + r], st2.at[sl], r, sem.at[1, sl]).start()
            return carry
        lax.fori_loop(0, tm, body, 0, unroll=GATHER_UNROLL)

    pl.when(step == 0)(lambda: issue(0, 0))
    pl.when(step + 1 < nsteps)(lambda: issue(step + 1, 1 - slot))
    pltpu.make_async_copy(e_hbm.at[pl.ds(0, tm), :], st1.at[slot], sem.at[0, slot]).wait()
    pltpu.make_async_copy(e_hbm.at[pl.ds(0, tm), :], st2.at[slot], sem.at[1, slot]).wait()

    f = r_ref[:, 2:3] * st1[slot] + r_ref[:, 3:4] * st2[slot]
    x2 = x_ref[...] + g2_ref[...] * f.reshape(nb, tt, d)
    rs = lax.rsqrt(jnp.mean(x2 * x2, axis=-1, keepdims=True) + EPS)
    o_ref[...] = (x2 * rs) * gf_ref[...]


def _combine(x, mod, l, route, slot1, slot2, e_out, gf):
    b, t, d = x.shape
    nb, tt = _tile(b, t, ROWS_COMBINE)
    tm = nb * tt
    nt = t // tt
    grid_spec = pltpu.PrefetchScalarGridSpec(
        num_scalar_prefetch=2,
        grid=(b // nb, nt),
        in_specs=[pl.BlockSpec((nb, tt, d), lambda bi, i, a, c: (bi, i, 0)),
                  _mod_spec(nb, d, 5, l),
                  pl.BlockSpec((tm, ROUTE_LANES), lambda bi, i, a, c: (bi * nt + i, 0)),
                  pl.BlockSpec((1, d), lambda bi, i, a, c: (0, 0)),
                  pl.BlockSpec(memory_space=pl.ANY)],
        out_specs=pl.BlockSpec((nb, tt, d), lambda bi, i, a, c: (bi, i, 0)),
        scratch_shapes=[pltpu.VMEM((2, tm, d), F32), pltpu.VMEM((2, tm, d), F32), pltpu.SemaphoreType.DMA((2, 2))],
    )
    return pl.pallas_call(
        _combine_kernel,
        grid_spec=grid_spec,
        out_shape=jax.ShapeDtypeStruct((b, t, d), F32),
        compiler_params=_cparams(("arbitrary", "arbitrary")),
        name="moe_combine_norm",
    )(slot1, slot2, x, mod, route, gf, e_out)


def _lookup(table, idx):
    out = jnp.zeros_like(idx)
    for e in range(N_EXPERTS):
        out = jnp.where(idx == e, table[e], out)
    return out


def _floor_div(a, b):
    q = jnp.floor(a.astype(F32) / b.astype(F32)).astype(I32)
    q = jnp.where((q + 1) * b <= a, q + 1, q)
    return jnp.where(q * b > a, q - 1, q)


def _moe_layout(route_t, counts_f):
    n = route_t.shape[1]
    r, sub = MOE_TILE_ROWS, MOE_SUB
    n_tiles = -(-TOP_K * n // r) + N_EXPERTS
    i1 = route_t[0].astype(I32)
    i2 = route_t[1].astype(I32)
    r1 = route_t[4].astype(I32)
    r2 = route_t[5].astype(I32)
    counts = counts_f[0, :N_EXPERTS].astype(I32)
    nt = (counts + r - 1) // r
    ntc = jnp.maximum(nt, 1)
    sz = jnp.maximum((((counts + ntc - 1) // ntc + sub - 1) // sub) * sub, sub)
    cum = jnp.cumsum(nt)
    tstart = cum - nt
    total = cum[-1]
    s_idx = jnp.arange(n_tiles, dtype=I32)
    te_raw = jnp.minimum(jnp.sum((s_idx[:, None] >= cum[None, :]).astype(I32), axis=1), N_EXPERTS - 1)
    used = s_idx < total
    last_e = jnp.max(jnp.where(used, te_raw, 0))
    te = jnp.where(used, te_raw, last_e)
    sz_t = _lookup(sz, te)
    k_in = s_idx - _lookup(tstart, te)
    tr = jnp.where(used, jnp.clip(_lookup(counts, te) - k_in * sz_t, 0, sz_t), 0).astype(I32)

    def slot(ei, ri):
        sz_i = _lookup(sz, ei)
        k = _floor_div(ri, sz_i)
        return (_lookup(tstart, ei) + k) * r + (ri - k * sz_i)
    slot1, slot2 = slot(i1, r1), slot(i2, r2)
    tok = jnp.arange(n, dtype=I32)
    src = jnp.zeros((n_tiles * r,), I32).at[jnp.concatenate([slot1, slot2])].set(jnp.concatenate([tok, tok]))
    return te, tr, src, slot1, slot2, n_tiles


def kernel(x_prompt, x_sample, c_prompt, c_sample, state_pool, state_ssm_re, state_ssm_im, w_ada, b_ada, norm_mix, norm_ffn, w_in, w_pool, pool_scale, ssm_lam_re, ssm_lam_im, ssm_log_step, ssm_b_re, ssm_b_im, ssm_c_re, ssm_c_im, ssm_d, ssm_w_glu, w_out, ffn_w_gate, ffn_w_up, ffn_w_down, moe_w_router, moe_b_router, moe_w_gate, moe_w_up, moe_w_down, norm_final):
    bp, tp, d = x_prompt.shape
    bs, ts, _ = x_sample.shape
    depth = w_ada.shape[0]
    c_pool = w_pool.shape[1] * w_pool.shape[2]
    assert tp % CHUNK == 0 and ts <= CHUNK

    m_rows = -(-(bp + bs) // SUBLANES_V7X) * SUBLANES_V7X
    c_all = jnp.concatenate([c_prompt, c_sample, jnp.zeros((m_rows - bp - bs, d), F32)], axis=0)
    mod_all = _ada(c_all, w_ada, b_ada)

    mod_p = mod_all[:, :bp].reshape(depth, bp, 1, 6 * d)
    mod_s = mod_all[:, bp:bp + bs].reshape(depth, bs, 1, 6 * d)
    w_in_b, w_out_b = w_in.astype(BF16), w_out.astype(BF16)
    wp_b, wglu_b = w_pool.astype(BF16), ssm_w_glu.astype(BF16)
    ps = pool_scale.reshape(depth, 1, c_pool).astype(F32)
    dsk = ssm_d.reshape(depth, 1, d - c_pool).astype(F32)
    k_t, b_t, c_t, p16, q16, p8b, q8b = jax.vmap(lambda *a: _s5_params(*a, ts, bs))(
        ssm_lam_re, ssm_lam_im, ssm_log_step, ssm_b_re, ssm_b_im, ssm_c_re, ssm_c_im)
    x0_t = jnp.concatenate([state_ssm_re, state_ssm_im], axis=-1).astype(F32).transpose(0, 2, 3, 1)
    st_p = jnp.zeros((1, bp, POOL_BUF, c_pool), F32)
    st_s = state_pool.astype(F32)

    xp, xs = x_prompt, x_sample
    pool_p, re_p, im_p, pool_s, re_s, im_s = [], [], [], [], [], []
    y_p = y_s = None
    for l in range(depth):
        gm = norm_mix[l].reshape(1, d)
        gn = norm_ffn[l].reshape(1, d)

        up_all = _proj_in(xp, mod_p, l, gm, w_in_b)
        us_all = _proj_in(xs, mod_s, l, gm, w_in_b)
        y_ssm_p, xend = _s5_prompt(up_all, l, k_t, b_t, c_t, p16, q16, c_pool)
        ys2, x1_t = _s5_sample(us_all.reshape(bs * ts, d), l, x0_t, k_t, b_t, c_t, p8b, q8b, ts, c_pool)
        y_ssm_s = ys2.reshape(bs, ts, d - c_pool)
        xp, new_pool_p = _mix(xp, up_all, y_ssm_p, st_p, 0, mod_p, l, wp_b, ps, dsk, wglu_b, w_out_b, 0)
        xs, new_pool_s = _mix(xs, us_all, y_ssm_s, st_s, l, mod_s, l, wp_b, ps, dsk, wglu_b, w_out_b, PAST_LEN)
        pool_p.append(new_pool_p)
        pool_s.append(new_pool_s)
        re_p.append(xend[..., :SSM_P])
        im_p.append(xend[..., SSM_P:])
        re_s.append(x1_t[:, :SSM_P, :].transpose(2, 0, 1))
        im_s.append(x1_t[:, SSM_P:, :].transpose(2, 0, 1))

        i = l // 2
        if l % 2 == 0:
            wg = ffn_w_gate[i].astype(BF16)
            wu = ffn_w_up[i].astype(BF16)
            wd = ffn_w_down[i].astype(BF16)
            xp = _ffn(xp, mod_p, l, gn, wg, wu, wd)
            xs = _ffn(xs, mod_s, l, gn, wg, wu, wd)
        else:
            ne = moe_w_router.shape[-1]
            wr32 = jnp.pad(moe_w_router[i].astype(F32), ((0, 0), (0, ROUTE_LANES - ne)))
            wr_hi = wr32.astype(BF16)
            wr = jnp.stack([wr_hi, (wr32 - wr_hi.astype(F32)).astype(BF16)])
            br = jnp.pad(moe_b_router[i].astype(F32), (0, ROUTE_LANES - ne)).reshape(1, ROUTE_LANES)
            zero_counts = jnp.zeros((1, ROUTE_LANES), F32)
            h_p, route_p, rt_p, cnt_p = _route(xp, mod_p, l, gn, wr, br, zero_counts)
            h_s, route_s, rt_s, cnt_s = _route(xs, mod_s, l, gn, wr, br, cnt_p)
            h_all = jnp.concatenate([h_p, h_s], axis=0)
            te, tr, src, slot1, slot2, n_tiles = _moe_layout(jnp.concatenate([rt_p, rt_s], axis=1), cnt_s)
            e_out = _moe_experts(h_all, te, tr, src, moe_w_gate[i], moe_w_up[i], moe_w_down[i], n_tiles)
            n_p = bp * tp
            last = l == depth - 1
            assert last, "the combine kernel also applies the final norm"
            gf = norm_final.reshape(1, d)
            y_p = _combine(xp, mod_p, l, route_p, slot1[:n_p], slot2[:n_p], e_out, gf)
            y_s = _combine(xs, mod_s, l, route_s, slot1[n_p:], slot2[n_p:], e_out, gf)

    return (y_p, y_s, jnp.stack(pool_p), jnp.stack(re_p), jnp.stack(im_p),
            jnp.stack(pool_s), jnp.stack(re_s), jnp.stack(im_s))
```

```python
import functools
import math

import jax
import jax.numpy as jnp
import numpy as np
from jax import lax
from jax.experimental import pallas as pl
from jax.experimental.pallas import tpu as pltpu

F32 = jnp.float32
BF16 = jnp.bfloat16
I32 = jnp.int32

EPS = 1e-6
POOL_WINDOWS = (2, 4, 8, 16)
POOL_BUF = max(POOL_WINDOWS) - 1
SUBLANES_V7X = 8
BF16_SUBLANES_V7X = 16
LANES_V7X = 128
HALO = -(-POOL_BUF // SUBLANES_V7X) * SUBLANES_V7X
SSM_H = 16
SSM_P = 64
CHUNK = 16
CW = CHUNK * SSM_H
SW = 2 * SSM_P
N_EXPERTS = 8
TOP_K = 2
PAST_LEN = 16384

VMEM_LIMIT_V7X = 56 * 1024 * 1024
ROWS_IN = 512
ROWS_MIX = 256
ROWS_FFN = 512
FF_TILE = 512
ROWS_ROUTE = 512
MOE_TILE_ROWS = 2560
MOE_SUB = 128
MOE_BLK_BIG = 1024
MOE_BLK = 512
GATHER_UNROLL = 8
MOE_FF_TILE = 256
ROWS_COMBINE = 256
ADA_TN = 1024


def _cparams(sem):
    return pltpu.CompilerParams(dimension_semantics=sem, vmem_limit_bytes=VMEM_LIMIT_V7X)


def _tile(b, t, rows):
    if t >= rows:
        assert t % rows == 0
        return 1, rows
    nb = max(1, min(b, rows // t))
    assert b % nb == 0
    return nb, t


def _modnorm(x, g, s, sh):
    r = lax.rsqrt(jnp.mean(x * x, axis=-1, keepdims=True) + EPS)
    return (x * r) * g * (1.0 + s) + sh


def _dot(a, b):
    return jnp.dot(a, b, preferred_element_type=F32)


def _mod_spec(nb, d, k, l):
    return pl.BlockSpec((None, nb, 1, d), lambda b, i, *_: (l, b, 0, k))


def _layer_spec(shape, l):
    return pl.BlockSpec((None,) + tuple(shape), lambda *_: (l,) + (0,) * len(shape))


def _ada_kernel(c_ref, w_ref, b_ref, o_ref):
    c = c_ref[...]
    sc = c * jax.nn.sigmoid(c)
    o_ref[0] = _dot(sc.astype(BF16), w_ref[0].astype(BF16)) + b_ref[0]


def _ada(c_all, w_ada, b_ada):
    m, d = c_all.shape
    depth, _, n6 = w_ada.shape
    tn = math.gcd(ADA_TN, n6)
    return pl.pallas_call(
        _ada_kernel,
        grid=(depth, n6 // tn),
        in_specs=[pl.BlockSpec((m, d), lambda l, n: (0, 0)),
                  pl.BlockSpec((1, d, tn), lambda l, n: (l, 0, n)),
                  pl.BlockSpec((1, 1, tn), lambda l, n: (l, 0, n))],
        out_specs=pl.BlockSpec((1, m, tn), lambda l, n: (l, 0, n)),
        out_shape=jax.ShapeDtypeStruct((depth, m, n6), F32),
        compiler_params=_cparams(("arbitrary", "arbitrary")),
        name="adaln_mod",
    )(c_all, w_ada, b_ada.reshape(depth, 1, n6))


def _in_kernel(x_ref, g_ref, s_ref, sh_ref, w_ref, u_ref):
    nb, tt, d = x_ref.shape
    h = _modnorm(x_ref[...], g_ref[...], s_ref[...], sh_ref[...])
    u = _dot(h.reshape(nb * tt, d).astype(BF16), w_ref[...])
    u_ref[...] = u.reshape(nb, tt, u_ref.shape[-1])


def _proj_in(x, mod, l, g, w_bf16):
    b, t, d = x.shape
    n = w_bf16.shape[-1]
    nb, tt = _tile(b, t, ROWS_IN)
    return pl.pallas_call(
        _in_kernel,
        grid=(b // nb, t // tt),
        in_specs=[pl.BlockSpec((nb, tt, d), lambda bi, i: (bi, i, 0)),
                  pl.BlockSpec((1, d), lambda bi, i: (0, 0)),
                  _mod_spec(nb, d, 1, l), _mod_spec(nb, d, 0, l),
                  _layer_spec((d, n), l)],
        out_specs=pl.BlockSpec((nb, tt, n), lambda bi, i: (bi, i, 0)),
        out_shape=jax.ShapeDtypeStruct((b, t, n), F32),
        compiler_params=_cparams(("arbitrary", "arbitrary")),
        name="norm_proj_in",
    )(x, g, mod, mod, w_bf16)


GROUPS_PER_LANE_TILE = LANES_V7X // SSM_H


def _toeplitz_placement():
    toe = np.zeros((CHUNK, CW, CW), np.float32)
    for s in range(CHUNK):
        for h in range(SSM_H):
            for t in range(s, CHUNK):
                toe[t, (t - s) * SSM_H + h, s * SSM_H + h] = 1.0
    return jnp.asarray(toe)


def _s5_params(lam_re, lam_im, log_step, b_re, b_im, c_re, c_im, ts, bs):
    hp = lax.Precision.HIGHEST
    toe = _toeplitz_placement()
    lr, li = lam_re.astype(F32), lam_im.astype(F32)
    step = jnp.exp(log_step.astype(F32))[:, None]
    mag = jnp.exp(lr * step)
    a_re = mag * jnp.cos(li * step)
    a_im = mag * jnp.sin(li * step)
    den = lr * lr + li * li
    nr = a_re - 1.0
    f_re = (nr * lr + a_im * li) / den
    f_im = (a_im * lr - nr * li) / den
    br, bi = b_re.astype(F32), b_im.astype(F32)
    bb_re = f_re[..., None] * br - f_im[..., None] * bi
    bb_im = f_re[..., None] * bi + f_im[..., None] * br
    cr, ci = c_re.astype(F32), c_im.astype(F32)
    g = lr.shape[0]

    pw_re, pw_im = [jnp.ones_like(a_re)], [jnp.zeros_like(a_im)]
    for _ in range(CHUNK):
        pr, pi = pw_re[-1], pw_im[-1]
        pw_re.append(pr * a_re - pi * a_im)
        pw_im.append(pr * a_im + pi * a_re)
    pw_re, pw_im = jnp.stack(pw_re), jnp.stack(pw_im)

    bbt_re, bbt_im = bb_re.transpose(0, 2, 1), bb_im.transpose(0, 2, 1)
    w_re = pw_re[:CHUNK, :, None, :] * bbt_re - pw_im[:CHUNK, :, None, :] * bbt_im
    w_im = pw_re[:CHUNK, :, None, :] * bbt_im + pw_im[:CHUNK, :, None, :] * bbt_re
    m = (jnp.einsum('gop,lgip->goli', cr, w_re, precision=hp)
         - jnp.einsum('gop,lgip->goli', ci, w_im, precision=hp)).reshape(g, SSM_H, CW)
    k_t = jnp.einsum('gok,tkn->gton', m.astype(BF16), toe.astype(BF16),
                     preferred_element_type=F32).reshape(g, CW, CW)

    rev_re, rev_im = pw_re[:CHUNK][::-1], pw_im[:CHUNK][::-1]
    ar = jnp.repeat(rev_re.transpose(1, 2, 0), SSM_H, axis=-1)
    ai = jnp.repeat(rev_im.transpose(1, 2, 0), SSM_H, axis=-1)
    xr = jnp.tile(bb_re, (1, 1, CHUNK))
    xi = jnp.tile(bb_im, (1, 1, CHUNK))
    bt_re = ar * xr - ai * xi
    bt_im = ar * xi + ai * xr
    b_t = jnp.concatenate([bt_re, bt_im, bt_im, bt_re], axis=1)

    qr = pw_re[1:].transpose(1, 0, 2)[:, :, None, :]
    qi = pw_im[1:].transpose(1, 0, 2)[:, :, None, :]
    cm_re = cr[:, None] * qr - ci[:, None] * qi
    cm_im = -(cr[:, None] * qi + ci[:, None] * qr)
    c_t = jnp.concatenate([cm_re, cm_im], axis=-1).reshape(g, CW, SW)

    def coef(k):
        pr, pi = pw_re[k], pw_im[k]
        return (jnp.concatenate([pr, pr, pr, pr], axis=-1),
                jnp.concatenate([-pi, pi, pi, -pi], axis=-1))
    p16, q16 = coef(CHUNK)
    p8, q8 = coef(ts)
    p8b = jnp.broadcast_to(p8[:, :SW, None], (g, SW, bs))
    q8b = jnp.broadcast_to(q8[:, :SW, None], (g, SW, bs))
    return k_t.astype(BF16), b_t.astype(BF16), c_t.astype(BF16), p16, q16, p8b, q8b


def _s5_prompt_kernel(us_ref, kt_ref, bt_ref, ct_ref, p_ref, q_ref, y_ref, xe_ref, ut, yt, scan_x, scan_xs, s_scr):
    gt = ut.shape[0]
    nchunk = ut.shape[-1]
    for t in range(CHUNK):
        slab = us_ref[0, pl.ds(t, nchunk, stride=CHUNK), :]
        ut[:, t] = slab.T.reshape(gt, SSM_H, nchunk).astype(BF16)
    for gi in range(gt):
        buz = _dot(bt_ref[gi], ut[gi].reshape(CW, nchunk))
        buz_t = buz.T
        scan_x[pl.ds(gi, nchunk, stride=gt), :] = buz_t[:, :SW]
        scan_xs[pl.ds(gi, nchunk, stride=gt), :] = buz_t[:, SW:]
    px, pxs = p_ref[:, :SW], p_ref[:, SW:]
    qx, qxs = q_ref[:, :SW], q_ref[:, SW:]
    x = jnp.zeros((gt, SW), F32)
    xs = jnp.zeros((gt, SW), F32)
    for c in range(nchunk):
        lo, hi = c * gt, (c + 1) * gt
        s_scr[lo:hi, :] = x
        x, xs = (px * x + qx * xs + scan_x[lo:hi, :],
                 pxs * xs + qxs * x + scan_xs[lo:hi, :])
    xe_ref[0] = x
    for gi in range(gt):
        st = s_scr[pl.ds(gi, nchunk, stride=gt), :].T.astype(BF16)
        y_t = _dot(kt_ref[gi], ut[gi].reshape(CW, nchunk)) + _dot(ct_ref[gi], st)
        yt[gi] = y_t.reshape(CHUNK, SSM_H, nchunk)
    for t in range(CHUNK):
        y_ref[0, pl.ds(t, nchunk, stride=CHUNK), :] = yt[:, t].reshape(gt * SSM_H, nchunk).T


def _s5_prompt(u, l, k_t, b_t, c_t, p16, q16, c_off):
    b, t, _ = u.shape
    g = k_t.shape[1]
    gt = GROUPS_PER_LANE_TILE
    lt = gt * SSM_H
    nchunk = t // CHUNK
    off = c_off // lt

    def wspec(r, c):
        return pl.BlockSpec((None, gt, r, c), lambda bi, j: (l, j, 0, 0))
    return pl.pallas_call(
        _s5_prompt_kernel,
        grid=(b, g // gt),
        in_specs=[pl.BlockSpec((1, t, lt), lambda bi, j: (bi, 0, off + j)),
                  wspec(CW, CW), wspec(2 * SW, CW), wspec(CW, SW),
                  pl.BlockSpec((None, gt, 2 * SW), lambda bi, j: (l, j, 0)),
                  pl.BlockSpec((None, gt, 2 * SW), lambda bi, j: (l, j, 0))],
        out_specs=[pl.BlockSpec((1, t, lt), lambda bi, j: (bi, 0, j)),
                   pl.BlockSpec((1, gt, SW), lambda bi, j: (bi, j, 0))],
        out_shape=[jax.ShapeDtypeStruct((b, t, g * SSM_H), F32), jax.ShapeDtypeStruct((b, g, SW), F32)],
        scratch_shapes=[pltpu.VMEM((gt, CHUNK, SSM_H, nchunk), BF16), pltpu.VMEM((gt, CHUNK, SSM_H, nchunk), F32),
                        pltpu.VMEM((nchunk * gt, SW), F32), pltpu.VMEM((nchunk * gt, SW), F32),
                        pltpu.VMEM((nchunk * gt, SW), F32)],
        compiler_params=_cparams(("arbitrary", "arbitrary")),
        name="s5_prompt",
    )(u, k_t, b_t, c_t, p16, q16)


def _s5_sample_kernel(us_ref, x0_ref, kt_ref, bt_ref, ct_ref, p_ref, q_ref, y_ref, x1_ref, ut, yt, *, ts):
    gt = ut.shape[0]
    bs = ut.shape[-1]
    w = ts * SSM_H
    for t in range(ts):
        slab = us_ref[pl.ds(t, bs, stride=ts), :]
        ut[:, t] = slab.T.reshape(gt, SSM_H, bs).astype(BF16)
    for gi in range(gt):
        ug = ut[gi].reshape(w, bs)
        x0 = x0_ref[gi]
        x0s = jnp.concatenate([x0[SSM_P:], x0[:SSM_P]], axis=0)
        y_t = _dot(kt_ref[gi, :w, :w], ug) + _dot(ct_ref[gi, :w, :], x0.astype(BF16))
        x1_ref[gi] = p_ref[gi] * x0 + q_ref[gi] * x0s + _dot(bt_ref[gi, :SW, CW - w:], ug)
        yt[gi] = y_t.reshape(ts, SSM_H, bs)
    for t in range(ts):
        y_ref[pl.ds(t, bs, stride=ts), :] = yt[:, t].reshape(gt * SSM_H, bs).T


def _s5_sample(u2, l, x0_t, k_t, b_t, c_t, p8b, q8b, ts, c_off):
    rows, _ = u2.shape
    _, g, _, bs = x0_t.shape
    gt = GROUPS_PER_LANE_TILE
    lt = gt * SSM_H
    off = c_off // lt

    def wspec(r, c):
        return pl.BlockSpec((None, gt, r, c), lambda j: (l, j, 0, 0))
    return pl.pallas_call(
        functools.partial(_s5_sample_kernel, ts=ts),
        grid=(g // gt,),
        in_specs=[pl.BlockSpec((rows, lt), lambda j: (0, off + j)),
                  wspec(SW, bs), wspec(CW, CW), wspec(2 * SW, CW), wspec(CW, SW), wspec(SW, bs), wspec(SW, bs)],
        out_specs=[pl.BlockSpec((rows, lt), lambda j: (0, j)), pl.BlockSpec((gt, SW, bs), lambda j: (j, 0, 0))],
        out_shape=[jax.ShapeDtypeStruct((rows, g * SSM_H), F32), jax.ShapeDtypeStruct((g, SW, bs), F32)],
        scratch_shapes=[pltpu.VMEM((gt, ts, SSM_H, bs), BF16), pltpu.VMEM((gt, ts, SSM_H, bs), F32)],
        compiler_params=_cparams(("arbitrary",)),
        name="s5_sample",
    )(u2, x0_t, k_t, b_t, c_t, p8b, q8b)


def _mix_kernel(x_ref, up_ref, us_ref, ys_ref, st_ref, g1_ref, wp_ref, ps_ref, dsk_ref, wglu_ref,
                wout_ref, *rest, start_pos, carry, ncast):
    cast_in, (o_ref, np_ref), cast_out, ext = rest[:ncast], rest[ncast:ncast + 2], rest[ncast + 2:-1], rest[-1]
    for src, dst in zip(cast_in, cast_out):
        dst[...] = src[...].astype(BF16)
    nb, tt, d = x_ref.shape
    c = up_ref.shape[-1]
    pg = c // len(POOL_WINDOWS)
    rows = nb * tt
    i = pl.program_id(1)

    @pl.when(i == 0)
    def _():
        ext[:, HALO - POOL_BUF:HALO, :] = st_ref[...]
    ext[:, HALO:HALO + tt, :] = up_ref[...]
    np_ref[...] = ext[:, HALO + tt - POOL_BUF:HALO + tt, :]

    pos = start_pos + i * tt + lax.broadcasted_iota(I32, (1, tt, 1), 1)
    outs = []
    for gidx, w in enumerate(POOL_WINDOWS):
        c0 = gidx * pg
        cur = ext[:, HALO:HALO + tt, c0:c0 + pg]
        s = cur
        for k in range(1, w):
            s = s + ext[:, HALO - k:HALO - k + tt, c0:c0 + pg]
        cnt = jnp.minimum(pos + 1, w).astype(F32)
        dd = s / cnt - cur
        yg = _dot(dd.reshape(rows, pg).astype(BF16), wp_ref[gidx])
        outs.append(yg * ps_ref[:, c0:c0 + pg])
    y_pool = jnp.concatenate(outs, axis=-1)

    if carry:
        ext[:, 0:HALO, :] = ext[:, tt:tt + HALO, :]

    us = us_ref[...].reshape(rows, c)
    yf = ys_ref[...].reshape(rows, c) + dsk_ref[...] * us
    gl = jax.nn.gelu(yf, approximate=True)
    o = gl * jax.nn.sigmoid(_dot(gl.astype(BF16), wglu_ref[...]))
    mo = _dot(y_pool.astype(BF16), wout_ref[:c, :]) + _dot(o.astype(BF16), wout_ref[c:, :])
    o_ref[...] = x_ref[...] + g1_ref[...] * mo.reshape(nb, tt, d)


def _mix(x, u, y_ssm, pool_state, l_state, mod, l, wp, ps, dsk, wglu, wout, start_pos, cast=()):
    b, t, d = x.shape
    c = y_ssm.shape[-1]
    nb, tt = _tile(b, t, ROWS_MIX)
    nt = t // tt
    assert nt == 1 or tt >= HALO
    nsteps = (b // nb) * nt
    cast_specs = []
    for a in cast:
        blk = -(-(-(-a.shape[0] // nsteps)) // BF16_SUBLANES_V7X) * BF16_SUBLANES_V7X
        last = -(-a.shape[0] // blk) - 1
        cast_specs.append(pl.BlockSpec((blk, a.shape[1]), lambda bi, i, last=last: (jnp.minimum(bi * nt + i, last), 0)))
    outs = pl.pallas_call(
        functools.partial(_mix_kernel, start_pos=start_pos, carry=nt > 1, ncast=len(cast)),
        grid=(b // nb, nt),
        in_specs=[pl.BlockSpec((nb, tt, d), lambda bi, i: (bi, i, 0)),
                  pl.BlockSpec((nb, tt, c), lambda bi, i: (bi, i, 0)),
                  pl.BlockSpec((nb, tt, c), lambda bi, i: (bi, i, 1)),
                  pl.BlockSpec((nb, tt, c), lambda bi, i: (bi, i, 0)),
                  pl.BlockSpec((None, nb, POOL_BUF, c), lambda bi, i: (l_state, bi, 0, 0)),
                  _mod_spec(nb, d, 2, l),
                  _layer_spec(wp.shape[1:], l), _layer_spec((1, c), l), _layer_spec((1, c), l),
                  _layer_spec((c, c), l), _layer_spec((d, d), l)] + cast_specs,
        out_specs=[pl.BlockSpec((nb, tt, d), lambda bi, i: (bi, i, 0)),
                   pl.BlockSpec((nb, POOL_BUF, c), lambda bi, i: (bi, 0, 0))] + cast_specs,
        out_shape=[jax.ShapeDtypeStruct((b, t, d), F32), jax.ShapeDtypeStruct((b, POOL_BUF, c), F32)]
        + [jax.ShapeDtypeStruct(a.shape, BF16) for a in cast],
        scratch_shapes=[pltpu.VMEM((nb, HALO + tt, c), F32)],
        compiler_params=_cparams(("arbitrary", "arbitrary")),
        name="pool_glu_proj_out",
    )(x, u, u, y_ssm, pool_state, mod, wp, ps, dsk, wglu, wout, *cast)
    return outs[0], outs[1], tuple(outs[2:])


def _ffn_kernel(x_ref, g_ref, s_ref, sh_ref, g2_ref, wg_ref, wu_ref, wd_ref, o_ref, hb, acc, *, last_valid):
    nb, tt, d = x_ref.shape
    tf = wg_ref.shape[1]
    j = pl.program_id(2)
    nj = pl.num_programs(2)

    @pl.when(j == 0)
    def _():
        h = _modnorm(x_ref[...], g_ref[...], s_ref[...], sh_ref[...])
        hb[...] = h.reshape(nb * tt, d).astype(BF16)
        acc[...] = jnp.zeros_like(acc)

    def step(valid):
        h = hb[...]
        gate = _dot(h, wg_ref[...])
        up = _dot(h, wu_ref[...])
        a = gate * jax.nn.sigmoid(gate) * up
        wd = wd_ref[...]
        if valid < tf:
            a = jnp.where(lax.broadcasted_iota(I32, (1, tf), 1) < valid, a, 0.0)
            wd = jnp.where(lax.broadcasted_iota(I32, (tf, 1), 0) < valid, wd, jnp.zeros_like(wd))
        acc[...] += _dot(a.astype(BF16), wd)

    if last_valid == tf:
        step(tf)
    else:
        pl.when(j < nj - 1)(lambda: step(tf))
        pl.when(j == nj - 1)(lambda: step(last_valid))

    @pl.when(j == nj - 1)
    def _():
        o_ref[...] = x_ref[...] + g2_ref[...] * acc[...].reshape(nb, tt, d)


def _ffn(x, mod, l, g, wg, wu, wd):
    b, t, d = x.shape
    ff = wg.shape[1]
    nb, tt = _tile(b, t, ROWS_FFN)
    tf = min(FF_TILE, ff)
    nj = -(-ff // tf)
    return pl.pallas_call(
        functools.partial(_ffn_kernel, last_valid=ff - (nj - 1) * tf),
        grid=(b // nb, t // tt, nj),
        in_specs=[pl.BlockSpec((nb, tt, d), lambda bi, i, j: (bi, i, 0)),
                  pl.BlockSpec((1, d), lambda bi, i, j: (0, 0)),
                  _mod_spec(nb, d, 4, l), _mod_spec(nb, d, 3, l), _mod_spec(nb, d, 5, l),
                  pl.BlockSpec((d, tf), lambda bi, i, j: (0, j)),
                  pl.BlockSpec((d, tf), lambda bi, i, j: (0, j)),
                  pl.BlockSpec((tf, d), lambda bi, i, j: (j, 0))],
        out_specs=pl.BlockSpec((nb, tt, d), lambda bi, i, j: (bi, i, 0)),
        out_shape=jax.ShapeDtypeStruct((b, t, d), F32),
        scratch_shapes=[pltpu.VMEM((nb * tt, d), BF16), pltpu.VMEM((nb * tt, d), F32)],
        compiler_params=_cparams(("arbitrary", "arbitrary", "arbitrary")),
        name="dense_swiglu",
    )(x, g, mod, mod, mod, wg, wu, wd)


ROUTE_LANES = LANES_V7X
ROUTE_ROWS = SUBLANES_V7X


def _route_kernel(x_ref, g_ref, s_ref, sh_ref, wr_ref, br_ref, cin_ref, h_ref, r_ref, rt_ref, cout_ref, carry):
    nb, tt, d = x_ref.shape
    tm = nb * tt
    first = jnp.logical_and(pl.program_id(0) == 0, pl.program_id(1) == 0)

    @pl.when(first)
    def _():
        carry[...] = cin_ref[...]

    h = _modnorm(x_ref[...], g_ref[...], s_ref[...], sh_ref[...]).reshape(tm, d)
    h_ref[...] = h
    h_hi = h.astype(BF16)
    h_lo = (h - h_hi.astype(F32)).astype(BF16)
    logits = _dot(h_hi, wr_ref[0]) + (_dot(h_hi, wr_ref[1]) + _dot(h_lo, wr_ref[0])) + br_ref[...]
    lane = lax.broadcasted_iota(I32, (tm, ROUTE_LANES), 1)
    neg = jnp.float32(-jnp.inf)
    l1 = jnp.where(lane < N_EXPERTS, logits, neg)
    m1 = jnp.max(l1, axis=-1, keepdims=True)
    i1 = jnp.min(jnp.where(l1 == m1, lane, ROUTE_LANES), axis=-1, keepdims=True)
    l2 = jnp.where(lane == i1, neg, l1)
    m2 = jnp.max(l2, axis=-1, keepdims=True)
    i2 = jnp.min(jnp.where(l2 == m2, lane, ROUTE_LANES), axis=-1, keepdims=True)
    e2 = jnp.exp(m2 - m1)
    den = 1.0 + e2
    p1 = 1.0 / den
    p2 = e2 / den

    onehot = jnp.logical_or(lane == i1, lane == i2).astype(F32)
    rr = lax.broadcasted_iota(I32, (tm, tm), 0)
    cc = lax.broadcasted_iota(I32, (tm, tm), 1)
    before = (cc < rr).astype(BF16)
    rank = _dot(before, onehot.astype(BF16)) + carry[...]
    r1 = jnp.sum(jnp.where(lane == i1, rank, 0.0), axis=-1, keepdims=True)
    r2 = jnp.sum(jnp.where(lane == i2, rank, 0.0), axis=-1, keepdims=True)
    carry[...] += jnp.sum(onehot, axis=0, keepdims=True)

    out = jnp.where(lane == 0, i1.astype(F32), 0.0)
    out = jnp.where(lane == 1, i2.astype(F32), out)
    out = jnp.where(lane == 2, p1, out)
    out = jnp.where(lane == 3, p2, out)
    out = jnp.where(lane == 4, r1, out)
    out = jnp.where(lane == 5, r2, out)
    r_ref[...] = out
    rt_ref[...] = out.T[:ROUTE_ROWS, :]
    cout_ref[...] = carry[...]


def _route(x, mod, l, g, wr_pad, br_pad, counts_in):
    b, t, d = x.shape
    nb, tt = _tile(b, t, ROWS_ROUTE)
    tm = nb * tt
    nt = t // tt
    const2 = lambda bi, i: (0, 0)
    return pl.pallas_call(
        _route_kernel,
        grid=(b // nb, nt),
        in_specs=[pl.BlockSpec((nb, tt, d), lambda bi, i: (bi, i, 0)),
                  pl.BlockSpec((1, d), const2),
                  _mod_spec(nb, d, 4, l), _mod_spec(nb, d, 3, l),
                  pl.BlockSpec((2, d, ROUTE_LANES), lambda bi, i: (0, 0, 0)),
                  pl.BlockSpec((1, ROUTE_LANES), const2),
                  pl.BlockSpec((1, ROUTE_LANES), const2)],
        out_specs=[pl.BlockSpec((tm, d), lambda bi, i: (bi * nt + i, 0)),
                   pl.BlockSpec((tm, ROUTE_LANES), lambda bi, i: (bi * nt + i, 0)),
                   pl.BlockSpec((ROUTE_ROWS, tm), lambda bi, i: (0, bi * nt + i)),
                   pl.BlockSpec((1, ROUTE_LANES), const2)],
        out_shape=[jax.ShapeDtypeStruct((b * t, d), F32),
                   jax.ShapeDtypeStruct((b * t, ROUTE_LANES), F32),
                   jax.ShapeDtypeStruct((ROUTE_ROWS, b * t), F32),
                   jax.ShapeDtypeStruct((1, ROUTE_LANES), F32)],
        scratch_shapes=[pltpu.VMEM((1, ROUTE_LANES), F32)],
        compiler_params=_cparams(("arbitrary", "arbitrary")),
        name="moe_route",
    )(x, g, mod, mod, wr_pad, br_pad, counts_in)


def _row_gather_copy(src_hbm, tok, dst, r, sem):
    return pltpu.make_async_copy(src_hbm.at[pl.ds(tok, 1), :], dst.at[pl.ds(r, 1), :], sem)


def _moe_kernel(te_ref, tr_ref, src_ref, h_hbm, wg_ref, wu_ref, wd_ref, o_hbm,
                xb, acc, stage, wgu, wdb, gsem, osem):
    s = pl.program_id(0)
    j = pl.program_id(1)
    ns = pl.num_programs(0)
    nj = pl.num_programs(1)
    rows = tr_ref[s]
    nsub = (rows + MOE_SUB - 1) // MOE_SUB
    tile_rows = xb.shape[0]

    def out_copy():
        return pltpu.make_async_copy(acc, o_hbm.at[pl.ds(pl.multiple_of(s * tile_rows, MOE_SUB), tile_rows), :], osem)

    @pl.when(j == 0)
    def _():
        @pl.when(rows > 0)
        def _gather():
            def issue(sub, slot):
                base = s * tile_rows + sub * MOE_SUB

                def body(r, carry):
                    _row_gather_copy(h_hbm, src_ref[base + r], stage.at[slot], r, gsem.at[slot]).start()
                    return carry
                lax.fori_loop(0, MOE_SUB, body, 0, unroll=GATHER_UNROLL)

            issue(0, 0)

            def sub_body(sub, carry):
                slot = sub % 2

                @pl.when(sub + 1 < nsub)
                def _():
                    issue(sub + 1, 1 - slot)
                pltpu.make_async_copy(h_hbm.at[pl.ds(0, MOE_SUB), :], stage.at[slot], gsem.at[slot]).wait()
                r0 = pl.multiple_of(sub * MOE_SUB, MOE_SUB)
                xb[pl.ds(r0, MOE_SUB), :] = stage[slot].astype(BF16)
                return carry
            lax.fori_loop(0, nsub, sub_body, 0)

        @pl.when(s > 0)
        def _():
            out_copy().wait()
        acc[...] = jnp.zeros_like(acc)

    @pl.when(rows > 0)
    def _():
        tf = wg_ref.shape[1]
        wgu[:, :tf] = wg_ref[...].astype(BF16)
        wgu[:, tf:] = wu_ref[...].astype(BF16)
        wdb[...] = wd_ref[...].astype(BF16)

        def block(r0, m):
            gu = _dot(xb[pl.ds(r0, m), :], wgu[...])
            gate, up = gu[:, :tf], gu[:, tf:]
            a = (gate * jax.nn.sigmoid(gate) * up).astype(BF16)
            acc[pl.ds(r0, m), :] += _dot(a, wdb[...])

        per_big = MOE_BLK_BIG // MOE_SUB
        per_blk = MOE_BLK // MOE_SUB
        nbig = nsub // per_big
        rem_big = nsub - nbig * per_big
        has_blk = rem_big >= per_blk
        rem = rem_big - jnp.where(has_blk, per_blk, 0)
        r_blk = pl.multiple_of(nbig * MOE_BLK_BIG, MOE_BLK)
        r_tail = pl.multiple_of(r_blk + jnp.where(has_blk, MOE_BLK, 0), MOE_SUB)

        def body(blk, carry):
            block(pl.multiple_of(blk * MOE_BLK_BIG, MOE_BLK_BIG), MOE_BLK_BIG)
            return carry
        lax.fori_loop(0, nbig, body, 0)
        pl.when(has_blk)(lambda: block(r_blk, MOE_BLK))
        for k in range(1, per_blk):
            @pl.when(rem == k)
            def _(k=k):
                block(r_tail, k * MOE_SUB)

    @pl.when(j == nj - 1)
    def _():
        out_copy().start()

        @pl.when(s == ns - 1)
        def _():
            out_copy().wait()


def _moe_experts(h_all, te, tr, src, wg, wu, wd, n_tiles):
    n, d = h_all.shape
    e, _, ff = wg.shape
    tf = min(MOE_FF_TILE, ff)
    assert ff % tf == 0 and MOE_TILE_ROWS % MOE_BLK == 0 and MOE_BLK % MOE_SUB == 0 and MOE_BLK_BIG == 2 * MOE_BLK
    nj = ff // tf
    r = MOE_TILE_ROWS

    def jj(s, j, tr_ref):
        return jnp.where(tr_ref[s] > 0, j, nj - 1)

    grid_spec = pltpu.PrefetchScalarGridSpec(
        num_scalar_prefetch=3,
        grid=(n_tiles, nj),
        in_specs=[pl.BlockSpec(memory_space=pl.ANY),
                  pl.BlockSpec((None, d, tf), lambda s, j, te_r, tr_r, src_r: (te_r[s], 0, jj(s, j, tr_r))),
                  pl.BlockSpec((None, d, tf), lambda s, j, te_r, tr_r, src_r: (te_r[s], 0, jj(s, j, tr_r))),
                  pl.BlockSpec((None, tf, d), lambda s, j, te_r, tr_r, src_r: (te_r[s], jj(s, j, tr_r), 0))],
        out_specs=pl.BlockSpec(memory_space=pl.ANY),
        scratch_shapes=[pltpu.VMEM((r, d), BF16), pltpu.VMEM((r, d), F32), pltpu.VMEM((2, MOE_SUB, d), F32),
                        pltpu.VMEM((d, 2 * tf), BF16), pltpu.VMEM((tf, d), BF16),
                        pltpu.SemaphoreType.DMA((2,)), pltpu.SemaphoreType.DMA(())],
    )
    return pl.pallas_call(
        _moe_kernel,
        grid_spec=grid_spec,
        out_shape=jax.ShapeDtypeStruct((n_tiles * r, d), F32),
        compiler_params=_cparams(("arbitrary", "arbitrary")),
        name="moe_experts",
    )(te, tr, src, h_all, wg, wu, wd)


def _combine_kernel(s1_ref, s2_ref, x_ref, g2_ref, r_ref, gf_ref, e_hbm, o_ref, st1, st2, sem):
    nb, tt, d = x_ref.shape
    tm = nb * tt
    step = pl.program_id(0) * pl.num_programs(1) + pl.program_id(1)
    nsteps = pl.num_programs(0) * pl.num_programs(1)
    slot = step % 2

    def issue(tile, sl):
        base = tile * tm

        def body(r, carry):
            _row_gather_copy(e_hbm, s1_ref[base + r], st1.at[sl], r, sem.at[0, sl]).start()
            _row_gather_copy(e_hbm, s2_ref[base + r], st2.at[sl], r, sem.at[1, sl]).start()
            return carry
        lax.fori_loop(0, tm, body, 0, unroll=GATHER_UNROLL)

    pl.when(step == 0)(lambda: issue(0, 0))
    pl.when(step + 1 < nsteps)(lambda: issue(step + 1, 1 - slot))
    pltpu.make_async_copy(e_hbm.at[pl.ds(0, tm), :], st1.at[slot], sem.at[0, slot]).wait()
    pltpu.make_async_copy(e_hbm.at[pl.ds(0, tm), :], st2.at[slot], sem.at[1, slot]).wait()

    f = r_ref[:, 2:3] * st1[slot] + r_ref[:, 3:4] * st2[slot]
    x2 = x_ref[...] + g2_ref[...] * f.reshape(nb, tt, d)
    rs = lax.rsqrt(jnp.mean(x2 * x2, axis=-1, keepdims=True) + EPS)
    o_ref[...] = (x2 * rs) * gf_ref[...]


def _combine(x, mod, l, route, slot1, slot2, e_out, gf):
    b, t, d = x.shape
    nb, tt = _tile(b, t, ROWS_COMBINE)
    tm = nb * tt
    nt = t // tt
    grid_spec = pltpu.PrefetchScalarGridSpec(
        num_scalar_prefetch=2,
        grid=(b // nb, nt),
        in_specs=[pl.BlockSpec((nb, tt, d), lambda bi, i, a, c: (bi, i, 0)),
                  _mod_spec(nb, d, 5, l),
                  pl.BlockSpec((tm, ROUTE_LANES), lambda bi, i, a, c: (bi * nt + i, 0)),
                  pl.BlockSpec((1, d), lambda bi, i, a, c: (0, 0)),
                  pl.BlockSpec(memory_space=pl.ANY)],
        out_specs=pl.BlockSpec((nb, tt, d), lambda bi, i, a, c: (bi, i, 0)),
        scratch_shapes=[pltpu.VMEM((2, tm, d), F32), pltpu.VMEM((2, tm, d), F32), pltpu.SemaphoreType.DMA((2, 2))],
    )
    return pl.pallas_call(
        _combine_kernel,
        grid_spec=grid_spec,
        out_shape=jax.ShapeDtypeStruct((b, t, d), F32),
        compiler_params=_cparams(("arbitrary", "arbitrary")),
        name="moe_combine_norm",
    )(slot1, slot2, x, mod, route, gf, e_out)


def _lookup(table, idx):
    out = jnp.zeros_like(idx)
    for e in range(N_EXPERTS):
        out = jnp.where(idx == e, table[e], out)
    return out


def _floor_div(a, b):
    q = jnp.floor(a.astype(F32) / b.astype(F32)).astype(I32)
    q = jnp.where((q + 1) * b <= a, q + 1, q)
    return jnp.where(q * b > a, q - 1, q)


def _moe_layout(route_t, counts_f):
    n = route_t.shape[1]
    r, sub = MOE_TILE_ROWS, MOE_SUB
    n_tiles = -(-TOP_K * n // r) + N_EXPERTS
    i1 = route_t[0].astype(I32)
    i2 = route_t[1].astype(I32)
    r1 = route_t[4].astype(I32)
    r2 = route_t[5].astype(I32)
    counts = counts_f[0, :N_EXPERTS].astype(I32)
    nt = (counts + r - 1) // r
    ntc = jnp.maximum(nt, 1)
    sz = jnp.maximum(((_floor_div(counts + ntc - 1, ntc) + sub - 1) // sub) * sub, sub)
    cum = jnp.cumsum(nt)
    tstart = cum - nt
    total = cum[-1]
    s_idx = jnp.arange(n_tiles, dtype=I32)
    te_raw = jnp.minimum(jnp.sum((s_idx[:, None] >= cum[None, :]).astype(I32), axis=1), N_EXPERTS - 1)
    used = s_idx < total
    last_e = jnp.max(jnp.where(used, te_raw, 0))
    te = jnp.where(used, te_raw, last_e)
    sz_t = _lookup(sz, te)
    k_in = s_idx - _lookup(tstart, te)
    tr = jnp.where(used, jnp.clip(_lookup(counts, te) - k_in * sz_t, 0, sz_t), 0).astype(I32)

    def slot(ei, ri):
        sz_i = _lookup(sz, ei)
        k = _floor_div(ri, sz_i)
        return (_lookup(tstart, ei) + k) * r + (ri - k * sz_i)
    slot1, slot2 = slot(i1, r1), slot(i2, r2)
    tok = jnp.arange(n, dtype=I32)
    src = jnp.zeros((n_tiles * r,), I32).at[jnp.concatenate([slot1, slot2])].set(jnp.concatenate([tok, tok]))
    return te, tr, src, slot1, slot2, n_tiles


def kernel(x_prompt, x_sample, c_prompt, c_sample, state_pool, state_ssm_re, state_ssm_im, w_ada, b_ada, norm_mix, norm_ffn, w_in, w_pool, pool_scale, ssm_lam_re, ssm_lam_im, ssm_log_step, ssm_b_re, ssm_b_im, ssm_c_re, ssm_c_im, ssm_d, ssm_w_glu, w_out, ffn_w_gate, ffn_w_up, ffn_w_down, moe_w_router, moe_b_router, moe_w_gate, moe_w_up, moe_w_down, norm_final):
    bp, tp, d = x_prompt.shape
    bs, ts, _ = x_sample.shape
    depth = w_ada.shape[0]
    c_pool = w_pool.shape[1] * w_pool.shape[2]
    assert tp % CHUNK == 0 and ts <= CHUNK

    m_rows = -(-(bp + bs) // SUBLANES_V7X) * SUBLANES_V7X
    c_all = jnp.concatenate([c_prompt, c_sample, jnp.zeros((m_rows - bp - bs, d), F32)], axis=0)
    mod_all = _ada(c_all, w_ada, b_ada)

    mod_p = mod_all[:, :bp].reshape(depth, bp, 1, 6 * d)
    mod_s = mod_all[:, bp:bp + bs].reshape(depth, bs, 1, 6 * d)
    w_in_b, w_out_b = w_in.astype(BF16), w_out.astype(BF16)
    wp_b, wglu_b = w_pool.astype(BF16), ssm_w_glu.astype(BF16)
    ps = pool_scale.reshape(depth, 1, c_pool).astype(F32)
    dsk = ssm_d.reshape(depth, 1, d - c_pool).astype(F32)
    k_t, b_t, c_t, p16, q16, p8b, q8b = jax.vmap(lambda *a: _s5_params(*a, ts, bs))(
        ssm_lam_re, ssm_lam_im, ssm_log_step, ssm_b_re, ssm_b_im, ssm_c_re, ssm_c_im)
    x0_t = jnp.concatenate([state_ssm_re, state_ssm_im], axis=-1).astype(F32).transpose(0, 2, 3, 1)
    st_p = jnp.zeros((1, bp, POOL_BUF, c_pool), F32)
    st_s = state_pool.astype(F32)

    xp, xs = x_prompt, x_sample
    pool_p, re_p, im_p, pool_s, re_s, im_s = [], [], [], [], [], []
    y_p = y_s = None
    for l in range(depth):
        gm = norm_mix[l].reshape(1, d)
        gn = norm_ffn[l].reshape(1, d)

        up_all = _proj_in(xp, mod_p, l, gm, w_in_b)
        us_all = _proj_in(xs, mod_s, l, gm, w_in_b)
        y_ssm_p, xend = _s5_prompt(up_all, l, k_t, b_t, c_t, p16, q16, c_pool)
        ys2, x1_t = _s5_sample(us_all.reshape(bs * ts, d), l, x0_t, k_t, b_t, c_t, p8b, q8b, ts, c_pool)
        y_ssm_s = ys2.reshape(bs, ts, d - c_pool)
        ffn_f32 = (ffn_w_gate[l // 2], ffn_w_up[l // 2], ffn_w_down[l // 2]) if l % 2 == 0 else ()
        xp, new_pool_p, ffn_bf16 = _mix(xp, up_all, y_ssm_p, st_p, 0, mod_p, l, wp_b, ps, dsk, wglu_b, w_out_b, 0,
                                        cast=ffn_f32)
        xs, new_pool_s, _ = _mix(xs, us_all, y_ssm_s, st_s, l, mod_s, l, wp_b, ps, dsk, wglu_b, w_out_b, PAST_LEN)
        pool_p.append(new_pool_p)
        pool_s.append(new_pool_s)
        re_p.append(xend[..., :SSM_P])
        im_p.append(xend[..., SSM_P:])
        re_s.append(x1_t[:, :SSM_P, :].transpose(2, 0, 1))
        im_s.append(x1_t[:, SSM_P:, :].transpose(2, 0, 1))

        i = l // 2
        if l % 2 == 0:
            wg, wu, wd = ffn_bf16
            xp = _ffn(xp, mod_p, l, gn, wg, wu, wd)
            xs = _ffn(xs, mod_s, l, gn, wg, wu, wd)
        else:
            ne = moe_w_router.shape[-1]
            wr32 = jnp.pad(moe_w_router[i].astype(F32), ((0, 0), (0, ROUTE_LANES - ne)))
            wr_hi = wr32.astype(BF16)
            wr = jnp.stack([wr_hi, (wr32 - wr_hi.astype(F32)).astype(BF16)])
            br = jnp.pad(moe_b_router[i].astype(F32), (0, ROUTE_LANES - ne)).reshape(1, ROUTE_LANES)
            zero_counts = jnp.zeros((1, ROUTE_LANES), F32)
            h_p, route_p, rt_p, cnt_p = _route(xp, mod_p, l, gn, wr, br, zero_counts)
            h_s, route_s, rt_s, cnt_s = _route(xs, mod_s, l, gn, wr, br, cnt_p)
            h_all = jnp.concatenate([h_p, h_s], axis=0)
            te, tr, src, slot1, slot2, n_tiles = _moe_layout(jnp.concatenate([rt_p, rt_s], axis=1), cnt_s)
            e_out = _moe_experts(h_all, te, tr, src, moe_w_gate[i], moe_w_up[i], moe_w_down[i], n_tiles)
            n_p = bp * tp
            last = l == depth - 1
            assert last, "the combine kernel also applies the final norm"
            gf = norm_final.reshape(1, d)
            y_p = _combine(xp, mod_p, l, route_p, slot1[:n_p], slot2[:n_p], e_out, gf)
            y_s = _combine(xs, mod_s, l, route_s, slot1[n_p:], slot2[n_p:], e_out, gf)

    return (y_p, y_s, jnp.stack(pool_p), jnp.stack(re_p), jnp.stack(im_p),
            jnp.stack(pool_s), jnp.stack(re_s), jnp.stack(im_s))
```

```python
import functools
import math

import jax
import jax.numpy as jnp
import numpy as np
from jax import lax
from jax.experimental import pallas as pl
from jax.experimental.pallas import tpu as pltpu

F32 = jnp.float32
BF16 = jnp.bfloat16
I32 = jnp.int32

EPS = 1e-6
POOL_WINDOWS = (2, 4, 8, 16)
POOL_BUF = max(POOL_WINDOWS) - 1
SUBLANES_V7X = 8
BF16_SUBLANES_V7X = 16
LANES_V7X = 128
HALO = -(-POOL_BUF // SUBLANES_V7X) * SUBLANES_V7X
SSM_H = 16
SSM_P = 64
CHUNK = 16
CW = CHUNK * SSM_H
SW = 2 * SSM_P
N_EXPERTS = 8
TOP_K = 2
PAST_LEN = 16384

VMEM_LIMIT_V7X = 56 * 1024 * 1024
ROWS_IN = 512
ROWS_MIX = 256
ROWS_FFN = 512
FF_TILE = 512
FFN_NORM_ROWS = 128
ROWS_ROUTE = 512
MOE_TILE_ROWS = 2560
MOE_SUB = 128
MOE_BLK_BIG = 1024
MOE_BLK = 512
GATHER_UNROLL = 8
MOE_FF_TILE = 256
ROWS_COMBINE = 256
ADA_TN = 1024


def _cparams(sem):
    return pltpu.CompilerParams(dimension_semantics=sem, vmem_limit_bytes=VMEM_LIMIT_V7X)


def _tile(b, t, rows):
    if t >= rows:
        assert t % rows == 0
        return 1, rows
    nb = max(1, min(b, rows // t))
    assert b % nb == 0
    return nb, t


def _modnorm(x, g, s, sh):
    r = lax.rsqrt(jnp.mean(x * x, axis=-1, keepdims=True) + EPS)
    return (x * r) * g * (1.0 + s) + sh


def _dot(a, b):
    return jnp.dot(a, b, preferred_element_type=F32)


def _mod_spec(nb, d, k, l):
    return pl.BlockSpec((None, nb, 1, d), lambda b, i, *_: (l, b, 0, k))


def _layer_spec(shape, l):
    return pl.BlockSpec((None,) + tuple(shape), lambda *_: (l,) + (0,) * len(shape))


def _ada_kernel(c_ref, w_ref, b_ref, o_ref):
    c = c_ref[...]
    sc = c * jax.nn.sigmoid(c)
    o_ref[0] = _dot(sc.astype(BF16), w_ref[0].astype(BF16)) + b_ref[0]


def _ada(c_all, w_ada, b_ada):
    m, d = c_all.shape
    depth, _, n6 = w_ada.shape
    tn = math.gcd(ADA_TN, n6)
    return pl.pallas_call(
        _ada_kernel,
        grid=(depth, n6 // tn),
        in_specs=[pl.BlockSpec((m, d), lambda l, n: (0, 0)),
                  pl.BlockSpec((1, d, tn), lambda l, n: (l, 0, n)),
                  pl.BlockSpec((1, 1, tn), lambda l, n: (l, 0, n))],
        out_specs=pl.BlockSpec((1, m, tn), lambda l, n: (l, 0, n)),
        out_shape=jax.ShapeDtypeStruct((depth, m, n6), F32),
        compiler_params=_cparams(("arbitrary", "arbitrary")),
        name="adaln_mod",
    )(c_all, w_ada, b_ada.reshape(depth, 1, n6))


def _in_kernel(x_ref, g_ref, s_ref, sh_ref, w_ref, u_ref):
    nb, tt, d = x_ref.shape
    h = _modnorm(x_ref[...], g_ref[...], s_ref[...], sh_ref[...])
    u = _dot(h.reshape(nb * tt, d).astype(BF16), w_ref[...])
    u_ref[...] = u.reshape(nb, tt, u_ref.shape[-1])


def _proj_in(x, mod, l, g, w_bf16):
    b, t, d = x.shape
    n = w_bf16.shape[-1]
    nb, tt = _tile(b, t, ROWS_IN)
    return pl.pallas_call(
        _in_kernel,
        grid=(b // nb, t // tt),
        in_specs=[pl.BlockSpec((nb, tt, d), lambda bi, i: (bi, i, 0)),
                  pl.BlockSpec((1, d), lambda bi, i: (0, 0)),
                  _mod_spec(nb, d, 1, l), _mod_spec(nb, d, 0, l),
                  _layer_spec((d, n), l)],
        out_specs=pl.BlockSpec((nb, tt, n), lambda bi, i: (bi, i, 0)),
        out_shape=jax.ShapeDtypeStruct((b, t, n), F32),
        compiler_params=_cparams(("arbitrary", "arbitrary")),
        name="norm_proj_in",
    )(x, g, mod, mod, w_bf16)


GROUPS_PER_LANE_TILE = LANES_V7X // SSM_H


def _toeplitz_placement():
    toe = np.zeros((CHUNK, CW, CW), np.float32)
    for s in range(CHUNK):
        for h in range(SSM_H):
            for t in range(s, CHUNK):
                toe[t, (t - s) * SSM_H + h, s * SSM_H + h] = 1.0
    return jnp.asarray(toe)


def _s5_params(lam_re, lam_im, log_step, b_re, b_im, c_re, c_im, ts, bs):
    hp = lax.Precision.HIGHEST
    toe = _toeplitz_placement()
    lr, li = lam_re.astype(F32), lam_im.astype(F32)
    step = jnp.exp(log_step.astype(F32))[:, None]
    mag = jnp.exp(lr * step)
    a_re = mag * jnp.cos(li * step)
    a_im = mag * jnp.sin(li * step)
    den = lr * lr + li * li
    nr = a_re - 1.0
    f_re = (nr * lr + a_im * li) / den
    f_im = (a_im * lr - nr * li) / den
    br, bi = b_re.astype(F32), b_im.astype(F32)
    bb_re = f_re[..., None] * br - f_im[..., None] * bi
    bb_im = f_re[..., None] * bi + f_im[..., None] * br
    cr, ci = c_re.astype(F32), c_im.astype(F32)
    g = lr.shape[0]

    pw_re, pw_im = [jnp.ones_like(a_re)], [jnp.zeros_like(a_im)]
    for _ in range(CHUNK):
        pr, pi = pw_re[-1], pw_im[-1]
        pw_re.append(pr * a_re - pi * a_im)
        pw_im.append(pr * a_im + pi * a_re)
    pw_re, pw_im = jnp.stack(pw_re), jnp.stack(pw_im)

    bbt_re, bbt_im = bb_re.transpose(0, 2, 1), bb_im.transpose(0, 2, 1)
    w_re = pw_re[:CHUNK, :, None, :] * bbt_re - pw_im[:CHUNK, :, None, :] * bbt_im
    w_im = pw_re[:CHUNK, :, None, :] * bbt_im + pw_im[:CHUNK, :, None, :] * bbt_re
    m = (jnp.einsum('gop,lgip->goli', cr, w_re, precision=hp)
         - jnp.einsum('gop,lgip->goli', ci, w_im, precision=hp)).reshape(g, SSM_H, CW)
    k_t = jnp.einsum('gok,tkn->gton', m.astype(BF16), toe.astype(BF16),
                     preferred_element_type=F32).reshape(g, CW, CW)

    rev_re, rev_im = pw_re[:CHUNK][::-1], pw_im[:CHUNK][::-1]
    ar = jnp.repeat(rev_re.transpose(1, 2, 0), SSM_H, axis=-1)
    ai = jnp.repeat(rev_im.transpose(1, 2, 0), SSM_H, axis=-1)
    xr = jnp.tile(bb_re, (1, 1, CHUNK))
    xi = jnp.tile(bb_im, (1, 1, CHUNK))
    bt_re = ar * xr - ai * xi
    bt_im = ar * xi + ai * xr
    b_t = jnp.concatenate([bt_re, bt_im, bt_im, bt_re], axis=1)

    qr = pw_re[1:].transpose(1, 0, 2)[:, :, None, :]
    qi = pw_im[1:].transpose(1, 0, 2)[:, :, None, :]
    cm_re = cr[:, None] * qr - ci[:, None] * qi
    cm_im = -(cr[:, None] * qi + ci[:, None] * qr)
    c_t = jnp.concatenate([cm_re, cm_im], axis=-1).reshape(g, CW, SW)

    def coef(k):
        pr, pi = pw_re[k], pw_im[k]
        return (jnp.concatenate([pr, pr, pr, pr], axis=-1),
                jnp.concatenate([-pi, pi, pi, -pi], axis=-1))
    p16, q16 = coef(CHUNK)
    p8, q8 = coef(ts)
    p8b = jnp.broadcast_to(p8[:, :SW, None], (g, SW, bs))
    q8b = jnp.broadcast_to(q8[:, :SW, None], (g, SW, bs))
    return k_t.astype(BF16), b_t.astype(BF16), c_t.astype(BF16), p16, q16, p8b, q8b


def _s5_prompt_kernel(us_ref, kt_ref, bt_ref, ct_ref, p_ref, q_ref, y_ref, xe_ref, ut, yt, scan_x, scan_xs, s_scr):
    gt = ut.shape[0]
    nchunk = ut.shape[-1]
    for t in range(CHUNK):
        slab = us_ref[0, pl.ds(t, nchunk, stride=CHUNK), :]
        ut[:, t] = slab.T.reshape(gt, SSM_H, nchunk).astype(BF16)
    for gi in range(gt):
        buz = _dot(bt_ref[gi], ut[gi].reshape(CW, nchunk))
        buz_t = buz.T
        scan_x[pl.ds(gi, nchunk, stride=gt), :] = buz_t[:, :SW]
        scan_xs[pl.ds(gi, nchunk, stride=gt), :] = buz_t[:, SW:]
    px, pxs = p_ref[:, :SW], p_ref[:, SW:]
    qx, qxs = q_ref[:, :SW], q_ref[:, SW:]
    x = jnp.zeros((gt, SW), F32)
    xs = jnp.zeros((gt, SW), F32)
    for c in range(nchunk):
        lo, hi = c * gt, (c + 1) * gt
        s_scr[lo:hi, :] = x
        x, xs = (px * x + qx * xs + scan_x[lo:hi, :],
                 pxs * xs + qxs * x + scan_xs[lo:hi, :])
    xe_ref[0] = x
    for gi in range(gt):
        st = s_scr[pl.ds(gi, nchunk, stride=gt), :].T.astype(BF16)
        y_t = _dot(kt_ref[gi], ut[gi].reshape(CW, nchunk)) + _dot(ct_ref[gi], st)
        yt[gi] = y_t.reshape(CHUNK, SSM_H, nchunk)
    for t in range(CHUNK):
        y_ref[0, pl.ds(t, nchunk, stride=CHUNK), :] = yt[:, t].reshape(gt * SSM_H, nchunk).T


def _s5_prompt(u, l, k_t, b_t, c_t, p16, q16, c_off):
    b, t, _ = u.shape
    g = k_t.shape[1]
    gt = GROUPS_PER_LANE_TILE
    lt = gt * SSM_H
    nchunk = t // CHUNK
    off = c_off // lt

    def wspec(r, c):
        return pl.BlockSpec((None, gt, r, c), lambda bi, j: (l, j, 0, 0))
    return pl.pallas_call(
        _s5_prompt_kernel,
        grid=(b, g // gt),
        in_specs=[pl.BlockSpec((1, t, lt), lambda bi, j: (bi, 0, off + j)),
                  wspec(CW, CW), wspec(2 * SW, CW), wspec(CW, SW),
                  pl.BlockSpec((None, gt, 2 * SW), lambda bi, j: (l, j, 0)),
                  pl.BlockSpec((None, gt, 2 * SW), lambda bi, j: (l, j, 0))],
        out_specs=[pl.BlockSpec((1, t, lt), lambda bi, j: (bi, 0, j)),
                   pl.BlockSpec((1, gt, SW), lambda bi, j: (bi, j, 0))],
        out_shape=[jax.ShapeDtypeStruct((b, t, g * SSM_H), F32), jax.ShapeDtypeStruct((b, g, SW), F32)],
        scratch_shapes=[pltpu.VMEM((gt, CHUNK, SSM_H, nchunk), BF16), pltpu.VMEM((gt, CHUNK, SSM_H, nchunk), F32),
                        pltpu.VMEM((nchunk * gt, SW), F32), pltpu.VMEM((nchunk * gt, SW), F32),
                        pltpu.VMEM((nchunk * gt, SW), F32)],
        compiler_params=_cparams(("arbitrary", "arbitrary")),
        name="s5_prompt",
    )(u, k_t, b_t, c_t, p16, q16)


def _s5_sample_kernel(us_ref, x0_ref, kt_ref, bt_ref, ct_ref, p_ref, q_ref, y_ref, x1_ref, ut, yt, *, ts):
    gt = ut.shape[0]
    bs = ut.shape[-1]
    w = ts * SSM_H
    for t in range(ts):
        slab = us_ref[pl.ds(t, bs, stride=ts), :]
        ut[:, t] = slab.T.reshape(gt, SSM_H, bs).astype(BF16)
    for gi in range(gt):
        ug = ut[gi].reshape(w, bs)
        x0 = x0_ref[gi]
        x0s = jnp.concatenate([x0[SSM_P:], x0[:SSM_P]], axis=0)
        y_t = _dot(kt_ref[gi, :w, :w], ug) + _dot(ct_ref[gi, :w, :], x0.astype(BF16))
        x1_ref[gi] = p_ref[gi] * x0 + q_ref[gi] * x0s + _dot(bt_ref[gi, :SW, CW - w:], ug)
        yt[gi] = y_t.reshape(ts, SSM_H, bs)
    for t in range(ts):
        y_ref[pl.ds(t, bs, stride=ts), :] = yt[:, t].reshape(gt * SSM_H, bs).T


def _s5_sample(u2, l, x0_t, k_t, b_t, c_t, p8b, q8b, ts, c_off):
    rows, _ = u2.shape
    _, g, _, bs = x0_t.shape
    gt = GROUPS_PER_LANE_TILE
    lt = gt * SSM_H
    off = c_off // lt

    def wspec(r, c):
        return pl.BlockSpec((None, gt, r, c), lambda j: (l, j, 0, 0))
    return pl.pallas_call(
        functools.partial(_s5_sample_kernel, ts=ts),
        grid=(g // gt,),
        in_specs=[pl.BlockSpec((rows, lt), lambda j: (0, off + j)),
                  wspec(SW, bs), wspec(CW, CW), wspec(2 * SW, CW), wspec(CW, SW), wspec(SW, bs), wspec(SW, bs)],
        out_specs=[pl.BlockSpec((rows, lt), lambda j: (0, j)), pl.BlockSpec((gt, SW, bs), lambda j: (j, 0, 0))],
        out_shape=[jax.ShapeDtypeStruct((rows, g * SSM_H), F32), jax.ShapeDtypeStruct((g, SW, bs), F32)],
        scratch_shapes=[pltpu.VMEM((gt, ts, SSM_H, bs), BF16), pltpu.VMEM((gt, ts, SSM_H, bs), F32)],
        compiler_params=_cparams(("arbitrary",)),
        name="s5_sample",
    )(u2, x0_t, k_t, b_t, c_t, p8b, q8b)


def _mix_kernel(x_ref, up_ref, us_ref, ys_ref, st_ref, g1_ref, wp_ref, ps_ref, dsk_ref, wglu_ref,
                wout_ref, *rest, start_pos, carry, ncast):
    cast_in, (o_ref, np_ref), cast_out, ext = rest[:ncast], rest[ncast:ncast + 2], rest[ncast + 2:-1], rest[-1]
    for src, dst in zip(cast_in, cast_out):
        dst[...] = src[...].astype(BF16)
    nb, tt, d = x_ref.shape
    c = up_ref.shape[-1]
    pg = c // len(POOL_WINDOWS)
    rows = nb * tt
    i = pl.program_id(1)

    @pl.when(i == 0)
    def _():
        ext[:, HALO - POOL_BUF:HALO, :] = st_ref[...]
    ext[:, HALO:HALO + tt, :] = up_ref[...]
    np_ref[...] = ext[:, HALO + tt - POOL_BUF:HALO + tt, :]

    pos = start_pos + i * tt + lax.broadcasted_iota(I32, (1, tt, 1), 1)
    outs = []
    for gidx, w in enumerate(POOL_WINDOWS):
        c0 = gidx * pg
        cur = ext[:, HALO:HALO + tt, c0:c0 + pg]
        s = cur
        for k in range(1, w):
            s = s + ext[:, HALO - k:HALO - k + tt, c0:c0 + pg]
        cnt = jnp.minimum(pos + 1, w).astype(F32)
        dd = s / cnt - cur
        yg = _dot(dd.reshape(rows, pg).astype(BF16), wp_ref[gidx])
        outs.append(yg * ps_ref[:, c0:c0 + pg])
    y_pool = jnp.concatenate(outs, axis=-1)

    if carry:
        ext[:, 0:HALO, :] = ext[:, tt:tt + HALO, :]

    us = us_ref[...].reshape(rows, c)
    yf = ys_ref[...].reshape(rows, c) + dsk_ref[...] * us
    gl = jax.nn.gelu(yf, approximate=True)
    o = gl * jax.nn.sigmoid(_dot(gl.astype(BF16), wglu_ref[...]))
    mo = _dot(y_pool.astype(BF16), wout_ref[:c, :]) + _dot(o.astype(BF16), wout_ref[c:, :])
    o_ref[...] = x_ref[...] + g1_ref[...] * mo.reshape(nb, tt, d)


def _mix(x, u, y_ssm, pool_state, l_state, mod, l, wp, ps, dsk, wglu, wout, start_pos, cast=()):
    b, t, d = x.shape
    c = y_ssm.shape[-1]
    nb, tt = _tile(b, t, ROWS_MIX)
    nt = t // tt
    assert nt == 1 or tt >= HALO
    nsteps = (b // nb) * nt
    cast_specs = []
    for a in cast:
        blk = -(-(-(-a.shape[0] // nsteps)) // BF16_SUBLANES_V7X) * BF16_SUBLANES_V7X
        last = -(-a.shape[0] // blk) - 1
        cast_specs.append(pl.BlockSpec((blk, a.shape[1]), lambda bi, i, last=last: (jnp.minimum(bi * nt + i, last), 0)))
    outs = pl.pallas_call(
        functools.partial(_mix_kernel, start_pos=start_pos, carry=nt > 1, ncast=len(cast)),
        grid=(b // nb, nt),
        in_specs=[pl.BlockSpec((nb, tt, d), lambda bi, i: (bi, i, 0)),
                  pl.BlockSpec((nb, tt, c), lambda bi, i: (bi, i, 0)),
                  pl.BlockSpec((nb, tt, c), lambda bi, i: (bi, i, 1)),
                  pl.BlockSpec((nb, tt, c), lambda bi, i: (bi, i, 0)),
                  pl.BlockSpec((None, nb, POOL_BUF, c), lambda bi, i: (l_state, bi, 0, 0)),
                  _mod_spec(nb, d, 2, l),
                  _layer_spec(wp.shape[1:], l), _layer_spec((1, c), l), _layer_spec((1, c), l),
                  _layer_spec((c, c), l), _layer_spec((d, d), l)] + cast_specs,
        out_specs=[pl.BlockSpec((nb, tt, d), lambda bi, i: (bi, i, 0)),
                   pl.BlockSpec((nb, POOL_BUF, c), lambda bi, i: (bi, 0, 0))] + cast_specs,
        out_shape=[jax.ShapeDtypeStruct((b, t, d), F32), jax.ShapeDtypeStruct((b, POOL_BUF, c), F32)]
        + [jax.ShapeDtypeStruct(a.shape, BF16) for a in cast],
        scratch_shapes=[pltpu.VMEM((nb, HALO + tt, c), F32)],
        compiler_params=_cparams(("arbitrary", "arbitrary")),
        name="pool_glu_proj_out",
    )(x, u, u, y_ssm, pool_state, mod, wp, ps, dsk, wglu, wout, *cast)
    return outs[0], outs[1], tuple(outs[2:])


def _ffn_kernel(x_ref, xn_ref, g_ref, s_ref, sh_ref, sn_ref, shn_ref, g2_ref, wg_ref, wu_ref, wd_ref, o_ref,
                hb, acc, *, last_valid):
    nb, tt, d = x_ref.shape
    tf = wg_ref.shape[1]
    j = pl.program_id(2)
    nj = pl.num_programs(2)
    tile = pl.program_id(0) * pl.num_programs(1) + pl.program_id(1)
    slot = tile % 2

    def normalise_into(dst, xr, sr, shr):
        if nb == 1:
            for r0 in range(0, tt, FFN_NORM_ROWS):
                h = _modnorm(xr[:, r0:r0 + FFN_NORM_ROWS, :], g_ref[...], sr[...], shr[...])
                hb[dst, r0:r0 + FFN_NORM_ROWS, :] = h.reshape(FFN_NORM_ROWS, d).astype(BF16)
        else:
            cb = max(1, FFN_NORM_ROWS // tt)
            for b0 in range(0, nb, cb):
                h = _modnorm(xr[b0:b0 + cb], g_ref[...], sr[b0:b0 + cb], shr[b0:b0 + cb])
                hb[dst, b0 * tt:(b0 + cb) * tt, :] = h.reshape(cb * tt, d).astype(BF16)

    @pl.when(jnp.logical_and(tile == 0, j == 0))
    def _():
        normalise_into(0, x_ref, s_ref, sh_ref)

    @pl.when(j == 0)
    def _():
        acc[...] = jnp.zeros_like(acc)

    def step(valid, last):
        h = hb[slot]
        gate = _dot(h, wg_ref[...])
        up = _dot(h, wu_ref[...])
        a = gate * jax.nn.sigmoid(gate) * up
        wd = wd_ref[...]
        if valid < tf:
            a = jnp.where(lax.broadcasted_iota(I32, (1, tf), 1) < valid, a, 0.0)
            wd = jnp.where(lax.broadcasted_iota(I32, (tf, 1), 0) < valid, wd, jnp.zeros_like(wd))
        acc[...] += _dot(a.astype(BF16), wd)
        if last:
            normalise_into(1 - slot, xn_ref, sn_ref, shn_ref)
            o_ref[...] = x_ref[...] + g2_ref[...] * acc[...].reshape(nb, tt, d)

    pl.when(j < nj - 1)(lambda: step(tf, False))
    pl.when(j == nj - 1)(lambda: step(last_valid, True))


def _ffn(x, mod, l, g, wg, wu, wd):
    b, t, d = x.shape
    ff = wg.shape[1]
    nb, tt = _tile(b, t, ROWS_FFN)
    nt = t // tt
    tf = min(FF_TILE, ff)
    nj = -(-ff // tf)
    assert nj > 1
    last_tile = (b // nb) * nt - 1

    def nxt(bi, i):
        flat = jnp.minimum(bi * nt + i + 1, last_tile)
        return flat // nt, flat % nt

    def mod_next(k):
        return pl.BlockSpec((None, nb, 1, d), lambda bi, i, j: (l, nxt(bi, i)[0], 0, k))
    return pl.pallas_call(
        functools.partial(_ffn_kernel, last_valid=ff - (nj - 1) * tf),
        grid=(b // nb, nt, nj),
        in_specs=[pl.BlockSpec((nb, tt, d), lambda bi, i, j: (bi, i, 0)),
                  pl.BlockSpec((nb, tt, d), lambda bi, i, j: nxt(bi, i) + (0,)),
                  pl.BlockSpec((1, d), lambda bi, i, j: (0, 0)),
                  _mod_spec(nb, d, 4, l), _mod_spec(nb, d, 3, l), mod_next(4), mod_next(3), _mod_spec(nb, d, 5, l),
                  pl.BlockSpec((d, tf), lambda bi, i, j: (0, j)),
                  pl.BlockSpec((d, tf), lambda bi, i, j: (0, j)),
                  pl.BlockSpec((tf, d), lambda bi, i, j: (j, 0))],
        out_specs=pl.BlockSpec((nb, tt, d), lambda bi, i, j: (bi, i, 0)),
        out_shape=jax.ShapeDtypeStruct((b, t, d), F32),
        scratch_shapes=[pltpu.VMEM((2, nb * tt, d), BF16), pltpu.VMEM((nb * tt, d), F32)],
        compiler_params=_cparams(("arbitrary", "arbitrary", "arbitrary")),
        name="dense_swiglu",
    )(x, x, g, mod, mod, mod, mod, mod, wg, wu, wd)


ROUTE_LANES = LANES_V7X
ROUTE_ROWS = SUBLANES_V7X


def _route_kernel(x_ref, g_ref, s_ref, sh_ref, wr_ref, br_ref, cin_ref, h_ref, r_ref, rt_ref, cout_ref, carry):
    nb, tt, d = x_ref.shape
    tm = nb * tt
    first = jnp.logical_and(pl.program_id(0) == 0, pl.program_id(1) == 0)

    @pl.when(first)
    def _():
        carry[...] = cin_ref[...]

    h = _modnorm(x_ref[...], g_ref[...], s_ref[...], sh_ref[...]).reshape(tm, d)
    h_ref[...] = h
    h_hi = h.astype(BF16)
    h_lo = (h - h_hi.astype(F32)).astype(BF16)
    logits = _dot(h_hi, wr_ref[0]) + (_dot(h_hi, wr_ref[1]) + _dot(h_lo, wr_ref[0])) + br_ref[...]
    lane = lax.broadcasted_iota(I32, (tm, ROUTE_LANES), 1)
    neg = jnp.float32(-jnp.inf)
    l1 = jnp.where(lane < N_EXPERTS, logits, neg)
    m1 = jnp.max(l1, axis=-1, keepdims=True)
    i1 = jnp.min(jnp.where(l1 == m1, lane, ROUTE_LANES), axis=-1, keepdims=True)
    l2 = jnp.where(lane == i1, neg, l1)
    m2 = jnp.max(l2, axis=-1, keepdims=True)
    i2 = jnp.min(jnp.where(l2 == m2, lane, ROUTE_LANES), axis=-1, keepdims=True)
    e2 = jnp.exp(m2 - m1)
    den = 1.0 + e2
    p1 = 1.0 / den
    p2 = e2 / den

    onehot = jnp.logical_or(lane == i1, lane == i2).astype(F32)
    rr = lax.broadcasted_iota(I32, (tm, tm), 0)
    cc = lax.broadcasted_iota(I32, (tm, tm), 1)
    before = (cc < rr).astype(BF16)
    rank = _dot(before, onehot.astype(BF16)) + carry[...]
    r1 = jnp.sum(jnp.where(lane == i1, rank, 0.0), axis=-1, keepdims=True)
    r2 = jnp.sum(jnp.where(lane == i2, rank, 0.0), axis=-1, keepdims=True)
    carry[...] += jnp.sum(onehot, axis=0, keepdims=True)

    out = jnp.where(lane == 0, i1.astype(F32), 0.0)
    out = jnp.where(lane == 1, i2.astype(F32), out)
    out = jnp.where(lane == 2, p1, out)
    out = jnp.where(lane == 3, p2, out)
    out = jnp.where(lane == 4, r1, out)
    out = jnp.where(lane == 5, r2, out)
    r_ref[...] = out
    rt_ref[...] = out.T[:ROUTE_ROWS, :]
    cout_ref[...] = carry[...]


def _route(x, mod, l, g, wr_pad, br_pad, counts_in):
    b, t, d = x.shape
    nb, tt = _tile(b, t, ROWS_ROUTE)
    tm = nb * tt
    nt = t // tt
    const2 = lambda bi, i: (0, 0)
    return pl.pallas_call(
        _route_kernel,
        grid=(b // nb, nt),
        in_specs=[pl.BlockSpec((nb, tt, d), lambda bi, i: (bi, i, 0)),
                  pl.BlockSpec((1, d), const2),
                  _mod_spec(nb, d, 4, l), _mod_spec(nb, d, 3, l),
                  pl.BlockSpec((2, d, ROUTE_LANES), lambda bi, i: (0, 0, 0)),
                  pl.BlockSpec((1, ROUTE_LANES), const2),
                  pl.BlockSpec((1, ROUTE_LANES), const2)],
        out_specs=[pl.BlockSpec((tm, d), lambda bi, i: (bi * nt + i, 0)),
                   pl.BlockSpec((tm, ROUTE_LANES), lambda bi, i: (bi * nt + i, 0)),
                   pl.BlockSpec((ROUTE_ROWS, tm), lambda bi, i: (0, bi * nt + i)),
                   pl.BlockSpec((1, ROUTE_LANES), const2)],
        out_shape=[jax.ShapeDtypeStruct((b * t, d), F32),
                   jax.ShapeDtypeStruct((b * t, ROUTE_LANES), F32),
                   jax.ShapeDtypeStruct((ROUTE_ROWS, b * t), F32),
                   jax.ShapeDtypeStruct((1, ROUTE_LANES), F32)],
        scratch_shapes=[pltpu.VMEM((1, ROUTE_LANES), F32)],
        compiler_params=_cparams(("arbitrary", "arbitrary")),
        name="moe_route",
    )(x, g, mod, mod, wr_pad, br_pad, counts_in)


def _row_gather_copy(src_hbm, tok, dst, r, sem):
    return pltpu.make_async_copy(src_hbm.at[pl.ds(tok, 1), :], dst.at[pl.ds(r, 1), :], sem)


def _moe_kernel(te_ref, tr_ref, tsp_ref, src_ref, ha_hbm, hb_hbm, wg_ref, wu_ref, wd_ref, o_hbm,
                xb, acc, stage, wgu, wdb, gsem, osem):
    s = pl.program_id(0)
    j = pl.program_id(1)
    ns = pl.num_programs(0)
    nj = pl.num_programs(1)
    rows = tr_ref[s]
    nsub = (rows + MOE_SUB - 1) // MOE_SUB
    tile_rows = xb.shape[0]

    def out_copy():
        return pltpu.make_async_copy(acc, o_hbm.at[pl.ds(pl.multiple_of(s * tile_rows, MOE_SUB), tile_rows), :], osem)

    @pl.when(j == 0)
    def _():
        @pl.when(rows > 0)
        def _gather():
            def issue(sub, slot):
                base = s * tile_rows + sub * MOE_SUB
                na = jnp.clip(tsp_ref[s] - sub * MOE_SUB, 0, MOE_SUB)

                def start_from(h_hbm):
                    def body(r, carry):
                        _row_gather_copy(h_hbm, src_ref[base + r], stage.at[slot], r, gsem.at[slot]).start()
                        return carry
                    return body

                @pl.when(na == MOE_SUB)
                def _():
                    lax.fori_loop(0, MOE_SUB, start_from(ha_hbm), 0, unroll=GATHER_UNROLL)

                @pl.when(na == 0)
                def _():
                    lax.fori_loop(0, MOE_SUB, start_from(hb_hbm), 0, unroll=GATHER_UNROLL)

                @pl.when(jnp.logical_and(na > 0, na < MOE_SUB))
                def _():
                    lax.fori_loop(0, na, start_from(ha_hbm), 0)
                    lax.fori_loop(na, MOE_SUB, start_from(hb_hbm), 0)

            issue(0, 0)

            def sub_body(sub, carry):
                slot = sub % 2

                @pl.when(sub + 1 < nsub)
                def _():
                    issue(sub + 1, 1 - slot)
                pltpu.make_async_copy(ha_hbm.at[pl.ds(0, MOE_SUB), :], stage.at[slot], gsem.at[slot]).wait()
                r0 = pl.multiple_of(sub * MOE_SUB, MOE_SUB)
                xb[pl.ds(r0, MOE_SUB), :] = stage[slot].astype(BF16)
                return carry
            lax.fori_loop(0, nsub, sub_body, 0)

        @pl.when(s > 0)
        def _():
            out_copy().wait()
        acc[...] = jnp.zeros_like(acc)

    @pl.when(rows > 0)
    def _():
        tf = wg_ref.shape[1]
        wgu[:, :tf] = wg_ref[...].astype(BF16)
        wgu[:, tf:] = wu_ref[...].astype(BF16)
        wdb[...] = wd_ref[...].astype(BF16)

        def block(r0, m):
            gu = _dot(xb[pl.ds(r0, m), :], wgu[...])
            gate, up = gu[:, :tf], gu[:, tf:]
            a = (gate * jax.nn.sigmoid(gate) * up).astype(BF16)
            acc[pl.ds(r0, m), :] += _dot(a, wdb[...])

        per_big = MOE_BLK_BIG // MOE_SUB
        per_blk = MOE_BLK // MOE_SUB
        nbig = nsub // per_big
        rem_big = nsub - nbig * per_big
        has_blk = rem_big >= per_blk
        rem = rem_big - jnp.where(has_blk, per_blk, 0)
        r_blk = pl.multiple_of(nbig * MOE_BLK_BIG, MOE_BLK)
        r_tail = pl.multiple_of(r_blk + jnp.where(has_blk, MOE_BLK, 0), MOE_SUB)

        def body(blk, carry):
            block(pl.multiple_of(blk * MOE_BLK_BIG, MOE_BLK_BIG), MOE_BLK_BIG)
            return carry
        lax.fori_loop(0, nbig, body, 0)
        pl.when(has_blk)(lambda: block(r_blk, MOE_BLK))
        for k in range(1, per_blk):
            @pl.when(rem == k)
            def _(k=k):
                block(r_tail, k * MOE_SUB)

    @pl.when(j == nj - 1)
    def _():
        out_copy().start()

        @pl.when(s == ns - 1)
        def _():
            out_copy().wait()


def _moe_experts(h_a, h_b, te, tr, tsp, src, wg, wu, wd, n_tiles):
    d = h_a.shape[1]
    assert h_a.shape[0] >= MOE_SUB
    e, _, ff = wg.shape
    tf = min(MOE_FF_TILE, ff)
    assert ff % tf == 0 and MOE_TILE_ROWS % MOE_BLK == 0 and MOE_BLK % MOE_SUB == 0 and MOE_BLK_BIG == 2 * MOE_BLK
    nj = ff // tf
    r = MOE_TILE_ROWS

    def jj(s, j, tr_ref):
        return jnp.where(tr_ref[s] > 0, j, nj - 1)

    grid_spec = pltpu.PrefetchScalarGridSpec(
        num_scalar_prefetch=4,
        grid=(n_tiles, nj),
        in_specs=[pl.BlockSpec(memory_space=pl.ANY), pl.BlockSpec(memory_space=pl.ANY),
                  pl.BlockSpec((None, d, tf), lambda s, j, te_r, tr_r, *_: (te_r[s], 0, jj(s, j, tr_r))),
                  pl.BlockSpec((None, d, tf), lambda s, j, te_r, tr_r, *_: (te_r[s], 0, jj(s, j, tr_r))),
                  pl.BlockSpec((None, tf, d), lambda s, j, te_r, tr_r, *_: (te_r[s], jj(s, j, tr_r), 0))],
        out_specs=pl.BlockSpec(memory_space=pl.ANY),
        scratch_shapes=[pltpu.VMEM((r, d), BF16), pltpu.VMEM((r, d), F32), pltpu.VMEM((2, MOE_SUB, d), F32),
                        pltpu.VMEM((d, 2 * tf), BF16), pltpu.VMEM((tf, d), BF16),
                        pltpu.SemaphoreType.DMA((2,)), pltpu.SemaphoreType.DMA(())],
    )
    return pl.pallas_call(
        _moe_kernel,
        grid_spec=grid_spec,
        out_shape=jax.ShapeDtypeStruct((n_tiles * r, d), F32),
        compiler_params=_cparams(("arbitrary", "arbitrary")),
        name="moe_experts",
    )(te, tr, tsp, src, h_a, h_b, wg, wu, wd)


def _combine_kernel(s1_ref, s2_ref, x_ref, g2_ref, r_ref, gf_ref, e_hbm, o_ref, st1, st2, sem):
    nb, tt, d = x_ref.shape
    tm = nb * tt
    step = pl.program_id(0) * pl.num_programs(1) + pl.program_id(1)
    nsteps = pl.num_programs(0) * pl.num_programs(1)
    slot = step % 2

    def issue(tile, sl):
        base = tile * tm

        def body(r, carry):
            _row_gather_copy(e_hbm, s1_ref[base + r], st1.at[sl], r, sem.at[0, sl]).start()
            _row_gather_copy(e_hbm, s2_ref[base + r], st2.at[sl], r, sem.at[1, sl]).start()
            return carry
        lax.fori_loop(0, tm, body, 0, unroll=GATHER_UNROLL)

    pl.when(step == 0)(lambda: issue(0, 0))
    pl.when(step + 1 < nsteps)(lambda: issue(step + 1, 1 - slot))
    pltpu.make_async_copy(e_hbm.at[pl.ds(0, tm), :], st1.at[slot], sem.at[0, slot]).wait()
    pltpu.make_async_copy(e_hbm.at[pl.ds(0, tm), :], st2.at[slot], sem.at[1, slot]).wait()

    f = r_ref[:, 2:3] * st1[slot] + r_ref[:, 3:4] * st2[slot]
    x2 = x_ref[...] + g2_ref[...] * f.reshape(nb, tt, d)
    rs = lax.rsqrt(jnp.mean(x2 * x2, axis=-1, keepdims=True) + EPS)
    o_ref[...] = (x2 * rs) * gf_ref[...]


def _combine(x, mod, l, route, slot1, slot2, e_out, gf):
    b, t, d = x.shape
    nb, tt = _tile(b, t, ROWS_COMBINE)
    tm = nb * tt
    nt = t // tt
    grid_spec = pltpu.PrefetchScalarGridSpec(
        num_scalar_prefetch=2,
        grid=(b // nb, nt),
        in_specs=[pl.BlockSpec((nb, tt, d), lambda bi, i, a, c: (bi, i, 0)),
                  _mod_spec(nb, d, 5, l),
                  pl.BlockSpec((tm, ROUTE_LANES), lambda bi, i, a, c: (bi * nt + i, 0)),
                  pl.BlockSpec((1, d), lambda bi, i, a, c: (0, 0)),
                  pl.BlockSpec(memory_space=pl.ANY)],
        out_specs=pl.BlockSpec((nb, tt, d), lambda bi, i, a, c: (bi, i, 0)),
        scratch_shapes=[pltpu.VMEM((2, tm, d), F32), pltpu.VMEM((2, tm, d), F32), pltpu.SemaphoreType.DMA((2, 2))],
    )
    return pl.pallas_call(
        _combine_kernel,
        grid_spec=grid_spec,
        out_shape=jax.ShapeDtypeStruct((b, t, d), F32),
        compiler_params=_cparams(("arbitrary", "arbitrary")),
        name="moe_combine_norm",
    )(slot1, slot2, x, mod, route, gf, e_out)


def _lookup(table, idx):
    out = jnp.zeros_like(idx)
    for e in range(N_EXPERTS):
        out = jnp.where(idx == e, table[e], out)
    return out


def _floor_div(a, b):
    q = jnp.floor(a.astype(F32) / b.astype(F32)).astype(I32)
    q = jnp.where((q + 1) * b <= a, q + 1, q)
    return jnp.where(q * b > a, q - 1, q)


def _moe_layout(route_t, counts_f, counts_a_f, n_a):
    n = route_t.shape[1]
    r, sub = MOE_TILE_ROWS, MOE_SUB
    n_tiles = -(-TOP_K * n // r) + N_EXPERTS
    i1 = route_t[0].astype(I32)
    i2 = route_t[1].astype(I32)
    r1 = route_t[4].astype(I32)
    r2 = route_t[5].astype(I32)
    counts = counts_f[0, :N_EXPERTS].astype(I32)
    nt = (counts + r - 1) // r
    ntc = jnp.maximum(nt, 1)
    sz = jnp.maximum(((_floor_div(counts + ntc - 1, ntc) + sub - 1) // sub) * sub, sub)
    cum = jnp.cumsum(nt)
    tstart = cum - nt
    total = cum[-1]
    s_idx = jnp.arange(n_tiles, dtype=I32)
    te_raw = jnp.minimum(jnp.sum((s_idx[:, None] >= cum[None, :]).astype(I32), axis=1), N_EXPERTS - 1)
    used = s_idx < total
    last_e = jnp.max(jnp.where(used, te_raw, 0))
    te = jnp.where(used, te_raw, last_e)
    sz_t = _lookup(sz, te)
    k_in = s_idx - _lookup(tstart, te)
    tr = jnp.where(used, jnp.clip(_lookup(counts, te) - k_in * sz_t, 0, sz_t), 0).astype(I32)
    counts_a = counts_a_f[0, :N_EXPERTS].astype(I32)
    tsp = jnp.clip(_lookup(counts_a, te) - k_in * sz_t, 0, tr).astype(I32)

    def slot(ei, ri):
        sz_i = _lookup(sz, ei)
        k = _floor_div(ri, sz_i)
        return (_lookup(tstart, ei) + k) * r + (ri - k * sz_i)
    slot1, slot2 = slot(i1, r1), slot(i2, r2)
    tok = jnp.arange(n, dtype=I32)
    tok = jnp.where(tok < n_a, tok, tok - n_a)
    src = jnp.zeros((n_tiles * r,), I32).at[jnp.concatenate([slot1, slot2])].set(jnp.concatenate([tok, tok]))
    return te, tr, tsp, src, slot1, slot2, n_tiles


def kernel(x_prompt, x_sample, c_prompt, c_sample, state_pool, state_ssm_re, state_ssm_im, w_ada, b_ada, norm_mix, norm_ffn, w_in, w_pool, pool_scale, ssm_lam_re, ssm_lam_im, ssm_log_step, ssm_b_re, ssm_b_im, ssm_c_re, ssm_c_im, ssm_d, ssm_w_glu, w_out, ffn_w_gate, ffn_w_up, ffn_w_down, moe_w_router, moe_b_router, moe_w_gate, moe_w_up, moe_w_down, norm_final):
    bp, tp, d = x_prompt.shape
    bs, ts, _ = x_sample.shape
    depth = w_ada.shape[0]
    c_pool = w_pool.shape[1] * w_pool.shape[2]
    assert tp % CHUNK == 0 and ts <= CHUNK

    m_rows = -(-(bp + bs) // SUBLANES_V7X) * SUBLANES_V7X
    c_all = jnp.concatenate([c_prompt, c_sample, jnp.zeros((m_rows - bp - bs, d), F32)], axis=0)
    mod_all = _ada(c_all, w_ada, b_ada)

    mod_p = mod_all[:, :bp].reshape(depth, bp, 1, 6 * d)
    mod_s = mod_all[:, bp:bp + bs].reshape(depth, bs, 1, 6 * d)
    w_in_b, w_out_b = w_in.astype(BF16), w_out.astype(BF16)
    wp_b, wglu_b = w_pool.astype(BF16), ssm_w_glu.astype(BF16)
    ps = pool_scale.reshape(depth, 1, c_pool).astype(F32)
    dsk = ssm_d.reshape(depth, 1, d - c_pool).astype(F32)
    k_t, b_t, c_t, p16, q16, p8b, q8b = jax.vmap(lambda *a: _s5_params(*a, ts, bs))(
        ssm_lam_re, ssm_lam_im, ssm_log_step, ssm_b_re, ssm_b_im, ssm_c_re, ssm_c_im)
    x0_t = jnp.concatenate([state_ssm_re, state_ssm_im], axis=-1).astype(F32).transpose(0, 2, 3, 1)
    st_p = jnp.zeros((1, bp, POOL_BUF, c_pool), F32)
    st_s = state_pool.astype(F32)

    xp, xs = x_prompt, x_sample
    pool_p, re_p, im_p, pool_s, re_s, im_s = [], [], [], [], [], []
    y_p = y_s = None
    for l in range(depth):
        gm = norm_mix[l].reshape(1, d)
        gn = norm_ffn[l].reshape(1, d)

        up_all = _proj_in(xp, mod_p, l, gm, w_in_b)
        us_all = _proj_in(xs, mod_s, l, gm, w_in_b)
        y_ssm_p, xend = _s5_prompt(up_all, l, k_t, b_t, c_t, p16, q16, c_pool)
        ys2, x1_t = _s5_sample(us_all.reshape(bs * ts, d), l, x0_t, k_t, b_t, c_t, p8b, q8b, ts, c_pool)
        y_ssm_s = ys2.reshape(bs, ts, d - c_pool)
        ffn_f32 = (ffn_w_gate[l // 2], ffn_w_up[l // 2], ffn_w_down[l // 2]) if l % 2 == 0 else ()
        xp, new_pool_p, ffn_bf16 = _mix(xp, up_all, y_ssm_p, st_p, 0, mod_p, l, wp_b, ps, dsk, wglu_b, w_out_b, 0,
                                        cast=ffn_f32)
        xs, new_pool_s, _ = _mix(xs, us_all, y_ssm_s, st_s, l, mod_s, l, wp_b, ps, dsk, wglu_b, w_out_b, PAST_LEN)
        pool_p.append(new_pool_p)
        pool_s.append(new_pool_s)
        re_p.append(xend[..., :SSM_P])
        im_p.append(xend[..., SSM_P:])
        re_s.append(x1_t[:, :SSM_P, :].transpose(2, 0, 1))
        im_s.append(x1_t[:, SSM_P:, :].transpose(2, 0, 1))

        i = l // 2
        if l % 2 == 0:
            wg, wu, wd = ffn_bf16
            xp = _ffn(xp, mod_p, l, gn, wg, wu, wd)
            xs = _ffn(xs, mod_s, l, gn, wg, wu, wd)
        else:
            ne = moe_w_router.shape[-1]
            wr32 = jnp.pad(moe_w_router[i].astype(F32), ((0, 0), (0, ROUTE_LANES - ne)))
            wr_hi = wr32.astype(BF16)
            wr = jnp.stack([wr_hi, (wr32 - wr_hi.astype(F32)).astype(BF16)])
            br = jnp.pad(moe_b_router[i].astype(F32), (0, ROUTE_LANES - ne)).reshape(1, ROUTE_LANES)
            zero_counts = jnp.zeros((1, ROUTE_LANES), F32)
            h_p, route_p, rt_p, cnt_p = _route(xp, mod_p, l, gn, wr, br, zero_counts)
            h_s, route_s, rt_s, cnt_s = _route(xs, mod_s, l, gn, wr, br, cnt_p)
            n_p = bp * tp
            te, tr, tsp, src, slot1, slot2, n_tiles = _moe_layout(jnp.concatenate([rt_p, rt_s], axis=1), cnt_s, cnt_p,
                                                                  n_p)
            e_out = _moe_experts(h_p, h_s, te, tr, tsp, src, moe_w_gate[i], moe_w_up[i], moe_w_down[i], n_tiles)
            last = l == depth - 1
            assert last, "the combine kernel also applies the final norm"
            gf = norm_final.reshape(1, d)
            y_p = _combine(xp, mod_p, l, route_p, slot1[:n_p], slot2[:n_p], e_out, gf)
            y_s = _combine(xs, mod_s, l, route_s, slot1[n_p:], slot2[n_p:], e_out, gf)

    return (y_p, y_s, jnp.stack(pool_p), jnp.stack(re_p), jnp.stack(im_p),
            jnp.stack(pool_s), jnp.stack(re_s), jnp.stack(im_s))
```

```python
import functools
import math

import jax
import jax.numpy as jnp
import numpy as np
from jax import lax
from jax.experimental import pallas as pl
from jax.experimental.pallas import tpu as pltpu

F32 = jnp.float32
BF16 = jnp.bfloat16
I32 = jnp.int32

EPS = 1e-6
POOL_WINDOWS = (2, 4, 8, 16)
POOL_BUF = max(POOL_WINDOWS) - 1
SUBLANES_V7X = 8
BF16_SUBLANES_V7X = 16
LANES_V7X = 128
HALO = -(-POOL_BUF // SUBLANES_V7X) * SUBLANES_V7X
SSM_H = 16
SSM_P = 64
CHUNK = 16
CW = CHUNK * SSM_H
SW = 2 * SSM_P
N_EXPERTS = 8
TOP_K = 2
PAST_LEN = 16384

VMEM_LIMIT_V7X = 56 * 1024 * 1024
ROWS_IN = 512
ROWS_MIX = 256
ROWS_FFN = 512
FF_TILE = 512
ROWS_ROUTE = 512
MOE_TILE_ROWS = 2560
MOE_SUB = 128
MOE_BLK_BIG = 1024
MOE_BLK = 512
GATHER_UNROLL = 8
MOE_FF_TILE = 256
ROWS_COMBINE = 256
ADA_TN = 1024


def _cparams(sem):
    return pltpu.CompilerParams(dimension_semantics=sem, vmem_limit_bytes=VMEM_LIMIT_V7X)


def _tile(b, t, rows):
    if t >= rows:
        assert t % rows == 0
        return 1, rows
    nb = max(1, min(b, rows // t))
    assert b % nb == 0
    return nb, t


def _modnorm(x, g, s, sh):
    r = lax.rsqrt(jnp.mean(x * x, axis=-1, keepdims=True) + EPS)
    return (x * r) * g * (1.0 + s) + sh


def _dot(a, b):
    return jnp.dot(a, b, preferred_element_type=F32)


def _mod_spec(nb, d, k, l):
    return pl.BlockSpec((None, nb, 1, d), lambda b, i, *_: (l, b, 0, k))


def _layer_spec(shape, l):
    return pl.BlockSpec((None,) + tuple(shape), lambda *_: (l,) + (0,) * len(shape))


def _ada_kernel(c_ref, w_ref, b_ref, o_ref):
    c = c_ref[...]
    sc = c * jax.nn.sigmoid(c)
    o_ref[0] = _dot(sc.astype(BF16), w_ref[0].astype(BF16)) + b_ref[0]


def _ada(c_all, w_ada, b_ada):
    m, d = c_all.shape
    depth, _, n6 = w_ada.shape
    tn = math.gcd(ADA_TN, n6)
    return pl.pallas_call(
        _ada_kernel,
        grid=(depth, n6 // tn),
        in_specs=[pl.BlockSpec((m, d), lambda l, n: (0, 0)),
                  pl.BlockSpec((1, d, tn), lambda l, n: (l, 0, n)),
                  pl.BlockSpec((1, 1, tn), lambda l, n: (l, 0, n))],
        out_specs=pl.BlockSpec((1, m, tn), lambda l, n: (l, 0, n)),
        out_shape=jax.ShapeDtypeStruct((depth, m, n6), F32),
        compiler_params=_cparams(("arbitrary", "arbitrary")),
        name="adaln_mod",
    )(c_all, w_ada, b_ada.reshape(depth, 1, n6))


def _in_kernel(x_ref, g_ref, s_ref, sh_ref, w_ref, u_ref):
    nb, tt, d = x_ref.shape
    h = _modnorm(x_ref[...], g_ref[...], s_ref[...], sh_ref[...])
    u = _dot(h.reshape(nb * tt, d).astype(BF16), w_ref[...])
    u_ref[...] = u.reshape(nb, tt, u_ref.shape[-1])


def _proj_in(x, mod, l, g, w_bf16):
    b, t, d = x.shape
    n = w_bf16.shape[-1]
    nb, tt = _tile(b, t, ROWS_IN)
    return pl.pallas_call(
        _in_kernel,
        grid=(b // nb, t // tt),
        in_specs=[pl.BlockSpec((nb, tt, d), lambda bi, i: (bi, i, 0)),
                  pl.BlockSpec((1, d), lambda bi, i: (0, 0)),
                  _mod_spec(nb, d, 1, l), _mod_spec(nb, d, 0, l),
                  _layer_spec((d, n), l)],
        out_specs=pl.BlockSpec((nb, tt, n), lambda bi, i: (bi, i, 0)),
        out_shape=jax.ShapeDtypeStruct((b, t, n), F32),
        compiler_params=_cparams(("arbitrary", "arbitrary")),
        name="norm_proj_in",
    )(x, g, mod, mod, w_bf16)


GROUPS_PER_LANE_TILE = LANES_V7X // SSM_H


def _toeplitz_placement():
    toe = np.zeros((CHUNK, CW, CW), np.float32)
    for s in range(CHUNK):
        for h in range(SSM_H):
            for t in range(s, CHUNK):
                toe[t, (t - s) * SSM_H + h, s * SSM_H + h] = 1.0
    return jnp.asarray(toe)


def _s5_params(lam_re, lam_im, log_step, b_re, b_im, c_re, c_im, ts, bs):
    hp = lax.Precision.HIGHEST
    toe = _toeplitz_placement()
    lr, li = lam_re.astype(F32), lam_im.astype(F32)
    step = jnp.exp(log_step.astype(F32))[:, None]
    mag = jnp.exp(lr * step)
    a_re = mag * jnp.cos(li * step)
    a_im = mag * jnp.sin(li * step)
    den = lr * lr + li * li
    nr = a_re - 1.0
    f_re = (nr * lr + a_im * li) / den
    f_im = (a_im * lr - nr * li) / den
    br, bi = b_re.astype(F32), b_im.astype(F32)
    bb_re = f_re[..., None] * br - f_im[..., None] * bi
    bb_im = f_re[..., None] * bi + f_im[..., None] * br
    cr, ci = c_re.astype(F32), c_im.astype(F32)
    g = lr.shape[0]

    pw_re, pw_im = [jnp.ones_like(a_re)], [jnp.zeros_like(a_im)]
    for _ in range(CHUNK):
        pr, pi = pw_re[-1], pw_im[-1]
        pw_re.append(pr * a_re - pi * a_im)
        pw_im.append(pr * a_im + pi * a_re)
    pw_re, pw_im = jnp.stack(pw_re), jnp.stack(pw_im)

    bbt_re, bbt_im = bb_re.transpose(0, 2, 1), bb_im.transpose(0, 2, 1)
    w_re = pw_re[:CHUNK, :, None, :] * bbt_re - pw_im[:CHUNK, :, None, :] * bbt_im
    w_im = pw_re[:CHUNK, :, None, :] * bbt_im + pw_im[:CHUNK, :, None, :] * bbt_re
    m = (jnp.einsum('gop,lgip->goli', cr, w_re, precision=hp)
         - jnp.einsum('gop,lgip->goli', ci, w_im, precision=hp)).reshape(g, SSM_H, CW)
    k_t = jnp.einsum('gok,tkn->gton', m.astype(BF16), toe.astype(BF16),
                     preferred_element_type=F32).reshape(g, CW, CW)

    rev_re, rev_im = pw_re[:CHUNK][::-1], pw_im[:CHUNK][::-1]
    ar = jnp.repeat(rev_re.transpose(1, 2, 0), SSM_H, axis=-1)
    ai = jnp.repeat(rev_im.transpose(1, 2, 0), SSM_H, axis=-1)
    xr = jnp.tile(bb_re, (1, 1, CHUNK))
    xi = jnp.tile(bb_im, (1, 1, CHUNK))
    bt_re = ar * xr - ai * xi
    bt_im = ar * xi + ai * xr
    b_t = jnp.concatenate([bt_re, bt_im, bt_im, bt_re], axis=1)

    qr = pw_re[1:].transpose(1, 0, 2)[:, :, None, :]
    qi = pw_im[1:].transpose(1, 0, 2)[:, :, None, :]
    cm_re = cr[:, None] * qr - ci[:, None] * qi
    cm_im = -(cr[:, None] * qi + ci[:, None] * qr)
    c_t = jnp.concatenate([cm_re, cm_im], axis=-1).reshape(g, CW, SW)

    def coef(k):
        pr, pi = pw_re[k], pw_im[k]
        return (jnp.concatenate([pr, pr, pr, pr], axis=-1),
                jnp.concatenate([-pi, pi, pi, -pi], axis=-1))
    p16, q16 = coef(CHUNK)
    p8, q8 = coef(ts)
    p8b = jnp.broadcast_to(p8[:, :SW, None], (g, SW, bs))
    q8b = jnp.broadcast_to(q8[:, :SW, None], (g, SW, bs))
    return k_t.astype(BF16), b_t.astype(BF16), c_t.astype(BF16), p16, q16, p8b, q8b


def _s5_prompt_kernel(us_ref, kt_ref, bt_ref, ct_ref, p_ref, q_ref, y_ref, xe_ref, ut, yt, scan_x, scan_xs, s_scr):
    gt = ut.shape[0]
    nchunk = ut.shape[-1]
    for t in range(CHUNK):
        slab = us_ref[0, pl.ds(t, nchunk, stride=CHUNK), :]
        ut[:, t] = slab.T.reshape(gt, SSM_H, nchunk).astype(BF16)
    for gi in range(gt):
        buz = _dot(bt_ref[gi], ut[gi].reshape(CW, nchunk))
        buz_t = buz.T
        scan_x[pl.ds(gi, nchunk, stride=gt), :] = buz_t[:, :SW]
        scan_xs[pl.ds(gi, nchunk, stride=gt), :] = buz_t[:, SW:]
    px, pxs = p_ref[:, :SW], p_ref[:, SW:]
    qx, qxs = q_ref[:, :SW], q_ref[:, SW:]
    x = jnp.zeros((gt, SW), F32)
    xs = jnp.zeros((gt, SW), F32)
    for c in range(nchunk):
        lo, hi = c * gt, (c + 1) * gt
        s_scr[lo:hi, :] = x
        x, xs = (px * x + qx * xs + scan_x[lo:hi, :],
                 pxs * xs + qxs * x + scan_xs[lo:hi, :])
    xe_ref[0] = x
    for gi in range(gt):
        st = s_scr[pl.ds(gi, nchunk, stride=gt), :].T.astype(BF16)
        y_t = _dot(kt_ref[gi], ut[gi].reshape(CW, nchunk)) + _dot(ct_ref[gi], st)
        yt[gi] = y_t.reshape(CHUNK, SSM_H, nchunk)
    for t in range(CHUNK):
        y_ref[0, pl.ds(t, nchunk, stride=CHUNK), :] = yt[:, t].reshape(gt * SSM_H, nchunk).T


def _s5_prompt(u, l, k_t, b_t, c_t, p16, q16, c_off):
    b, t, _ = u.shape
    g = k_t.shape[1]
    gt = GROUPS_PER_LANE_TILE
    lt = gt * SSM_H
    nchunk = t // CHUNK
    off = c_off // lt

    def wspec(r, c):
        return pl.BlockSpec((None, gt, r, c), lambda bi, j: (l, j, 0, 0))
    return pl.pallas_call(
        _s5_prompt_kernel,
        grid=(b, g // gt),
        in_specs=[pl.BlockSpec((1, t, lt), lambda bi, j: (bi, 0, off + j)),
                  wspec(CW, CW), wspec(2 * SW, CW), wspec(CW, SW),
                  pl.BlockSpec((None, gt, 2 * SW), lambda bi, j: (l, j, 0)),
                  pl.BlockSpec((None, gt, 2 * SW), lambda bi, j: (l, j, 0))],
        out_specs=[pl.BlockSpec((1, t, lt), lambda bi, j: (bi, 0, j)),
                   pl.BlockSpec((1, gt, SW), lambda bi, j: (bi, j, 0))],
        out_shape=[jax.ShapeDtypeStruct((b, t, g * SSM_H), F32), jax.ShapeDtypeStruct((b, g, SW), F32)],
        scratch_shapes=[pltpu.VMEM((gt, CHUNK, SSM_H, nchunk), BF16), pltpu.VMEM((gt, CHUNK, SSM_H, nchunk), F32),
                        pltpu.VMEM((nchunk * gt, SW), F32), pltpu.VMEM((nchunk * gt, SW), F32),
                        pltpu.VMEM((nchunk * gt, SW), F32)],
        compiler_params=_cparams(("arbitrary", "arbitrary")),
        name="s5_prompt",
    )(u, k_t, b_t, c_t, p16, q16)


def _s5_sample_kernel(us_ref, x0_ref, kt_ref, bt_ref, ct_ref, p_ref, q_ref, y_ref, x1_ref, ut, yt, *, ts):
    gt = ut.shape[0]
    bs = ut.shape[-1]
    w = ts * SSM_H
    for t in range(ts):
        slab = us_ref[pl.ds(t, bs, stride=ts), :]
        ut[:, t] = slab.T.reshape(gt, SSM_H, bs).astype(BF16)
    for gi in range(gt):
        ug = ut[gi].reshape(w, bs)
        x0 = x0_ref[gi]
        x0s = jnp.concatenate([x0[SSM_P:], x0[:SSM_P]], axis=0)
        y_t = _dot(kt_ref[gi, :w, :w], ug) + _dot(ct_ref[gi, :w, :], x0.astype(BF16))
        x1_ref[gi] = p_ref[gi] * x0 + q_ref[gi] * x0s + _dot(bt_ref[gi, :SW, CW - w:], ug)
        yt[gi] = y_t.reshape(ts, SSM_H, bs)
    for t in range(ts):
        y_ref[pl.ds(t, bs, stride=ts), :] = yt[:, t].reshape(gt * SSM_H, bs).T


def _s5_sample(u2, l, x0_t, k_t, b_t, c_t, p8b, q8b, ts, c_off):
    rows, _ = u2.shape
    _, g, _, bs = x0_t.shape
    gt = GROUPS_PER_LANE_TILE
    lt = gt * SSM_H
    off = c_off // lt

    def wspec(r, c):
        return pl.BlockSpec((None, gt, r, c), lambda j: (l, j, 0, 0))
    return pl.pallas_call(
        functools.partial(_s5_sample_kernel, ts=ts),
        grid=(g // gt,),
        in_specs=[pl.BlockSpec((rows, lt), lambda j: (0, off + j)),
                  wspec(SW, bs), wspec(CW, CW), wspec(2 * SW, CW), wspec(CW, SW), wspec(SW, bs), wspec(SW, bs)],
        out_specs=[pl.BlockSpec((rows, lt), lambda j: (0, j)), pl.BlockSpec((gt, SW, bs), lambda j: (j, 0, 0))],
        out_shape=[jax.ShapeDtypeStruct((rows, g * SSM_H), F32), jax.ShapeDtypeStruct((g, SW, bs), F32)],
        scratch_shapes=[pltpu.VMEM((gt, ts, SSM_H, bs), BF16), pltpu.VMEM((gt, ts, SSM_H, bs), F32)],
        compiler_params=_cparams(("arbitrary",)),
        name="s5_sample",
    )(u2, x0_t, k_t, b_t, c_t, p8b, q8b)


def _mix_kernel(x_ref, up_ref, us_ref, ys_ref, st_ref, g1_ref, wp_ref, ps_ref, dsk_ref, wglu_ref,
                wout_ref, *rest, start_pos, carry, ncast):
    cast_in, (o_ref, np_ref), cast_out, ext = rest[:ncast], rest[ncast:ncast + 2], rest[ncast + 2:-1], rest[-1]
    for src, dst in zip(cast_in, cast_out):
        dst[...] = src[...].astype(BF16)
    nb, tt, d = x_ref.shape
    c = up_ref.shape[-1]
    pg = c // len(POOL_WINDOWS)
    rows = nb * tt
    i = pl.program_id(1)

    @pl.when(i == 0)
    def _():
        ext[:, HALO - POOL_BUF:HALO, :] = st_ref[...]
    ext[:, HALO:HALO + tt, :] = up_ref[...]
    np_ref[...] = ext[:, HALO + tt - POOL_BUF:HALO + tt, :]

    pos = start_pos + i * tt + lax.broadcasted_iota(I32, (1, tt, 1), 1)
    outs = []
    for gidx, w in enumerate(POOL_WINDOWS):
        c0 = gidx * pg
        cur = ext[:, HALO:HALO + tt, c0:c0 + pg]
        s = cur
        for k in range(1, w):
            s = s + ext[:, HALO - k:HALO - k + tt, c0:c0 + pg]
        cnt = jnp.minimum(pos + 1, w).astype(F32)
        dd = s / cnt - cur
        yg = _dot(dd.reshape(rows, pg).astype(BF16), wp_ref[gidx])
        outs.append(yg * ps_ref[:, c0:c0 + pg])
    y_pool = jnp.concatenate(outs, axis=-1)

    if carry:
        ext[:, 0:HALO, :] = ext[:, tt:tt + HALO, :]

    us = us_ref[...].reshape(rows, c)
    yf = ys_ref[...].reshape(rows, c) + dsk_ref[...] * us
    gl = jax.nn.gelu(yf, approximate=True)
    o = gl * jax.nn.sigmoid(_dot(gl.astype(BF16), wglu_ref[...]))
    mo = _dot(y_pool.astype(BF16), wout_ref[:c, :]) + _dot(o.astype(BF16), wout_ref[c:, :])
    o_ref[...] = x_ref[...] + g1_ref[...] * mo.reshape(nb, tt, d)


def _mix(x, u, y_ssm, pool_state, l_state, mod, l, wp, ps, dsk, wglu, wout, start_pos, cast=()):
    b, t, d = x.shape
    c = y_ssm.shape[-1]
    nb, tt = _tile(b, t, ROWS_MIX)
    nt = t // tt
    assert nt == 1 or tt >= HALO
    nsteps = (b // nb) * nt
    cast_specs = []
    for a in cast:
        blk = -(-(-(-a.shape[0] // nsteps)) // BF16_SUBLANES_V7X) * BF16_SUBLANES_V7X
        last = -(-a.shape[0] // blk) - 1
        cast_specs.append(pl.BlockSpec((blk, a.shape[1]), lambda bi, i, last=last: (jnp.minimum(bi * nt + i, last), 0)))
    outs = pl.pallas_call(
        functools.partial(_mix_kernel, start_pos=start_pos, carry=nt > 1, ncast=len(cast)),
        grid=(b // nb, nt),
        in_specs=[pl.BlockSpec((nb, tt, d), lambda bi, i: (bi, i, 0)),
                  pl.BlockSpec((nb, tt, c), lambda bi, i: (bi, i, 0)),
                  pl.BlockSpec((nb, tt, c), lambda bi, i: (bi, i, 1)),
                  pl.BlockSpec((nb, tt, c), lambda bi, i: (bi, i, 0)),
                  pl.BlockSpec((None, nb, POOL_BUF, c), lambda bi, i: (l_state, bi, 0, 0)),
                  _mod_spec(nb, d, 2, l),
                  _layer_spec(wp.shape[1:], l), _layer_spec((1, c), l), _layer_spec((1, c), l),
                  _layer_spec((c, c), l), _layer_spec((d, d), l)] + cast_specs,
        out_specs=[pl.BlockSpec((nb, tt, d), lambda bi, i: (bi, i, 0)),
                   pl.BlockSpec((nb, POOL_BUF, c), lambda bi, i: (bi, 0, 0))] + cast_specs,
        out_shape=[jax.ShapeDtypeStruct((b, t, d), F32), jax.ShapeDtypeStruct((b, POOL_BUF, c), F32)]
        + [jax.ShapeDtypeStruct(a.shape, BF16) for a in cast],
        scratch_shapes=[pltpu.VMEM((nb, HALO + tt, c), F32)],
        compiler_params=_cparams(("arbitrary", "arbitrary")),
        name="pool_glu_proj_out",
    )(x, u, u, y_ssm, pool_state, mod, wp, ps, dsk, wglu, wout, *cast)
    return outs[0], outs[1], tuple(outs[2:])


def _ffn_kernel(x_ref, g_ref, s_ref, sh_ref, g2_ref, wg_ref, wu_ref, wd_ref, o_ref, hb, acc, *, last_valid):
    nb, tt, d = x_ref.shape
    tf = wg_ref.shape[1]
    j = pl.program_id(2)
    nj = pl.num_programs(2)

    @pl.when(j == 0)
    def _():
        h = _modnorm(x_ref[...], g_ref[...], s_ref[...], sh_ref[...])
        hb[...] = h.reshape(nb * tt, d).astype(BF16)
        acc[...] = jnp.zeros_like(acc)

    def step(valid):
        h = hb[...]
        gate = _dot(h, wg_ref[...])
        up = _dot(h, wu_ref[...])
        a = gate * jax.nn.sigmoid(gate) * up
        wd = wd_ref[...]
        if valid < tf:
            a = jnp.where(lax.broadcasted_iota(I32, (1, tf), 1) < valid, a, 0.0)
            wd = jnp.where(lax.broadcasted_iota(I32, (tf, 1), 0) < valid, wd, jnp.zeros_like(wd))
        acc[...] += _dot(a.astype(BF16), wd)

    if last_valid == tf:
        step(tf)
    else:
        pl.when(j < nj - 1)(lambda: step(tf))
        pl.when(j == nj - 1)(lambda: step(last_valid))

    @pl.when(j == nj - 1)
    def _():
        o_ref[...] = x_ref[...] + g2_ref[...] * acc[...].reshape(nb, tt, d)


def _ffn(x, mod, l, g, wg, wu, wd):
    b, t, d = x.shape
    ff = wg.shape[1]
    nb, tt = _tile(b, t, ROWS_FFN)
    tf = min(FF_TILE, ff)
    nj = -(-ff // tf)
    return pl.pallas_call(
        functools.partial(_ffn_kernel, last_valid=ff - (nj - 1) * tf),
        grid=(b // nb, t // tt, nj),
        in_specs=[pl.BlockSpec((nb, tt, d), lambda bi, i, j: (bi, i, 0)),
                  pl.BlockSpec((1, d), lambda bi, i, j: (0, 0)),
                  _mod_spec(nb, d, 4, l), _mod_spec(nb, d, 3, l), _mod_spec(nb, d, 5, l),
                  pl.BlockSpec((d, tf), lambda bi, i, j: (0, j)),
                  pl.BlockSpec((d, tf), lambda bi, i, j: (0, j)),
                  pl.BlockSpec((tf, d), lambda bi, i, j: (j, 0))],
        out_specs=pl.BlockSpec((nb, tt, d), lambda bi, i, j: (bi, i, 0)),
        out_shape=jax.ShapeDtypeStruct((b, t, d), F32),
        scratch_shapes=[pltpu.VMEM((nb * tt, d), BF16), pltpu.VMEM((nb * tt, d), F32)],
        compiler_params=_cparams(("arbitrary", "arbitrary", "arbitrary")),
        name="dense_swiglu",
    )(x, g, mod, mod, mod, wg, wu, wd)


ROUTE_LANES = LANES_V7X
ROUTE_ROWS = SUBLANES_V7X


def _route_kernel(x_ref, g_ref, s_ref, sh_ref, wr_ref, br_ref, cin_ref, h_ref, r_ref, rt_ref, cout_ref, carry):
    nb, tt, d = x_ref.shape
    tm = nb * tt
    first = jnp.logical_and(pl.program_id(0) == 0, pl.program_id(1) == 0)

    @pl.when(first)
    def _():
        carry[...] = cin_ref[...]

    h = _modnorm(x_ref[...], g_ref[...], s_ref[...], sh_ref[...]).reshape(tm, d)
    h_ref[...] = h
    h_hi = h.astype(BF16)
    h_lo = (h - h_hi.astype(F32)).astype(BF16)
    logits = _dot(h_hi, wr_ref[0]) + (_dot(h_hi, wr_ref[1]) + _dot(h_lo, wr_ref[0])) + br_ref[...]
    lane = lax.broadcasted_iota(I32, (tm, ROUTE_LANES), 1)
    neg = jnp.float32(-jnp.inf)
    l1 = jnp.where(lane < N_EXPERTS, logits, neg)
    m1 = jnp.max(l1, axis=-1, keepdims=True)
    i1 = jnp.min(jnp.where(l1 == m1, lane, ROUTE_LANES), axis=-1, keepdims=True)
    l2 = jnp.where(lane == i1, neg, l1)
    m2 = jnp.max(l2, axis=-1, keepdims=True)
    i2 = jnp.min(jnp.where(l2 == m2, lane, ROUTE_LANES), axis=-1, keepdims=True)
    e2 = jnp.exp(m2 - m1)
    den = 1.0 + e2
    p1 = 1.0 / den
    p2 = e2 / den

    onehot = jnp.logical_or(lane == i1, lane == i2).astype(F32)
    rr = lax.broadcasted_iota(I32, (tm, tm), 0)
    cc = lax.broadcasted_iota(I32, (tm, tm), 1)
    before = (cc < rr).astype(BF16)
    rank = _dot(before, onehot.astype(BF16)) + carry[...]
    r1 = jnp.sum(jnp.where(lane == i1, rank, 0.0), axis=-1, keepdims=True)
    r2 = jnp.sum(jnp.where(lane == i2, rank, 0.0), axis=-1, keepdims=True)
    carry[...] += jnp.sum(onehot, axis=0, keepdims=True)

    out = jnp.where(lane == 0, i1.astype(F32), 0.0)
    out = jnp.where(lane == 1, i2.astype(F32), out)
    out = jnp.where(lane == 2, p1, out)
    out = jnp.where(lane == 3, p2, out)
    out = jnp.where(lane == 4, r1, out)
    out = jnp.where(lane == 5, r2, out)
    r_ref[...] = out
    rt_ref[...] = out.T[:ROUTE_ROWS, :]
    cout_ref[...] = carry[...]


def _route(x, mod, l, g, wr_pad, br_pad, counts_in):
    b, t, d = x.shape
    nb, tt = _tile(b, t, ROWS_ROUTE)
    tm = nb * tt
    nt = t // tt
    const2 = lambda bi, i: (0, 0)
    return pl.pallas_call(
        _route_kernel,
        grid=(b // nb, nt),
        in_specs=[pl.BlockSpec((nb, tt, d), lambda bi, i: (bi, i, 0)),
                  pl.BlockSpec((1, d), const2),
                  _mod_spec(nb, d, 4, l), _mod_spec(nb, d, 3, l),
                  pl.BlockSpec((2, d, ROUTE_LANES), lambda bi, i: (0, 0, 0)),
                  pl.BlockSpec((1, ROUTE_LANES), const2),
                  pl.BlockSpec((1, ROUTE_LANES), const2)],
        out_specs=[pl.BlockSpec((tm, d), lambda bi, i: (bi * nt + i, 0)),
                   pl.BlockSpec((tm, ROUTE_LANES), lambda bi, i: (bi * nt + i, 0)),
                   pl.BlockSpec((ROUTE_ROWS, tm), lambda bi, i: (0, bi * nt + i)),
                   pl.BlockSpec((1, ROUTE_LANES), const2)],
        out_shape=[jax.ShapeDtypeStruct((b * t, d), F32),
                   jax.ShapeDtypeStruct((b * t, ROUTE_LANES), F32),
                   jax.ShapeDtypeStruct((ROUTE_ROWS, b * t), F32),
                   jax.ShapeDtypeStruct((1, ROUTE_LANES), F32)],
        scratch_shapes=[pltpu.VMEM((1, ROUTE_LANES), F32)],
        compiler_params=_cparams(("arbitrary", "arbitrary")),
        name="moe_route",
    )(x, g, mod, mod, wr_pad, br_pad, counts_in)


def _row_gather_copy(src_hbm, tok, dst, r, sem):
    return pltpu.make_async_copy(src_hbm.at[pl.ds(tok, 1), :], dst.at[pl.ds(r, 1), :], sem)


def _moe_kernel(te_ref, tr_ref, tsp_ref, src_ref, ha_hbm, hb_hbm, wg_ref, wu_ref, wd_ref, o_hbm,
                xb, acc, stage, wgu, wdb, gsem, osem):
    s = pl.program_id(0)
    j = pl.program_id(1)
    ns = pl.num_programs(0)
    nj = pl.num_programs(1)
    rows = tr_ref[s]
    nsub = (rows + MOE_SUB - 1) // MOE_SUB
    tile_rows = xb.shape[0]

    def out_copy():
        return pltpu.make_async_copy(acc, o_hbm.at[pl.ds(pl.multiple_of(s * tile_rows, MOE_SUB), tile_rows), :], osem)

    @pl.when(j == 0)
    def _():
        @pl.when(rows > 0)
        def _gather():
            def issue(sub, slot):
                base = s * tile_rows + sub * MOE_SUB
                na = jnp.clip(tsp_ref[s] - sub * MOE_SUB, 0, MOE_SUB)

                def start_from(h_hbm):
                    def body(r, carry):
                        _row_gather_copy(h_hbm, src_ref[base + r], stage.at[slot], r, gsem.at[slot]).start()
                        return carry
                    return body

                @pl.when(na == MOE_SUB)
                def _():
                    lax.fori_loop(0, MOE_SUB, start_from(ha_hbm), 0, unroll=GATHER_UNROLL)

                @pl.when(na == 0)
                def _():
                    lax.fori_loop(0, MOE_SUB, start_from(hb_hbm), 0, unroll=GATHER_UNROLL)

                @pl.when(jnp.logical_and(na > 0, na < MOE_SUB))
                def _():
                    lax.fori_loop(0, na, start_from(ha_hbm), 0)
                    lax.fori_loop(na, MOE_SUB, start_from(hb_hbm), 0)

            issue(0, 0)

            def sub_body(sub, carry):
                slot = sub % 2

                @pl.when(sub + 1 < nsub)
                def _():
                    issue(sub + 1, 1 - slot)
                pltpu.make_async_copy(ha_hbm.at[pl.ds(0, MOE_SUB), :], stage.at[slot], gsem.at[slot]).wait()
                r0 = pl.multiple_of(sub * MOE_SUB, MOE_SUB)
                xb[pl.ds(r0, MOE_SUB), :] = stage[slot].astype(BF16)
                return carry
            lax.fori_loop(0, nsub, sub_body, 0)

        @pl.when(s > 0)
        def _():
            out_copy().wait()
        acc[...] = jnp.zeros_like(acc)

    @pl.when(rows > 0)
    def _():
        tf = wg_ref.shape[1]
        wgu[:, :tf] = wg_ref[...].astype(BF16)
        wgu[:, tf:] = wu_ref[...].astype(BF16)
        wdb[...] = wd_ref[...].astype(BF16)

        def block(r0, m):
            gu = _dot(xb[pl.ds(r0, m), :], wgu[...])
            gate, up = gu[:, :tf], gu[:, tf:]
            a = (gate * jax.nn.sigmoid(gate) * up).astype(BF16)
            acc[pl.ds(r0, m), :] += _dot(a, wdb[...])

        per_big = MOE_BLK_BIG // MOE_SUB
        per_blk = MOE_BLK // MOE_SUB
        nbig = nsub // per_big
        rem_big = nsub - nbig * per_big
        has_blk = rem_big >= per_blk
        rem = rem_big - jnp.where(has_blk, per_blk, 0)
        r_blk = pl.multiple_of(nbig * MOE_BLK_BIG, MOE_BLK)
        r_tail = pl.multiple_of(r_blk + jnp.where(has_blk, MOE_BLK, 0), MOE_SUB)

        def body(blk, carry):
            block(pl.multiple_of(blk * MOE_BLK_BIG, MOE_BLK_BIG), MOE_BLK_BIG)
            return carry
        lax.fori_loop(0, nbig, body, 0)
        pl.when(has_blk)(lambda: block(r_blk, MOE_BLK))
        for k in range(1, per_blk):
            @pl.when(rem == k)
            def _(k=k):
                block(r_tail, k * MOE_SUB)

    @pl.when(j == nj - 1)
    def _():
        out_copy().start()

        @pl.when(s == ns - 1)
        def _():
            out_copy().wait()


def _moe_experts(h_a, h_b, te, tr, tsp, src, wg, wu, wd, n_tiles):
    d = h_a.shape[1]
    assert h_a.shape[0] >= MOE_SUB
    e, _, ff = wg.shape
    tf = min(MOE_FF_TILE, ff)
    assert ff % tf == 0 and MOE_TILE_ROWS % MOE_BLK == 0 and MOE_BLK % MOE_SUB == 0 and MOE_BLK_BIG == 2 * MOE_BLK
    nj = ff // tf
    r = MOE_TILE_ROWS

    def jj(s, j, tr_ref):
        return jnp.where(tr_ref[s] > 0, j, nj - 1)

    grid_spec = pltpu.PrefetchScalarGridSpec(
        num_scalar_prefetch=4,
        grid=(n_tiles, nj),
        in_specs=[pl.BlockSpec(memory_space=pl.ANY), pl.BlockSpec(memory_space=pl.ANY),
                  pl.BlockSpec((None, d, tf), lambda s, j, te_r, tr_r, *_: (te_r[s], 0, jj(s, j, tr_r))),
                  pl.BlockSpec((None, d, tf), lambda s, j, te_r, tr_r, *_: (te_r[s], 0, jj(s, j, tr_r))),
                  pl.BlockSpec((None, tf, d), lambda s, j, te_r, tr_r, *_: (te_r[s], jj(s, j, tr_r), 0))],
        out_specs=pl.BlockSpec(memory_space=pl.ANY),
        scratch_shapes=[pltpu.VMEM((r, d), BF16), pltpu.VMEM((r, d), F32), pltpu.VMEM((2, MOE_SUB, d), F32),
                        pltpu.VMEM((d, 2 * tf), BF16), pltpu.VMEM((tf, d), BF16),
                        pltpu.SemaphoreType.DMA((2,)), pltpu.SemaphoreType.DMA(())],
    )
    return pl.pallas_call(
        _moe_kernel,
        grid_spec=grid_spec,
        out_shape=jax.ShapeDtypeStruct((n_tiles * r, d), F32),
        compiler_params=_cparams(("arbitrary", "arbitrary")),
        name="moe_experts",
    )(te, tr, tsp, src, h_a, h_b, wg, wu, wd)


def _combine_kernel(s1_ref, s2_ref, x_ref, g2_ref, r_ref, gf_ref, e_hbm, o_ref, st1, st2, sem):
    nb, tt, d = x_ref.shape
    tm = nb * tt
    step = pl.program_id(0) * pl.num_programs(1) + pl.program_id(1)
    nsteps = pl.num_programs(0) * pl.num_programs(1)
    slot = step % 2

    def waits(sl):
        pltpu.make_async_copy(e_hbm.at[pl.ds(0, tm), :], st1.at[sl], sem.at[0, sl]).wait()
        pltpu.make_async_copy(e_hbm.at[pl.ds(0, tm), :], st2.at[sl], sem.at[1, sl]).wait()

    def start_row(base, sl, r):
        _row_gather_copy(e_hbm, s1_ref[base + r], st1.at[sl], r, sem.at[0, sl]).start()
        _row_gather_copy(e_hbm, s2_ref[base + r], st2.at[sl], r, sem.at[1, sl]).start()

    @pl.when(step == 0)
    def _():
        def body(r, carry):
            start_row(0, 0, r)
            return carry
        lax.fori_loop(0, tm, body, 0, unroll=GATHER_UNROLL)

    waits(slot)
    nxt = jnp.minimum(step + 1, nsteps - 1) * tm
    for r in range(tm):
        start_row(nxt, 1 - slot, r)

    f = r_ref[:, 2:3] * st1[slot] + r_ref[:, 3:4] * st2[slot]
    x2 = x_ref[...] + g2_ref[...] * f.reshape(nb, tt, d)
    rs = lax.rsqrt(jnp.mean(x2 * x2, axis=-1, keepdims=True) + EPS)
    o_ref[...] = (x2 * rs) * gf_ref[...]

    @pl.when(step == nsteps - 1)
    def _():
        waits(1 - slot)


def _combine(x, mod, l, route, slot1, slot2, e_out, gf):
    b, t, d = x.shape
    nb, tt = _tile(b, t, ROWS_COMBINE)
    tm = nb * tt
    nt = t // tt
    grid_spec = pltpu.PrefetchScalarGridSpec(
        num_scalar_prefetch=2,
        grid=(b // nb, nt),
        in_specs=[pl.BlockSpec((nb, tt, d), lambda bi, i, a, c: (bi, i, 0)),
                  _mod_spec(nb, d, 5, l),
                  pl.BlockSpec((tm, ROUTE_LANES), lambda bi, i, a, c: (bi * nt + i, 0)),
                  pl.BlockSpec((1, d), lambda bi, i, a, c: (0, 0)),
                  pl.BlockSpec(memory_space=pl.ANY)],
        out_specs=pl.BlockSpec((nb, tt, d), lambda bi, i, a, c: (bi, i, 0)),
        scratch_shapes=[pltpu.VMEM((2, tm, d), F32), pltpu.VMEM((2, tm, d), F32), pltpu.SemaphoreType.DMA((2, 2))],
    )
    return pl.pallas_call(
        _combine_kernel,
        grid_spec=grid_spec,
        out_shape=jax.ShapeDtypeStruct((b, t, d), F32),
        compiler_params=_cparams(("arbitrary", "arbitrary")),
        name="moe_combine_norm",
    )(slot1, slot2, x, mod, route, gf, e_out)


def _lookup(table, idx):
    out = jnp.zeros_like(idx)
    for e in range(N_EXPERTS):
        out = jnp.where(idx == e, table[e], out)
    return out


def _floor_div(a, b):
    q = jnp.floor(a.astype(F32) / b.astype(F32)).astype(I32)
    q = jnp.where((q + 1) * b <= a, q + 1, q)
    return jnp.where(q * b > a, q - 1, q)


def _moe_layout(route_t, counts_f, counts_a_f, n_a):
    n = route_t.shape[1]
    r, sub = MOE_TILE_ROWS, MOE_SUB
    n_tiles = -(-TOP_K * n // r) + N_EXPERTS
    i1 = route_t[0].astype(I32)
    i2 = route_t[1].astype(I32)
    r1 = route_t[4].astype(I32)
    r2 = route_t[5].astype(I32)
    counts = counts_f[0, :N_EXPERTS].astype(I32)
    nt = (counts + r - 1) // r
    ntc = jnp.maximum(nt, 1)
    sz = jnp.maximum(((_floor_div(counts + ntc - 1, ntc) + sub - 1) // sub) * sub, sub)
    cum = jnp.cumsum(nt)
    tstart = cum - nt
    total = cum[-1]
    s_idx = jnp.arange(n_tiles, dtype=I32)
    te_raw = jnp.minimum(jnp.sum((s_idx[:, None] >= cum[None, :]).astype(I32), axis=1), N_EXPERTS - 1)
    used = s_idx < total
    last_e = jnp.max(jnp.where(used, te_raw, 0))
    te = jnp.where(used, te_raw, last_e)
    sz_t = _lookup(sz, te)
    k_in = s_idx - _lookup(tstart, te)
    tr = jnp.where(used, jnp.clip(_lookup(counts, te) - k_in * sz_t, 0, sz_t), 0).astype(I32)
    counts_a = counts_a_f[0, :N_EXPERTS].astype(I32)
    tsp = jnp.clip(_lookup(counts_a, te) - k_in * sz_t, 0, tr).astype(I32)

    def slot(ei, ri):
        sz_i = _lookup(sz, ei)
        k = _floor_div(ri, sz_i)
        return (_lookup(tstart, ei) + k) * r + (ri - k * sz_i)
    slot1, slot2 = slot(i1, r1), slot(i2, r2)
    tok = jnp.arange(n, dtype=I32)
    tok = jnp.where(tok < n_a, tok, tok - n_a)
    src = jnp.zeros((n_tiles * r,), I32).at[jnp.concatenate([slot1, slot2])].set(jnp.concatenate([tok, tok]))
    return te, tr, tsp, src, slot1, slot2, n_tiles


def kernel(x_prompt, x_sample, c_prompt, c_sample, state_pool, state_ssm_re, state_ssm_im, w_ada, b_ada, norm_mix, norm_ffn, w_in, w_pool, pool_scale, ssm_lam_re, ssm_lam_im, ssm_log_step, ssm_b_re, ssm_b_im, ssm_c_re, ssm_c_im, ssm_d, ssm_w_glu, w_out, ffn_w_gate, ffn_w_up, ffn_w_down, moe_w_router, moe_b_router, moe_w_gate, moe_w_up, moe_w_down, norm_final):
    bp, tp, d = x_prompt.shape
    bs, ts, _ = x_sample.shape
    depth = w_ada.shape[0]
    c_pool = w_pool.shape[1] * w_pool.shape[2]
    assert tp % CHUNK == 0 and ts <= CHUNK

    m_rows = -(-(bp + bs) // SUBLANES_V7X) * SUBLANES_V7X
    c_all = jnp.concatenate([c_prompt, c_sample, jnp.zeros((m_rows - bp - bs, d), F32)], axis=0)
    mod_all = _ada(c_all, w_ada, b_ada)

    mod_p = mod_all[:, :bp].reshape(depth, bp, 1, 6 * d)
    mod_s = mod_all[:, bp:bp + bs].reshape(depth, bs, 1, 6 * d)
    w_in_b, w_out_b = w_in.astype(BF16), w_out.astype(BF16)
    wp_b, wglu_b = w_pool.astype(BF16), ssm_w_glu.astype(BF16)
    ps = pool_scale.reshape(depth, 1, c_pool).astype(F32)
    dsk = ssm_d.reshape(depth, 1, d - c_pool).astype(F32)
    k_t, b_t, c_t, p16, q16, p8b, q8b = jax.vmap(lambda *a: _s5_params(*a, ts, bs))(
        ssm_lam_re, ssm_lam_im, ssm_log_step, ssm_b_re, ssm_b_im, ssm_c_re, ssm_c_im)
    x0_t = jnp.concatenate([state_ssm_re, state_ssm_im], axis=-1).astype(F32).transpose(0, 2, 3, 1)
    st_p = jnp.zeros((1, bp, POOL_BUF, c_pool), F32)
    st_s = state_pool.astype(F32)

    xp, xs = x_prompt, x_sample
    pool_p, re_p, im_p, pool_s, re_s, im_s = [], [], [], [], [], []
    y_p = y_s = None
    for l in range(depth):
        gm = norm_mix[l].reshape(1, d)
        gn = norm_ffn[l].reshape(1, d)

        up_all = _proj_in(xp, mod_p, l, gm, w_in_b)
        us_all = _proj_in(xs, mod_s, l, gm, w_in_b)
        y_ssm_p, xend = _s5_prompt(up_all, l, k_t, b_t, c_t, p16, q16, c_pool)
        ys2, x1_t = _s5_sample(us_all.reshape(bs * ts, d), l, x0_t, k_t, b_t, c_t, p8b, q8b, ts, c_pool)
        y_ssm_s = ys2.reshape(bs, ts, d - c_pool)
        ffn_f32 = (ffn_w_gate[l // 2], ffn_w_up[l // 2], ffn_w_down[l // 2]) if l % 2 == 0 else ()
        xp, new_pool_p, ffn_bf16 = _mix(xp, up_all, y_ssm_p, st_p, 0, mod_p, l, wp_b, ps, dsk, wglu_b, w_out_b, 0,
                                        cast=ffn_f32)
        xs, new_pool_s, _ = _mix(xs, us_all, y_ssm_s, st_s, l, mod_s, l, wp_b, ps, dsk, wglu_b, w_out_b, PAST_LEN)
        pool_p.append(new_pool_p)
        pool_s.append(new_pool_s)
        re_p.append(xend[..., :SSM_P])
        im_p.append(xend[..., SSM_P:])
        re_s.append(x1_t[:, :SSM_P, :].transpose(2, 0, 1))
        im_s.append(x1_t[:, SSM_P:, :].transpose(2, 0, 1))

        i = l // 2
        if l % 2 == 0:
            wg, wu, wd = ffn_bf16
            xp = _ffn(xp, mod_p, l, gn, wg, wu, wd)
            xs = _ffn(xs, mod_s, l, gn, wg, wu, wd)
        else:
            ne = moe_w_router.shape[-1]
            wr32 = jnp.pad(moe_w_router[i].astype(F32), ((0, 0), (0, ROUTE_LANES - ne)))
            wr_hi = wr32.astype(BF16)
            wr = jnp.stack([wr_hi, (wr32 - wr_hi.astype(F32)).astype(BF16)])
            br = jnp.pad(moe_b_router[i].astype(F32), (0, ROUTE_LANES - ne)).reshape(1, ROUTE_LANES)
            zero_counts = jnp.zeros((1, ROUTE_LANES), F32)
            h_p, route_p, rt_p, cnt_p = _route(xp, mod_p, l, gn, wr, br, zero_counts)
            h_s, route_s, rt_s, cnt_s = _route(xs, mod_s, l, gn, wr, br, cnt_p)
            n_p = bp * tp
            te, tr, tsp, src, slot1, slot2, n_tiles = _moe_layout(jnp.concatenate([rt_p, rt_s], axis=1), cnt_s, cnt_p,
                                                                  n_p)
            e_out = _moe_experts(h_p, h_s, te, tr, tsp, src, moe_w_gate[i], moe_w_up[i], moe_w_down[i], n_tiles)
            last = l == depth - 1
            assert last, "the combine kernel also applies the final norm"
            gf = norm_final.reshape(1, d)
            y_p = _combine(xp, mod_p, l, route_p, slot1[:n_p], slot2[:n_p], e_out, gf)
            y_s = _combine(xs, mod_s, l, route_s, slot1[n_p:], slot2[n_p:], e_out, gf)

    return (y_p, y_s, jnp.stack(pool_p), jnp.stack(re_p), jnp.stack(im_p),
            jnp.stack(pool_s), jnp.stack(re_s), jnp.stack(im_s))
```

```python
import functools
import math

import jax
import jax.numpy as jnp
import numpy as np
from jax import lax
from jax.experimental import pallas as pl
from jax.experimental.pallas import tpu as pltpu

F32 = jnp.float32
BF16 = jnp.bfloat16
I32 = jnp.int32

EPS = 1e-6
POOL_WINDOWS = (2, 4, 8, 16)
POOL_BUF = max(POOL_WINDOWS) - 1
SUBLANES_V7X = 8
BF16_SUBLANES_V7X = 16
LANES_V7X = 128
HALO = -(-POOL_BUF // SUBLANES_V7X) * SUBLANES_V7X
SSM_H = 16
SSM_P = 64
CHUNK = 16
CW = CHUNK * SSM_H
SW = 2 * SSM_P
N_EXPERTS = 8
TOP_K = 2
PAST_LEN = 16384

VMEM_LIMIT_V7X = 56 * 1024 * 1024
ROWS_IN = 512
ROWS_MIX = 256
ROWS_FFN = 512
FF_TILE = 512
ROWS_ROUTE = 512
MOE_TILE_ROWS = 2560
MOE_SUB = 128
MOE_BLK_BIG = 1024
MOE_BLK = 512
GATHER_UNROLL = 8
MOE_FF_TILE = 256
ROWS_COMBINE = 256
ADA_TN = 1024


def _cparams(sem):
    return pltpu.CompilerParams(dimension_semantics=sem, vmem_limit_bytes=VMEM_LIMIT_V7X)


def _tile(b, t, rows):
    if t >= rows:
        assert t % rows == 0
        return 1, rows
    nb = max(1, min(b, rows // t))
    assert b % nb == 0
    return nb, t


def _modnorm(x, g, s, sh):
    r = lax.rsqrt(jnp.mean(x * x, axis=-1, keepdims=True) + EPS)
    return (x * r) * g * (1.0 + s) + sh


def _dot(a, b):
    return jnp.dot(a, b, preferred_element_type=F32)


def _mod_spec(nb, d, k, l):
    return pl.BlockSpec((None, nb, 1, d), lambda b, i, *_: (l, b, 0, k))


def _layer_spec(shape, l):
    return pl.BlockSpec((None,) + tuple(shape), lambda *_: (l,) + (0,) * len(shape))


def _ada_kernel(c_ref, w_ref, b_ref, o_ref):
    c = c_ref[...]
    sc = c * jax.nn.sigmoid(c)
    o_ref[0] = _dot(sc.astype(BF16), w_ref[0].astype(BF16)) + b_ref[0]


def _ada(c_all, w_ada, b_ada):
    m, d = c_all.shape
    depth, _, n6 = w_ada.shape
    tn = math.gcd(ADA_TN, n6)
    return pl.pallas_call(
        _ada_kernel,
        grid=(depth, n6 // tn),
        in_specs=[pl.BlockSpec((m, d), lambda l, n: (0, 0)),
                  pl.BlockSpec((1, d, tn), lambda l, n: (l, 0, n)),
                  pl.BlockSpec((1, 1, tn), lambda l, n: (l, 0, n))],
        out_specs=pl.BlockSpec((1, m, tn), lambda l, n: (l, 0, n)),
        out_shape=jax.ShapeDtypeStruct((depth, m, n6), F32),
        compiler_params=_cparams(("arbitrary", "arbitrary")),
        name="adaln_mod",
    )(c_all, w_ada, b_ada.reshape(depth, 1, n6))


def _in_kernel(x_ref, g_ref, s_ref, sh_ref, w_ref, u_ref):
    nb, tt, d = x_ref.shape
    h = _modnorm(x_ref[...], g_ref[...], s_ref[...], sh_ref[...])
    u = _dot(h.reshape(nb * tt, d).astype(BF16), w_ref[...])
    u_ref[...] = u.reshape(nb, tt, u_ref.shape[-1])


def _proj_in(x, mod, l, g, w_bf16):
    b, t, d = x.shape
    n = w_bf16.shape[-1]
    nb, tt = _tile(b, t, ROWS_IN)
    return pl.pallas_call(
        _in_kernel,
        grid=(b // nb, t // tt),
        in_specs=[pl.BlockSpec((nb, tt, d), lambda bi, i: (bi, i, 0)),
                  pl.BlockSpec((1, d), lambda bi, i: (0, 0)),
                  _mod_spec(nb, d, 1, l), _mod_spec(nb, d, 0, l),
                  _layer_spec((d, n), l)],
        out_specs=pl.BlockSpec((nb, tt, n), lambda bi, i: (bi, i, 0)),
        out_shape=jax.ShapeDtypeStruct((b, t, n), F32),
        compiler_params=_cparams(("arbitrary", "arbitrary")),
        name="norm_proj_in",
    )(x, g, mod, mod, w_bf16)


GROUPS_PER_LANE_TILE = LANES_V7X // SSM_H


def _toeplitz_placement():
    toe = np.zeros((CHUNK, CW, CW), np.float32)
    for s in range(CHUNK):
        for h in range(SSM_H):
            for t in range(s, CHUNK):
                toe[t, (t - s) * SSM_H + h, s * SSM_H + h] = 1.0
    return jnp.asarray(toe)


def _s5_params(lam_re, lam_im, log_step, b_re, b_im, c_re, c_im, ts, bs):
    hp = lax.Precision.HIGHEST
    toe = _toeplitz_placement()
    lr, li = lam_re.astype(F32), lam_im.astype(F32)
    step = jnp.exp(log_step.astype(F32))[:, None]
    mag = jnp.exp(lr * step)
    a_re = mag * jnp.cos(li * step)
    a_im = mag * jnp.sin(li * step)
    den = lr * lr + li * li
    nr = a_re - 1.0
    f_re = (nr * lr + a_im * li) / den
    f_im = (a_im * lr - nr * li) / den
    br, bi = b_re.astype(F32), b_im.astype(F32)
    bb_re = f_re[..., None] * br - f_im[..., None] * bi
    bb_im = f_re[..., None] * bi + f_im[..., None] * br
    cr, ci = c_re.astype(F32), c_im.astype(F32)
    g = lr.shape[0]

    pw_re, pw_im = [jnp.ones_like(a_re)], [jnp.zeros_like(a_im)]
    for _ in range(CHUNK):
        pr, pi = pw_re[-1], pw_im[-1]
        pw_re.append(pr * a_re - pi * a_im)
        pw_im.append(pr * a_im + pi * a_re)
    pw_re, pw_im = jnp.stack(pw_re), jnp.stack(pw_im)

    bbt_re, bbt_im = bb_re.transpose(0, 2, 1), bb_im.transpose(0, 2, 1)
    w_re = pw_re[:CHUNK, :, None, :] * bbt_re - pw_im[:CHUNK, :, None, :] * bbt_im
    w_im = pw_re[:CHUNK, :, None, :] * bbt_im + pw_im[:CHUNK, :, None, :] * bbt_re
    m = (jnp.einsum('gop,lgip->goli', cr, w_re, precision=hp)
         - jnp.einsum('gop,lgip->goli', ci, w_im, precision=hp)).reshape(g, SSM_H, CW)
    k_t = jnp.einsum('gok,tkn->gton', m.astype(BF16), toe.astype(BF16),
                     preferred_element_type=F32).reshape(g, CW, CW)

    rev_re, rev_im = pw_re[:CHUNK][::-1], pw_im[:CHUNK][::-1]
    ar = jnp.repeat(rev_re.transpose(1, 2, 0), SSM_H, axis=-1)
    ai = jnp.repeat(rev_im.transpose(1, 2, 0), SSM_H, axis=-1)
    xr = jnp.tile(bb_re, (1, 1, CHUNK))
    xi = jnp.tile(bb_im, (1, 1, CHUNK))
    bt_re = ar * xr - ai * xi
    bt_im = ar * xi + ai * xr
    b_t = jnp.concatenate([bt_re, bt_im, bt_im, bt_re], axis=1)

    qr = pw_re[1:].transpose(1, 0, 2)[:, :, None, :]
    qi = pw_im[1:].transpose(1, 0, 2)[:, :, None, :]
    cm_re = cr[:, None] * qr - ci[:, None] * qi
    cm_im = -(cr[:, None] * qi + ci[:, None] * qr)
    c_t = jnp.concatenate([cm_re, cm_im], axis=-1).reshape(g, CW, SW)

    def coef(k):
        pr, pi = pw_re[k], pw_im[k]
        return (jnp.concatenate([pr, pr, pr, pr], axis=-1),
                jnp.concatenate([-pi, pi, pi, -pi], axis=-1))
    p16, q16 = coef(CHUNK)
    p8, q8 = coef(ts)
    p8b = jnp.broadcast_to(p8[:, :SW, None], (g, SW, bs))
    q8b = jnp.broadcast_to(q8[:, :SW, None], (g, SW, bs))
    return k_t.astype(BF16), b_t.astype(BF16), c_t.astype(BF16), p16, q16, p8b, q8b


def _s5_prompt_kernel(us_ref, kt_ref, bt_ref, ct_ref, p_ref, q_ref, y_ref, xe_ref, ut, yt, scan_x, scan_xs, s_scr):
    gt = ut.shape[0]
    nchunk = ut.shape[-1]
    for t in range(CHUNK):
        slab = us_ref[0, pl.ds(t, nchunk, stride=CHUNK), :]
        ut[:, t] = slab.T.reshape(gt, SSM_H, nchunk).astype(BF16)
    for gi in range(gt):
        buz = _dot(bt_ref[gi], ut[gi].reshape(CW, nchunk))
        buz_t = buz.T
        scan_x[pl.ds(gi, nchunk, stride=gt), :] = buz_t[:, :SW]
        scan_xs[pl.ds(gi, nchunk, stride=gt), :] = buz_t[:, SW:]
    px, pxs = p_ref[:, :SW], p_ref[:, SW:]
    qx, qxs = q_ref[:, :SW], q_ref[:, SW:]
    x = jnp.zeros((gt, SW), F32)
    xs = jnp.zeros((gt, SW), F32)
    for c in range(nchunk):
        lo, hi = c * gt, (c + 1) * gt
        s_scr[lo:hi, :] = x
        x, xs = (px * x + qx * xs + scan_x[lo:hi, :],
                 pxs * xs + qxs * x + scan_xs[lo:hi, :])
    xe_ref[0] = x
    for gi in range(gt):
        st = s_scr[pl.ds(gi, nchunk, stride=gt), :].T.astype(BF16)
        y_t = _dot(kt_ref[gi], ut[gi].reshape(CW, nchunk)) + _dot(ct_ref[gi], st)
        yt[gi] = y_t.reshape(CHUNK, SSM_H, nchunk)
    for t in range(CHUNK):
        y_ref[0, pl.ds(t, nchunk, stride=CHUNK), :] = yt[:, t].reshape(gt * SSM_H, nchunk).T


def _s5_prompt(u, l, k_t, b_t, c_t, p16, q16, c_off):
    b, t, _ = u.shape
    g = k_t.shape[1]
    gt = GROUPS_PER_LANE_TILE
    lt = gt * SSM_H
    nchunk = t // CHUNK
    off = c_off // lt

    def wspec(r, c):
        return pl.BlockSpec((None, gt, r, c), lambda bi, j: (l, j, 0, 0))
    return pl.pallas_call(
        _s5_prompt_kernel,
        grid=(b, g // gt),
        in_specs=[pl.BlockSpec((1, t, lt), lambda bi, j: (bi, 0, off + j)),
                  wspec(CW, CW), wspec(2 * SW, CW), wspec(CW, SW),
                  pl.BlockSpec((None, gt, 2 * SW), lambda bi, j: (l, j, 0)),
                  pl.BlockSpec((None, gt, 2 * SW), lambda bi, j: (l, j, 0))],
        out_specs=[pl.BlockSpec((1, t, lt), lambda bi, j: (bi, 0, j)),
                   pl.BlockSpec((1, gt, SW), lambda bi, j: (bi, j, 0))],
        out_shape=[jax.ShapeDtypeStruct((b, t, g * SSM_H), F32), jax.ShapeDtypeStruct((b, g, SW), F32)],
        scratch_shapes=[pltpu.VMEM((gt, CHUNK, SSM_H, nchunk), BF16), pltpu.VMEM((gt, CHUNK, SSM_H, nchunk), F32),
                        pltpu.VMEM((nchunk * gt, SW), F32), pltpu.VMEM((nchunk * gt, SW), F32),
                        pltpu.VMEM((nchunk * gt, SW), F32)],
        compiler_params=_cparams(("arbitrary", "arbitrary")),
        name="s5_prompt",
    )(u, k_t, b_t, c_t, p16, q16)


def _s5_sample_kernel(us_ref, x0_ref, kt_ref, bt_ref, ct_ref, p_ref, q_ref, y_ref, x1_ref, ut, yt, *, ts):
    gt = ut.shape[0]
    bs = ut.shape[-1]
    w = ts * SSM_H
    for t in range(ts):
        slab = us_ref[pl.ds(t, bs, stride=ts), :]
        ut[:, t] = slab.T.reshape(gt, SSM_H, bs).astype(BF16)
    for gi in range(gt):
        ug = ut[gi].reshape(w, bs)
        x0 = x0_ref[gi]
        x0s = jnp.concatenate([x0[SSM_P:], x0[:SSM_P]], axis=0)
        y_t = _dot(kt_ref[gi, :w, :w], ug) + _dot(ct_ref[gi, :w, :], x0.astype(BF16))
        x1_ref[gi] = p_ref[gi] * x0 + q_ref[gi] * x0s + _dot(bt_ref[gi, :SW, CW - w:], ug)
        yt[gi] = y_t.reshape(ts, SSM_H, bs)
    for t in range(ts):
        y_ref[pl.ds(t, bs, stride=ts), :] = yt[:, t].reshape(gt * SSM_H, bs).T


def _s5_sample(u2, l, x0_t, k_t, b_t, c_t, p8b, q8b, ts, c_off):
    rows, _ = u2.shape
    _, g, _, bs = x0_t.shape
    gt = GROUPS_PER_LANE_TILE
    lt = gt * SSM_H
    off = c_off // lt

    def wspec(r, c):
        return pl.BlockSpec((None, gt, r, c), lambda j: (l, j, 0, 0))
    return pl.pallas_call(
        functools.partial(_s5_sample_kernel, ts=ts),
        grid=(g // gt,),
        in_specs=[pl.BlockSpec((rows, lt), lambda j: (0, off + j)),
                  wspec(SW, bs), wspec(CW, CW), wspec(2 * SW, CW), wspec(CW, SW), wspec(SW, bs), wspec(SW, bs)],
        out_specs=[pl.BlockSpec((rows, lt), lambda j: (0, j)), pl.BlockSpec((gt, SW, bs), lambda j: (j, 0, 0))],
        out_shape=[jax.ShapeDtypeStruct((rows, g * SSM_H), F32), jax.ShapeDtypeStruct((g, SW, bs), F32)],
        scratch_shapes=[pltpu.VMEM((gt, ts, SSM_H, bs), BF16), pltpu.VMEM((gt, ts, SSM_H, bs), F32)],
        compiler_params=_cparams(("arbitrary",)),
        name="s5_sample",
    )(u2, x0_t, k_t, b_t, c_t, p8b, q8b)


def _mix_kernel(x_ref, up_ref, us_ref, ys_ref, st_ref, g1_ref, wp_ref, ps_ref, dsk_ref, wglu_ref,
                wout_ref, *rest, start_pos, carry, ncast):
    cast_in, (o_ref, np_ref), cast_out, ext = rest[:ncast], rest[ncast:ncast + 2], rest[ncast + 2:-1], rest[-1]
    for src, dst in zip(cast_in, cast_out):
        dst[...] = src[...].astype(BF16)
    nb, tt, d = x_ref.shape
    c = up_ref.shape[-1]
    pg = c // len(POOL_WINDOWS)
    rows = nb * tt
    i = pl.program_id(1)

    @pl.when(i == 0)
    def _():
        ext[:, HALO - POOL_BUF:HALO, :] = st_ref[...]
    ext[:, HALO:HALO + tt, :] = up_ref[...]
    np_ref[...] = ext[:, HALO + tt - POOL_BUF:HALO + tt, :]

    pos = start_pos + i * tt + lax.broadcasted_iota(I32, (1, tt, 1), 1)
    outs = []
    for gidx, w in enumerate(POOL_WINDOWS):
        c0 = gidx * pg
        cur = ext[:, HALO:HALO + tt, c0:c0 + pg]
        s = cur
        for k in range(1, w):
            s = s + ext[:, HALO - k:HALO - k + tt, c0:c0 + pg]
        cnt = jnp.minimum(pos + 1, w).astype(F32)
        dd = s / cnt - cur
        yg = _dot(dd.reshape(rows, pg).astype(BF16), wp_ref[gidx])
        outs.append(yg * ps_ref[:, c0:c0 + pg])
    y_pool = jnp.concatenate(outs, axis=-1)

    if carry:
        ext[:, 0:HALO, :] = ext[:, tt:tt + HALO, :]

    us = us_ref[...].reshape(rows, c)
    yf = ys_ref[...].reshape(rows, c) + dsk_ref[...] * us
    gl = jax.nn.gelu(yf, approximate=True)
    o = gl * jax.nn.sigmoid(_dot(gl.astype(BF16), wglu_ref[...]))
    mo = _dot(y_pool.astype(BF16), wout_ref[:c, :]) + _dot(o.astype(BF16), wout_ref[c:, :])
    o_ref[...] = x_ref[...] + g1_ref[...] * mo.reshape(nb, tt, d)


def _mix(x, u, y_ssm, pool_state, l_state, mod, l, wp, ps, dsk, wglu, wout, start_pos, cast=()):
    b, t, d = x.shape
    c = y_ssm.shape[-1]
    nb, tt = _tile(b, t, ROWS_MIX)
    nt = t // tt
    assert nt == 1 or tt >= HALO
    nsteps = (b // nb) * nt
    cast_specs = []
    for a in cast:
        blk = -(-(-(-a.shape[0] // nsteps)) // BF16_SUBLANES_V7X) * BF16_SUBLANES_V7X
        last = -(-a.shape[0] // blk) - 1
        cast_specs.append(pl.BlockSpec((blk, a.shape[1]), lambda bi, i, last=last: (jnp.minimum(bi * nt + i, last), 0)))
    outs = pl.pallas_call(
        functools.partial(_mix_kernel, start_pos=start_pos, carry=nt > 1, ncast=len(cast)),
        grid=(b // nb, nt),
        in_specs=[pl.BlockSpec((nb, tt, d), lambda bi, i: (bi, i, 0)),
                  pl.BlockSpec((nb, tt, c), lambda bi, i: (bi, i, 0)),
                  pl.BlockSpec((nb, tt, c), lambda bi, i: (bi, i, 1)),
                  pl.BlockSpec((nb, tt, c), lambda bi, i: (bi, i, 0)),
                  pl.BlockSpec((None, nb, POOL_BUF, c), lambda bi, i: (l_state, bi, 0, 0)),
                  _mod_spec(nb, d, 2, l),
                  _layer_spec(wp.shape[1:], l), _layer_spec((1, c), l), _layer_spec((1, c), l),
                  _layer_spec((c, c), l), _layer_spec((d, d), l)] + cast_specs,
        out_specs=[pl.BlockSpec((nb, tt, d), lambda bi, i: (bi, i, 0)),
                   pl.BlockSpec((nb, POOL_BUF, c), lambda bi, i: (bi, 0, 0))] + cast_specs,
        out_shape=[jax.ShapeDtypeStruct((b, t, d), F32), jax.ShapeDtypeStruct((b, POOL_BUF, c), F32)]
        + [jax.ShapeDtypeStruct(a.shape, BF16) for a in cast],
        scratch_shapes=[pltpu.VMEM((nb, HALO + tt, c), F32)],
        compiler_params=_cparams(("arbitrary", "arbitrary")),
        name="pool_glu_proj_out",
    )(x, u, u, y_ssm, pool_state, mod, wp, ps, dsk, wglu, wout, *cast)
    return outs[0], outs[1], tuple(outs[2:])


def _ffn_kernel(x_ref, g_ref, s_ref, sh_ref, g2_ref, wg_ref, wu_ref, wd_ref, o_ref, hb, acc, *, last_valid):
    nb, tt, d = x_ref.shape
    tf = wg_ref.shape[1]
    j = pl.program_id(2)
    nj = pl.num_programs(2)

    @pl.when(j == 0)
    def _():
        h = _modnorm(x_ref[...], g_ref[...], s_ref[...], sh_ref[...])
        hb[...] = h.reshape(nb * tt, d).astype(BF16)
        acc[...] = jnp.zeros_like(acc)

    def step(valid):
        h = hb[...]
        gate = _dot(h, wg_ref[...])
        up = _dot(h, wu_ref[...])
        a = gate * jax.nn.sigmoid(gate) * up
        wd = wd_ref[...]
        if valid < tf:
            a = jnp.where(lax.broadcasted_iota(I32, (1, tf), 1) < valid, a, 0.0)
            wd = jnp.where(lax.broadcasted_iota(I32, (tf, 1), 0) < valid, wd, jnp.zeros_like(wd))
        acc[...] += _dot(a.astype(BF16), wd)

    if last_valid == tf:
        step(tf)
    else:
        pl.when(j < nj - 1)(lambda: step(tf))
        pl.when(j == nj - 1)(lambda: step(last_valid))

    @pl.when(j == nj - 1)
    def _():
        o_ref[...] = x_ref[...] + g2_ref[...] * acc[...].reshape(nb, tt, d)


def _ffn(x, mod, l, g, wg, wu, wd):
    b, t, d = x.shape
    ff = wg.shape[1]
    nb, tt = _tile(b, t, ROWS_FFN)
    tf = min(FF_TILE, ff)
    nj = -(-ff // tf)
    return pl.pallas_call(
        functools.partial(_ffn_kernel, last_valid=ff - (nj - 1) * tf),
        grid=(b // nb, t // tt, nj),
        in_specs=[pl.BlockSpec((nb, tt, d), lambda bi, i, j: (bi, i, 0)),
                  pl.BlockSpec((1, d), lambda bi, i, j: (0, 0)),
                  _mod_spec(nb, d, 4, l), _mod_spec(nb, d, 3, l), _mod_spec(nb, d, 5, l),
                  pl.BlockSpec((d, tf), lambda bi, i, j: (0, j)),
                  pl.BlockSpec((d, tf), lambda bi, i, j: (0, j)),
                  pl.BlockSpec((tf, d), lambda bi, i, j: (j, 0))],
        out_specs=pl.BlockSpec((nb, tt, d), lambda bi, i, j: (bi, i, 0)),
        out_shape=jax.ShapeDtypeStruct((b, t, d), F32),
        scratch_shapes=[pltpu.VMEM((nb * tt, d), BF16), pltpu.VMEM((nb * tt, d), F32)],
        compiler_params=_cparams(("arbitrary", "arbitrary", "arbitrary")),
        name="dense_swiglu",
    )(x, g, mod, mod, mod, wg, wu, wd)


ROUTE_LANES = LANES_V7X
ROUTE_ROWS = SUBLANES_V7X


def _route_kernel(x_ref, g_ref, s_ref, sh_ref, wr_ref, br_ref, cin_ref, h_ref, r_ref, rt_ref, cout_ref, carry):
    nb, tt, d = x_ref.shape
    tm = nb * tt
    first = jnp.logical_and(pl.program_id(0) == 0, pl.program_id(1) == 0)

    @pl.when(first)
    def _():
        carry[...] = cin_ref[...]

    h = _modnorm(x_ref[...], g_ref[...], s_ref[...], sh_ref[...]).reshape(tm, d)
    h_ref[...] = h
    h_hi = h.astype(BF16)
    h_lo = (h - h_hi.astype(F32)).astype(BF16)
    logits = _dot(h_hi, wr_ref[0]) + (_dot(h_hi, wr_ref[1]) + _dot(h_lo, wr_ref[0])) + br_ref[...]
    lane = lax.broadcasted_iota(I32, (tm, ROUTE_LANES), 1)
    neg = jnp.float32(-jnp.inf)
    l1 = jnp.where(lane < N_EXPERTS, logits, neg)
    m1 = jnp.max(l1, axis=-1, keepdims=True)
    i1 = jnp.min(jnp.where(l1 == m1, lane, ROUTE_LANES), axis=-1, keepdims=True)
    l2 = jnp.where(lane == i1, neg, l1)
    m2 = jnp.max(l2, axis=-1, keepdims=True)
    i2 = jnp.min(jnp.where(l2 == m2, lane, ROUTE_LANES), axis=-1, keepdims=True)
    e2 = jnp.exp(m2 - m1)
    den = 1.0 + e2
    p1 = 1.0 / den
    p2 = e2 / den

    onehot = jnp.logical_or(lane == i1, lane == i2).astype(F32)
    rr = lax.broadcasted_iota(I32, (tm, tm), 0)
    cc = lax.broadcasted_iota(I32, (tm, tm), 1)
    before = (cc < rr).astype(BF16)
    rank = _dot(before, onehot.astype(BF16)) + carry[...]
    r1 = jnp.sum(jnp.where(lane == i1, rank, 0.0), axis=-1, keepdims=True)
    r2 = jnp.sum(jnp.where(lane == i2, rank, 0.0), axis=-1, keepdims=True)
    carry[...] += jnp.sum(onehot, axis=0, keepdims=True)

    out = jnp.where(lane == 0, i1.astype(F32), 0.0)
    out = jnp.where(lane == 1, i2.astype(F32), out)
    out = jnp.where(lane == 2, p1, out)
    out = jnp.where(lane == 3, p2, out)
    out = jnp.where(lane == 4, r1, out)
    out = jnp.where(lane == 5, r2, out)
    r_ref[...] = out
    rt_ref[...] = out.T[:ROUTE_ROWS, :]
    cout_ref[...] = carry[...]


def _route(x, mod, l, g, wr_pad, br_pad, counts_in):
    b, t, d = x.shape
    nb, tt = _tile(b, t, ROWS_ROUTE)
    tm = nb * tt
    nt = t // tt
    const2 = lambda bi, i: (0, 0)
    return pl.pallas_call(
        _route_kernel,
        grid=(b // nb, nt),
        in_specs=[pl.BlockSpec((nb, tt, d), lambda bi, i: (bi, i, 0)),
                  pl.BlockSpec((1, d), const2),
                  _mod_spec(nb, d, 4, l), _mod_spec(nb, d, 3, l),
                  pl.BlockSpec((2, d, ROUTE_LANES), lambda bi, i: (0, 0, 0)),
                  pl.BlockSpec((1, ROUTE_LANES), const2),
                  pl.BlockSpec((1, ROUTE_LANES), const2)],
        out_specs=[pl.BlockSpec((tm, d), lambda bi, i: (bi * nt + i, 0)),
                   pl.BlockSpec((tm, ROUTE_LANES), lambda bi, i: (bi * nt + i, 0)),
                   pl.BlockSpec((ROUTE_ROWS, tm), lambda bi, i: (0, bi * nt + i)),
                   pl.BlockSpec((1, ROUTE_LANES), const2)],
        out_shape=[jax.ShapeDtypeStruct((b * t, d), F32),
                   jax.ShapeDtypeStruct((b * t, ROUTE_LANES), F32),
                   jax.ShapeDtypeStruct((ROUTE_ROWS, b * t), F32),
                   jax.ShapeDtypeStruct((1, ROUTE_LANES), F32)],
        scratch_shapes=[pltpu.VMEM((1, ROUTE_LANES), F32)],
        compiler_params=_cparams(("arbitrary", "arbitrary")),
        name="moe_route",
    )(x, g, mod, mod, wr_pad, br_pad, counts_in)


def _row_gather_copy(src_hbm, tok, dst, r, sem):
    return pltpu.make_async_copy(src_hbm.at[pl.ds(tok, 1), :], dst.at[pl.ds(r, 1), :], sem)


def _moe_kernel(te_ref, tr_ref, tsp_ref, src_ref, ha_hbm, hb_hbm, wg_ref, wu_ref, wd_ref, o_hbm,
                xb, acc, stage, wgu, wdb, gsem, osem, *, nj):
    q = pl.program_id(0)
    nq = pl.num_programs(0)
    qm = jnp.maximum(q - 1, 0)
    s = qm // nj
    j = qm - s * nj
    ns = (nq - 1) // nj
    rows = jnp.where(q >= 1, tr_ref[s], 0)
    nsub = (rows + MOE_SUB - 1) // MOE_SUB
    tile_rows = xb.shape[0]
    tf = wg_ref.shape[1]
    cur = (q + 1) % 2
    nxt = q % 2

    def out_copy():
        return pltpu.make_async_copy(acc, o_hbm.at[pl.ds(pl.multiple_of(s * tile_rows, MOE_SUB), tile_rows), :], osem)

    @pl.when(jnp.logical_and(q >= 1, j == 0))
    def _():
        @pl.when(rows > 0)
        def _gather():
            def issue(sub, slot):
                base = s * tile_rows + sub * MOE_SUB
                na = jnp.clip(tsp_ref[s] - sub * MOE_SUB, 0, MOE_SUB)

                def start_from(h_hbm):
                    def body(r, carry):
                        _row_gather_copy(h_hbm, src_ref[base + r], stage.at[slot], r, gsem.at[slot]).start()
                        return carry
                    return body

                @pl.when(na == MOE_SUB)
                def _():
                    lax.fori_loop(0, MOE_SUB, start_from(ha_hbm), 0, unroll=GATHER_UNROLL)

                @pl.when(na == 0)
                def _():
                    lax.fori_loop(0, MOE_SUB, start_from(hb_hbm), 0, unroll=GATHER_UNROLL)

                @pl.when(jnp.logical_and(na > 0, na < MOE_SUB))
                def _():
                    lax.fori_loop(0, na, start_from(ha_hbm), 0)
                    lax.fori_loop(na, MOE_SUB, start_from(hb_hbm), 0)

            issue(0, 0)

            def sub_body(sub, carry):
                slot = sub % 2

                @pl.when(sub + 1 < nsub)
                def _():
                    issue(sub + 1, 1 - slot)
                pltpu.make_async_copy(ha_hbm.at[pl.ds(0, MOE_SUB), :], stage.at[slot], gsem.at[slot]).wait()
                r0 = pl.multiple_of(sub * MOE_SUB, MOE_SUB)
                xb[pl.ds(r0, MOE_SUB), :] = stage[slot].astype(BF16)
                return carry
            lax.fori_loop(0, nsub, sub_body, 0)

        @pl.when(s > 0)
        def _():
            out_copy().wait()
        acc[...] = jnp.zeros_like(acc)

    half = wu_ref.shape[0] // 2

    def convert(piece):
        if piece == 0:
            wgu[nxt, :, :tf] = wg_ref[...].astype(BF16)
            wgu[nxt, :half, tf:] = wu_ref[:half, :].astype(BF16)
        else:
            wgu[nxt, half:, tf:] = wu_ref[half:, :].astype(BF16)
            wdb[nxt] = wd_ref[...].astype(BF16)

    def block(r0, m):
        gu = _dot(xb[pl.ds(r0, m), :], wgu[cur])
        gate, up = gu[:, :tf], gu[:, tf:]
        a = (gate * jax.nn.sigmoid(gate) * up).astype(BF16)
        acc[pl.ds(r0, m), :] += _dot(a, wdb[cur])

    per_big = MOE_BLK_BIG // MOE_SUB
    per_blk = MOE_BLK // MOE_SUB
    nbig = nsub // per_big
    rem_big = nsub - nbig * per_big
    has_blk = rem_big >= per_blk
    rem = rem_big - jnp.where(has_blk, per_blk, 0)
    r_blk = pl.multiple_of(nbig * MOE_BLK_BIG, MOE_BLK)
    r_tail = pl.multiple_of(r_blk + jnp.where(has_blk, MOE_BLK, 0), MOE_SUB)

    for piece in range(2):
        @pl.when(nbig > piece)
        def _(piece=piece):
            block(piece * MOE_BLK_BIG, MOE_BLK_BIG)
            convert(piece)

        @pl.when(nbig <= piece)
        def _(piece=piece):
            convert(piece)

    def body(blk, carry):
        block(pl.multiple_of(blk * MOE_BLK_BIG, MOE_BLK_BIG), MOE_BLK_BIG)
        return carry
    lax.fori_loop(2, jnp.maximum(nbig, 2), body, 0)
    pl.when(has_blk)(lambda: block(r_blk, MOE_BLK))
    for k in range(1, per_blk):
        @pl.when(rem == k)
        def _(k=k):
            block(r_tail, k * MOE_SUB)

    @pl.when(jnp.logical_and(q >= 1, j == nj - 1))
    def _():
        out_copy().start()

        @pl.when(s == ns - 1)
        def _():
            out_copy().wait()


def _moe_experts(h_a, h_b, te, tr, tsp, src, wg, wu, wd, n_tiles):
    d = h_a.shape[1]
    assert h_a.shape[0] >= MOE_SUB
    e, _, ff = wg.shape
    tf = min(MOE_FF_TILE, ff)
    assert ff % tf == 0 and MOE_TILE_ROWS % MOE_BLK == 0 and MOE_BLK % MOE_SUB == 0 and MOE_BLK_BIG == 2 * MOE_BLK
    assert MOE_TILE_ROWS >= 2 * MOE_BLK_BIG
    nj = ff // tf
    r = MOE_TILE_ROWS
    nq = n_tiles * nj + 1

    def wblock(q, te_r, tr_r):
        qc = jnp.minimum(q, nq - 2)
        sc = qc // nj
        jc = jnp.where(tr_r[sc] > 0, qc - sc * nj, nj - 1)
        return te_r[sc], jc

    def col_spec():
        return pl.BlockSpec((None, d, tf), lambda q, te_r, tr_r, *_: (wblock(q, te_r, tr_r)[0], 0, wblock(q, te_r, tr_r)[1]))
    grid_spec = pltpu.PrefetchScalarGridSpec(
        num_scalar_prefetch=4,
        grid=(nq,),
        in_specs=[pl.BlockSpec(memory_space=pl.ANY), pl.BlockSpec(memory_space=pl.ANY), col_spec(), col_spec(),
                  pl.BlockSpec((None, tf, d), lambda q, te_r, tr_r, *_: (wblock(q, te_r, tr_r)[0], wblock(q, te_r, tr_r)[1], 0))],
        out_specs=pl.BlockSpec(memory_space=pl.ANY),
        scratch_shapes=[pltpu.VMEM((r, d), BF16), pltpu.VMEM((r, d), F32), pltpu.VMEM((2, MOE_SUB, d), F32),
                        pltpu.VMEM((2, d, 2 * tf), BF16), pltpu.VMEM((2, tf, d), BF16),
                        pltpu.SemaphoreType.DMA((2,)), pltpu.SemaphoreType.DMA(())],
    )
    return pl.pallas_call(
        functools.partial(_moe_kernel, nj=nj),
        grid_spec=grid_spec,
        out_shape=jax.ShapeDtypeStruct((n_tiles * r, d), F32),
        compiler_params=_cparams(("arbitrary",)),
        name="moe_experts",
    )(te, tr, tsp, src, h_a, h_b, wg, wu, wd)


def _combine_kernel(s1_ref, s2_ref, x_ref, g2_ref, r_ref, gf_ref, e_hbm, o_ref, st1, st2, sem):
    nb, tt, d = x_ref.shape
    tm = nb * tt
    step = pl.program_id(0) * pl.num_programs(1) + pl.program_id(1)
    nsteps = pl.num_programs(0) * pl.num_programs(1)
    slot = step % 2

    def waits(sl):
        pltpu.make_async_copy(e_hbm.at[pl.ds(0, tm), :], st1.at[sl], sem.at[0, sl]).wait()
        pltpu.make_async_copy(e_hbm.at[pl.ds(0, tm), :], st2.at[sl], sem.at[1, sl]).wait()

    def start_row(base, sl, r):
        _row_gather_copy(e_hbm, s1_ref[base + r], st1.at[sl], r, sem.at[0, sl]).start()
        _row_gather_copy(e_hbm, s2_ref[base + r], st2.at[sl], r, sem.at[1, sl]).start()

    @pl.when(step == 0)
    def _():
        def body(r, carry):
            start_row(0, 0, r)
            return carry
        lax.fori_loop(0, tm, body, 0, unroll=GATHER_UNROLL)

    waits(slot)
    nxt = jnp.minimum(step + 1, nsteps - 1) * tm
    for r in range(tm):
        start_row(nxt, 1 - slot, r)

    f = r_ref[:, 2:3] * st1[slot] + r_ref[:, 3:4] * st2[slot]
    x2 = x_ref[...] + g2_ref[...] * f.reshape(nb, tt, d)
    rs = lax.rsqrt(jnp.mean(x2 * x2, axis=-1, keepdims=True) + EPS)
    o_ref[...] = (x2 * rs) * gf_ref[...]

    @pl.when(step == nsteps - 1)
    def _():
        waits(1 - slot)


def _combine(x, mod, l, route, slot1, slot2, e_out, gf):
    b, t, d = x.shape
    nb, tt = _tile(b, t, ROWS_COMBINE)
    tm = nb * tt
    nt = t // tt
    grid_spec = pltpu.PrefetchScalarGridSpec(
        num_scalar_prefetch=2,
        grid=(b // nb, nt),
        in_specs=[pl.BlockSpec((nb, tt, d), lambda bi, i, a, c: (bi, i, 0)),
                  _mod_spec(nb, d, 5, l),
                  pl.BlockSpec((tm, ROUTE_LANES), lambda bi, i, a, c: (bi * nt + i, 0)),
                  pl.BlockSpec((1, d), lambda bi, i, a, c: (0, 0)),
                  pl.BlockSpec(memory_space=pl.ANY)],
        out_specs=pl.BlockSpec((nb, tt, d), lambda bi, i, a, c: (bi, i, 0)),
        scratch_shapes=[pltpu.VMEM((2, tm, d), F32), pltpu.VMEM((2, tm, d), F32), pltpu.SemaphoreType.DMA((2, 2))],
    )
    return pl.pallas_call(
        _combine_kernel,
        grid_spec=grid_spec,
        out_shape=jax.ShapeDtypeStruct((b, t, d), F32),
        compiler_params=_cparams(("arbitrary", "arbitrary")),
        name="moe_combine_norm",
    )(slot1, slot2, x, mod, route, gf, e_out)


def _lookup(table, idx):
    out = jnp.zeros_like(idx)
    for e in range(N_EXPERTS):
        out = jnp.where(idx == e, table[e], out)
    return out


def _floor_div(a, b):
    q = jnp.floor(a.astype(F32) / b.astype(F32)).astype(I32)
    q = jnp.where((q + 1) * b <= a, q + 1, q)
    return jnp.where(q * b > a, q - 1, q)


def _moe_layout(route_t, counts_f, counts_a_f, n_a):
    n = route_t.shape[1]
    r, sub = MOE_TILE_ROWS, MOE_SUB
    n_tiles = -(-TOP_K * n // r) + N_EXPERTS
    i1 = route_t[0].astype(I32)
    i2 = route_t[1].astype(I32)
    r1 = route_t[4].astype(I32)
    r2 = route_t[5].astype(I32)
    counts = counts_f[0, :N_EXPERTS].astype(I32)
    nt = (counts + r - 1) // r
    ntc = jnp.maximum(nt, 1)
    sz = jnp.maximum(((_floor_div(counts + ntc - 1, ntc) + sub - 1) // sub) * sub, sub)
    cum = jnp.cumsum(nt)
    tstart = cum - nt
    total = cum[-1]
    s_idx = jnp.arange(n_tiles, dtype=I32)
    te_raw = jnp.minimum(jnp.sum((s_idx[:, None] >= cum[None, :]).astype(I32), axis=1), N_EXPERTS - 1)
    used = s_idx < total
    last_e = jnp.max(jnp.where(used, te_raw, 0))
    te = jnp.where(used, te_raw, last_e)
    sz_t = _lookup(sz, te)
    k_in = s_idx - _lookup(tstart, te)
    tr = jnp.where(used, jnp.clip(_lookup(counts, te) - k_in * sz_t, 0, sz_t), 0).astype(I32)
    counts_a = counts_a_f[0, :N_EXPERTS].astype(I32)
    tsp = jnp.clip(_lookup(counts_a, te) - k_in * sz_t, 0, tr).astype(I32)

    def slot(ei, ri):
        sz_i = _lookup(sz, ei)
        k = _floor_div(ri, sz_i)
        return (_lookup(tstart, ei) + k) * r + (ri - k * sz_i)
    slot1, slot2 = slot(i1, r1), slot(i2, r2)
    tok = jnp.arange(n, dtype=I32)
    tok = jnp.where(tok < n_a, tok, tok - n_a)
    src = jnp.zeros((n_tiles * r,), I32).at[jnp.concatenate([slot1, slot2])].set(jnp.concatenate([tok, tok]))
    return te, tr, tsp, src, slot1, slot2, n_tiles


def kernel(x_prompt, x_sample, c_prompt, c_sample, state_pool, state_ssm_re, state_ssm_im, w_ada, b_ada, norm_mix, norm_ffn, w_in, w_pool, pool_scale, ssm_lam_re, ssm_lam_im, ssm_log_step, ssm_b_re, ssm_b_im, ssm_c_re, ssm_c_im, ssm_d, ssm_w_glu, w_out, ffn_w_gate, ffn_w_up, ffn_w_down, moe_w_router, moe_b_router, moe_w_gate, moe_w_up, moe_w_down, norm_final):
    bp, tp, d = x_prompt.shape
    bs, ts, _ = x_sample.shape
    depth = w_ada.shape[0]
    c_pool = w_pool.shape[1] * w_pool.shape[2]
    assert tp % CHUNK == 0 and ts <= CHUNK

    m_rows = -(-(bp + bs) // SUBLANES_V7X) * SUBLANES_V7X
    c_all = jnp.concatenate([c_prompt, c_sample, jnp.zeros((m_rows - bp - bs, d), F32)], axis=0)
    mod_all = _ada(c_all, w_ada, b_ada)

    mod_p = mod_all[:, :bp].reshape(depth, bp, 1, 6 * d)
    mod_s = mod_all[:, bp:bp + bs].reshape(depth, bs, 1, 6 * d)
    w_in_b, w_out_b = w_in.astype(BF16), w_out.astype(BF16)
    wp_b, wglu_b = w_pool.astype(BF16), ssm_w_glu.astype(BF16)
    ps = pool_scale.reshape(depth, 1, c_pool).astype(F32)
    dsk = ssm_d.reshape(depth, 1, d - c_pool).astype(F32)
    k_t, b_t, c_t, p16, q16, p8b, q8b = jax.vmap(lambda *a: _s5_params(*a, ts, bs))(
        ssm_lam_re, ssm_lam_im, ssm_log_step, ssm_b_re, ssm_b_im, ssm_c_re, ssm_c_im)
    x0_t = jnp.concatenate([state_ssm_re, state_ssm_im], axis=-1).astype(F32).transpose(0, 2, 3, 1)
    st_p = jnp.zeros((1, bp, POOL_BUF, c_pool), F32)
    st_s = state_pool.astype(F32)

    xp, xs = x_prompt, x_sample
    pool_p, re_p, im_p, pool_s, re_s, im_s = [], [], [], [], [], []
    y_p = y_s = None
    for l in range(depth):
        gm = norm_mix[l].reshape(1, d)
        gn = norm_ffn[l].reshape(1, d)

        up_all = _proj_in(xp, mod_p, l, gm, w_in_b)
        us_all = _proj_in(xs, mod_s, l, gm, w_in_b)
        y_ssm_p, xend = _s5_prompt(up_all, l, k_t, b_t, c_t, p16, q16, c_pool)
        ys2, x1_t = _s5_sample(us_all.reshape(bs * ts, d), l, x0_t, k_t, b_t, c_t, p8b, q8b, ts, c_pool)
        y_ssm_s = ys2.reshape(bs, ts, d - c_pool)
        ffn_f32 = (ffn_w_gate[l // 2], ffn_w_up[l // 2], ffn_w_down[l // 2]) if l % 2 == 0 else ()
        xp, new_pool_p, ffn_bf16 = _mix(xp, up_all, y_ssm_p, st_p, 0, mod_p, l, wp_b, ps, dsk, wglu_b, w_out_b, 0,
                                        cast=ffn_f32)
        xs, new_pool_s, _ = _mix(xs, us_all, y_ssm_s, st_s, l, mod_s, l, wp_b, ps, dsk, wglu_b, w_out_b, PAST_LEN)
        pool_p.append(new_pool_p)
        pool_s.append(new_pool_s)
        re_p.append(xend[..., :SSM_P])
        im_p.append(xend[..., SSM_P:])
        re_s.append(x1_t[:, :SSM_P, :].transpose(2, 0, 1))
        im_s.append(x1_t[:, SSM_P:, :].transpose(2, 0, 1))

        i = l // 2
        if l % 2 == 0:
            wg, wu, wd = ffn_bf16
            xp = _ffn(xp, mod_p, l, gn, wg, wu, wd)
            xs = _ffn(xs, mod_s, l, gn, wg, wu, wd)
        else:
            ne = moe_w_router.shape[-1]
            wr32 = jnp.pad(moe_w_router[i].astype(F32), ((0, 0), (0, ROUTE_LANES - ne)))
            wr_hi = wr32.astype(BF16)
            wr = jnp.stack([wr_hi, (wr32 - wr_hi.astype(F32)).astype(BF16)])
            br = jnp.pad(moe_b_router[i].astype(F32), (0, ROUTE_LANES - ne)).reshape(1, ROUTE_LANES)
            zero_counts = jnp.zeros((1, ROUTE_LANES), F32)
            h_p, route_p, rt_p, cnt_p = _route(xp, mod_p, l, gn, wr, br, zero_counts)
            h_s, route_s, rt_s, cnt_s = _route(xs, mod_s, l, gn, wr, br, cnt_p)
            n_p = bp * tp
            te, tr, tsp, src, slot1, slot2, n_tiles = _moe_layout(jnp.concatenate([rt_p, rt_s], axis=1), cnt_s, cnt_p,
                                                                  n_p)
            e_out = _moe_experts(h_p, h_s, te, tr, tsp, src, moe_w_gate[i], moe_w_up[i], moe_w_down[i], n_tiles)
            last = l == depth - 1
            assert last, "the combine kernel also applies the final norm"
            gf = norm_final.reshape(1, d)
            y_p = _combine(xp, mod_p, l, route_p, slot1[:n_p], slot2[:n_p], e_out, gf)
            y_s = _combine(xs, mod_s, l, route_s, slot1[n_p:], slot2[n_p:], e_out, gf)

    return (y_p, y_s, jnp.stack(pool_p), jnp.stack(re_p), jnp.stack(im_p),
            jnp.stack(pool_s), jnp.stack(re_s), jnp.stack(im_s))
```

```python
import functools
import math

import jax
import jax.numpy as jnp
import numpy as np
from jax import lax
from jax.experimental import pallas as pl
from jax.experimental.pallas import tpu as pltpu

F32 = jnp.float32
BF16 = jnp.bfloat16
I32 = jnp.int32

EPS = 1e-6
POOL_WINDOWS = (2, 4, 8, 16)
POOL_BUF = max(POOL_WINDOWS) - 1
SUBLANES_V7X = 8
BF16_SUBLANES_V7X = 16
LANES_V7X = 128
HALO = -(-POOL_BUF // SUBLANES_V7X) * SUBLANES_V7X
SSM_H = 16
SSM_P = 64
CHUNK = 16
CW = CHUNK * SSM_H
SW = 2 * SSM_P
N_EXPERTS = 8
TOP_K = 2
PAST_LEN = 16384

VMEM_LIMIT_V7X = 56 * 1024 * 1024
ROWS_IN = 512
ROWS_MIX = 256
ROWS_FFN = 512
FF_TILE = 512
ROWS_ROUTE = 512
MOE_TILE_ROWS = 2560
MOE_SUB = 128
MOE_BLK_BIG = 1024
MOE_BLK = 512
GATHER_UNROLL = 8
DMA_PRIORITIES_V7X = 2
MOE_FF_TILE = 256
ROWS_COMBINE = 256
ADA_TN = 1024


def _cparams(sem):
    return pltpu.CompilerParams(dimension_semantics=sem, vmem_limit_bytes=VMEM_LIMIT_V7X)


def _tile(b, t, rows):
    if t >= rows:
        assert t % rows == 0
        return 1, rows
    nb = max(1, min(b, rows // t))
    assert b % nb == 0
    return nb, t


def _modnorm(x, g, s, sh):
    r = lax.rsqrt(jnp.mean(x * x, axis=-1, keepdims=True) + EPS)
    return (x * r) * g * (1.0 + s) + sh


def _dot(a, b):
    return jnp.dot(a, b, preferred_element_type=F32)


def _mod_spec(nb, d, k, l):
    return pl.BlockSpec((None, nb, 1, d), lambda b, i, *_: (l, b, 0, k))


def _layer_spec(shape, l):
    return pl.BlockSpec((None,) + tuple(shape), lambda *_: (l,) + (0,) * len(shape))


def _ada_kernel(c_ref, w_ref, b_ref, o_ref):
    c = c_ref[...]
    sc = c * jax.nn.sigmoid(c)
    o_ref[0] = _dot(sc.astype(BF16), w_ref[0].astype(BF16)) + b_ref[0]


def _ada(c_all, w_ada, b_ada):
    m, d = c_all.shape
    depth, _, n6 = w_ada.shape
    tn = math.gcd(ADA_TN, n6)
    return pl.pallas_call(
        _ada_kernel,
        grid=(depth, n6 // tn),
        in_specs=[pl.BlockSpec((m, d), lambda l, n: (0, 0)),
                  pl.BlockSpec((1, d, tn), lambda l, n: (l, 0, n)),
                  pl.BlockSpec((1, 1, tn), lambda l, n: (l, 0, n))],
        out_specs=pl.BlockSpec((1, m, tn), lambda l, n: (l, 0, n)),
        out_shape=jax.ShapeDtypeStruct((depth, m, n6), F32),
        compiler_params=_cparams(("arbitrary", "arbitrary")),
        name="adaln_mod",
    )(c_all, w_ada, b_ada.reshape(depth, 1, n6))


def _in_kernel(x_ref, g_ref, s_ref, sh_ref, w_ref, u_ref):
    nb, tt, d = x_ref.shape
    h = _modnorm(x_ref[...], g_ref[...], s_ref[...], sh_ref[...])
    u = _dot(h.reshape(nb * tt, d).astype(BF16), w_ref[...])
    u_ref[...] = u.reshape(nb, tt, u_ref.shape[-1])


def _proj_in(x, mod, l, g, w_bf16):
    b, t, d = x.shape
    n = w_bf16.shape[-1]
    nb, tt = _tile(b, t, ROWS_IN)
    return pl.pallas_call(
        _in_kernel,
        grid=(b // nb, t // tt),
        in_specs=[pl.BlockSpec((nb, tt, d), lambda bi, i: (bi, i, 0)),
                  pl.BlockSpec((1, d), lambda bi, i: (0, 0)),
                  _mod_spec(nb, d, 1, l), _mod_spec(nb, d, 0, l),
                  _layer_spec((d, n), l)],
        out_specs=pl.BlockSpec((nb, tt, n), lambda bi, i: (bi, i, 0)),
        out_shape=jax.ShapeDtypeStruct((b, t, n), F32),
        compiler_params=_cparams(("arbitrary", "arbitrary")),
        name="norm_proj_in",
    )(x, g, mod, mod, w_bf16)


GROUPS_PER_LANE_TILE = LANES_V7X // SSM_H


def _toeplitz_placement():
    toe = np.zeros((CHUNK, CW, CW), np.float32)
    for s in range(CHUNK):
        for h in range(SSM_H):
            for t in range(s, CHUNK):
                toe[t, (t - s) * SSM_H + h, s * SSM_H + h] = 1.0
    return jnp.asarray(toe)


def _s5_params(lam_re, lam_im, log_step, b_re, b_im, c_re, c_im, ts, bs):
    hp = lax.Precision.HIGHEST
    toe = _toeplitz_placement()
    lr, li = lam_re.astype(F32), lam_im.astype(F32)
    step = jnp.exp(log_step.astype(F32))[:, None]
    mag = jnp.exp(lr * step)
    a_re = mag * jnp.cos(li * step)
    a_im = mag * jnp.sin(li * step)
    den = lr * lr + li * li
    nr = a_re - 1.0
    f_re = (nr * lr + a_im * li) / den
    f_im = (a_im * lr - nr * li) / den
    br, bi = b_re.astype(F32), b_im.astype(F32)
    bb_re = f_re[..., None] * br - f_im[..., None] * bi
    bb_im = f_re[..., None] * bi + f_im[..., None] * br
    cr, ci = c_re.astype(F32), c_im.astype(F32)
    g = lr.shape[0]

    pw_re, pw_im = [jnp.ones_like(a_re)], [jnp.zeros_like(a_im)]
    for _ in range(CHUNK):
        pr, pi = pw_re[-1], pw_im[-1]
        pw_re.append(pr * a_re - pi * a_im)
        pw_im.append(pr * a_im + pi * a_re)
    pw_re, pw_im = jnp.stack(pw_re), jnp.stack(pw_im)

    bbt_re, bbt_im = bb_re.transpose(0, 2, 1), bb_im.transpose(0, 2, 1)
    w_re = pw_re[:CHUNK, :, None, :] * bbt_re - pw_im[:CHUNK, :, None, :] * bbt_im
    w_im = pw_re[:CHUNK, :, None, :] * bbt_im + pw_im[:CHUNK, :, None, :] * bbt_re
    m = (jnp.einsum('gop,lgip->goli', cr, w_re, precision=hp)
         - jnp.einsum('gop,lgip->goli', ci, w_im, precision=hp)).reshape(g, SSM_H, CW)
    k_t = jnp.einsum('gok,tkn->gton', m.astype(BF16), toe.astype(BF16),
                     preferred_element_type=F32).reshape(g, CW, CW)

    rev_re, rev_im = pw_re[:CHUNK][::-1], pw_im[:CHUNK][::-1]
    ar = jnp.repeat(rev_re.transpose(1, 2, 0), SSM_H, axis=-1)
    ai = jnp.repeat(rev_im.transpose(1, 2, 0), SSM_H, axis=-1)
    xr = jnp.tile(bb_re, (1, 1, CHUNK))
    xi = jnp.tile(bb_im, (1, 1, CHUNK))
    bt_re = ar * xr - ai * xi
    bt_im = ar * xi + ai * xr
    b_t = jnp.concatenate([bt_re, bt_im, bt_im, bt_re], axis=1)

    qr = pw_re[1:].transpose(1, 0, 2)[:, :, None, :]
    qi = pw_im[1:].transpose(1, 0, 2)[:, :, None, :]
    cm_re = cr[:, None] * qr - ci[:, None] * qi
    cm_im = -(cr[:, None] * qi + ci[:, None] * qr)
    c_t = jnp.concatenate([cm_re, cm_im], axis=-1).reshape(g, CW, SW)

    def coef(k):
        pr, pi = pw_re[k], pw_im[k]
        return (jnp.concatenate([pr, pr, pr, pr], axis=-1),
                jnp.concatenate([-pi, pi, pi, -pi], axis=-1))
    p16, q16 = coef(CHUNK)
    p8, q8 = coef(ts)
    p8b = jnp.broadcast_to(p8[:, :SW, None], (g, SW, bs))
    q8b = jnp.broadcast_to(q8[:, :SW, None], (g, SW, bs))
    return k_t.astype(BF16), b_t.astype(BF16), c_t.astype(BF16), p16, q16, p8b, q8b


def _s5_prompt_kernel(us_ref, kt_ref, bt_ref, ct_ref, p_ref, q_ref, y_ref, xe_ref, ut, yt, scan_x, scan_xs, s_scr):
    gt = ut.shape[0]
    nchunk = ut.shape[-1]
    for t in range(CHUNK):
        slab = us_ref[0, pl.ds(t, nchunk, stride=CHUNK), :]
        ut[:, t] = slab.T.reshape(gt, SSM_H, nchunk).astype(BF16)
    for gi in range(gt):
        buz = _dot(bt_ref[gi], ut[gi].reshape(CW, nchunk))
        buz_t = buz.T
        scan_x[pl.ds(gi, nchunk, stride=gt), :] = buz_t[:, :SW]
        scan_xs[pl.ds(gi, nchunk, stride=gt), :] = buz_t[:, SW:]
    px, pxs = p_ref[:, :SW], p_ref[:, SW:]
    qx, qxs = q_ref[:, :SW], q_ref[:, SW:]
    x = jnp.zeros((gt, SW), F32)
    xs = jnp.zeros((gt, SW), F32)
    for c in range(nchunk):
        lo, hi = c * gt, (c + 1) * gt
        s_scr[lo:hi, :] = x
        x, xs = (px * x + qx * xs + scan_x[lo:hi, :],
                 pxs * xs + qxs * x + scan_xs[lo:hi, :])
    xe_ref[0] = x
    for gi in range(gt):
        st = s_scr[pl.ds(gi, nchunk, stride=gt), :].T.astype(BF16)
        y_t = _dot(kt_ref[gi], ut[gi].reshape(CW, nchunk)) + _dot(ct_ref[gi], st)
        yt[gi] = y_t.reshape(CHUNK, SSM_H, nchunk)
    for t in range(CHUNK):
        y_ref[0, pl.ds(t, nchunk, stride=CHUNK), :] = yt[:, t].reshape(gt * SSM_H, nchunk).T


def _s5_prompt(u, l, k_t, b_t, c_t, p16, q16, c_off):
    b, t, _ = u.shape
    g = k_t.shape[1]
    gt = GROUPS_PER_LANE_TILE
    lt = gt * SSM_H
    nchunk = t // CHUNK
    off = c_off // lt

    def wspec(r, c):
        return pl.BlockSpec((None, gt, r, c), lambda bi, j: (l, j, 0, 0))
    return pl.pallas_call(
        _s5_prompt_kernel,
        grid=(b, g // gt),
        in_specs=[pl.BlockSpec((1, t, lt), lambda bi, j: (bi, 0, off + j)),
                  wspec(CW, CW), wspec(2 * SW, CW), wspec(CW, SW),
                  pl.BlockSpec((None, gt, 2 * SW), lambda bi, j: (l, j, 0)),
                  pl.BlockSpec((None, gt, 2 * SW), lambda bi, j: (l, j, 0))],
        out_specs=[pl.BlockSpec((1, t, lt), lambda bi, j: (bi, 0, j)),
                   pl.BlockSpec((1, gt, SW), lambda bi, j: (bi, j, 0))],
        out_shape=[jax.ShapeDtypeStruct((b, t, g * SSM_H), F32), jax.ShapeDtypeStruct((b, g, SW), F32)],
        scratch_shapes=[pltpu.VMEM((gt, CHUNK, SSM_H, nchunk), BF16), pltpu.VMEM((gt, CHUNK, SSM_H, nchunk), F32),
                        pltpu.VMEM((nchunk * gt, SW), F32), pltpu.VMEM((nchunk * gt, SW), F32),
                        pltpu.VMEM((nchunk * gt, SW), F32)],
        compiler_params=_cparams(("arbitrary", "arbitrary")),
        name="s5_prompt",
    )(u, k_t, b_t, c_t, p16, q16)


def _s5_sample_kernel(us_ref, x0_ref, kt_ref, bt_ref, ct_ref, p_ref, q_ref, y_ref, x1_ref, ut, yt, *, ts):
    gt = ut.shape[0]
    bs = ut.shape[-1]
    w = ts * SSM_H
    for t in range(ts):
        slab = us_ref[pl.ds(t, bs, stride=ts), :]
        ut[:, t] = slab.T.reshape(gt, SSM_H, bs).astype(BF16)
    for gi in range(gt):
        ug = ut[gi].reshape(w, bs)
        x0 = x0_ref[gi]
        x0s = jnp.concatenate([x0[SSM_P:], x0[:SSM_P]], axis=0)
        y_t = _dot(kt_ref[gi, :w, :w], ug) + _dot(ct_ref[gi, :w, :], x0.astype(BF16))
        x1_ref[gi] = p_ref[gi] * x0 + q_ref[gi] * x0s + _dot(bt_ref[gi, :SW, CW - w:], ug)
        yt[gi] = y_t.reshape(ts, SSM_H, bs)
    for t in range(ts):
        y_ref[pl.ds(t, bs, stride=ts), :] = yt[:, t].reshape(gt * SSM_H, bs).T


def _s5_sample(u2, l, x0_t, k_t, b_t, c_t, p8b, q8b, ts, c_off):
    rows, _ = u2.shape
    _, g, _, bs = x0_t.shape
    gt = GROUPS_PER_LANE_TILE
    lt = gt * SSM_H
    off = c_off // lt

    def wspec(r, c):
        return pl.BlockSpec((None, gt, r, c), lambda j: (l, j, 0, 0))
    return pl.pallas_call(
        functools.partial(_s5_sample_kernel, ts=ts),
        grid=(g // gt,),
        in_specs=[pl.BlockSpec((rows, lt), lambda j: (0, off + j)),
                  wspec(SW, bs), wspec(CW, CW), wspec(2 * SW, CW), wspec(CW, SW), wspec(SW, bs), wspec(SW, bs)],
        out_specs=[pl.BlockSpec((rows, lt), lambda j: (0, j)), pl.BlockSpec((gt, SW, bs), lambda j: (j, 0, 0))],
        out_shape=[jax.ShapeDtypeStruct((rows, g * SSM_H), F32), jax.ShapeDtypeStruct((g, SW, bs), F32)],
        scratch_shapes=[pltpu.VMEM((gt, ts, SSM_H, bs), BF16), pltpu.VMEM((gt, ts, SSM_H, bs), F32)],
        compiler_params=_cparams(("arbitrary",)),
        name="s5_sample",
    )(u2, x0_t, k_t, b_t, c_t, p8b, q8b)


def _mix_kernel(x_ref, up_ref, us_ref, ys_ref, st_ref, g1_ref, wp_ref, ps_ref, dsk_ref, wglu_ref,
                wout_ref, *rest, start_pos, carry, ncast):
    cast_in, (o_ref, np_ref), cast_out, ext = rest[:ncast], rest[ncast:ncast + 2], rest[ncast + 2:-1], rest[-1]
    for src, dst in zip(cast_in, cast_out):
        dst[...] = src[...].astype(BF16)
    nb, tt, d = x_ref.shape
    c = up_ref.shape[-1]
    pg = c // len(POOL_WINDOWS)
    rows = nb * tt
    i = pl.program_id(1)

    @pl.when(i == 0)
    def _():
        ext[:, HALO - POOL_BUF:HALO, :] = st_ref[...]
    ext[:, HALO:HALO + tt, :] = up_ref[...]
    np_ref[...] = ext[:, HALO + tt - POOL_BUF:HALO + tt, :]

    pos = start_pos + i * tt + lax.broadcasted_iota(I32, (1, tt, 1), 1)
    outs = []
    for gidx, w in enumerate(POOL_WINDOWS):
        c0 = gidx * pg
        cur = ext[:, HALO:HALO + tt, c0:c0 + pg]
        s = cur
        for k in range(1, w):
            s = s + ext[:, HALO - k:HALO - k + tt, c0:c0 + pg]
        cnt = jnp.minimum(pos + 1, w).astype(F32)
        dd = s / cnt - cur
        yg = _dot(dd.reshape(rows, pg).astype(BF16), wp_ref[gidx])
        outs.append(yg * ps_ref[:, c0:c0 + pg])
    y_pool = jnp.concatenate(outs, axis=-1)

    if carry:
        ext[:, 0:HALO, :] = ext[:, tt:tt + HALO, :]

    us = us_ref[...].reshape(rows, c)
    yf = ys_ref[...].reshape(rows, c) + dsk_ref[...] * us
    gl = jax.nn.gelu(yf, approximate=True)
    o = gl * jax.nn.sigmoid(_dot(gl.astype(BF16), wglu_ref[...]))
    mo = _dot(y_pool.astype(BF16), wout_ref[:c, :]) + _dot(o.astype(BF16), wout_ref[c:, :])
    o_ref[...] = x_ref[...] + g1_ref[...] * mo.reshape(nb, tt, d)


def _mix(x, u, y_ssm, pool_state, l_state, mod, l, wp, ps, dsk, wglu, wout, start_pos, cast=()):
    b, t, d = x.shape
    c = y_ssm.shape[-1]
    nb, tt = _tile(b, t, ROWS_MIX)
    nt = t // tt
    assert nt == 1 or tt >= HALO
    nsteps = (b // nb) * nt
    cast_specs = []
    for a in cast:
        blk = -(-(-(-a.shape[0] // nsteps)) // BF16_SUBLANES_V7X) * BF16_SUBLANES_V7X
        last = -(-a.shape[0] // blk) - 1
        cast_specs.append(pl.BlockSpec((blk, a.shape[1]), lambda bi, i, last=last: (jnp.minimum(bi * nt + i, last), 0)))
    outs = pl.pallas_call(
        functools.partial(_mix_kernel, start_pos=start_pos, carry=nt > 1, ncast=len(cast)),
        grid=(b // nb, nt),
        in_specs=[pl.BlockSpec((nb, tt, d), lambda bi, i: (bi, i, 0)),
                  pl.BlockSpec((nb, tt, c), lambda bi, i: (bi, i, 0)),
                  pl.BlockSpec((nb, tt, c), lambda bi, i: (bi, i, 1)),
                  pl.BlockSpec((nb, tt, c), lambda bi, i: (bi, i, 0)),
                  pl.BlockSpec((None, nb, POOL_BUF, c), lambda bi, i: (l_state, bi, 0, 0)),
                  _mod_spec(nb, d, 2, l),
                  _layer_spec(wp.shape[1:], l), _layer_spec((1, c), l), _layer_spec((1, c), l),
                  _layer_spec((c, c), l), _layer_spec((d, d), l)] + cast_specs,
        out_specs=[pl.BlockSpec((nb, tt, d), lambda bi, i: (bi, i, 0)),
                   pl.BlockSpec((nb, POOL_BUF, c), lambda bi, i: (bi, 0, 0))] + cast_specs,
        out_shape=[jax.ShapeDtypeStruct((b, t, d), F32), jax.ShapeDtypeStruct((b, POOL_BUF, c), F32)]
        + [jax.ShapeDtypeStruct(a.shape, BF16) for a in cast],
        scratch_shapes=[pltpu.VMEM((nb, HALO + tt, c), F32)],
        compiler_params=_cparams(("arbitrary", "arbitrary")),
        name="pool_glu_proj_out",
    )(x, u, u, y_ssm, pool_state, mod, wp, ps, dsk, wglu, wout, *cast)
    return outs[0], outs[1], tuple(outs[2:])


def _ffn_kernel(x_ref, g_ref, s_ref, sh_ref, g2_ref, wg_ref, wu_ref, wd_ref, o_ref, hb, acc, *, last_valid):
    nb, tt, d = x_ref.shape
    tf = wg_ref.shape[1]
    j = pl.program_id(2)
    nj = pl.num_programs(2)

    @pl.when(j == 0)
    def _():
        h = _modnorm(x_ref[...], g_ref[...], s_ref[...], sh_ref[...])
        hb[...] = h.reshape(nb * tt, d).astype(BF16)
        acc[...] = jnp.zeros_like(acc)

    def step(valid):
        h = hb[...]
        gate = _dot(h, wg_ref[...])
        up = _dot(h, wu_ref[...])
        a = gate * jax.nn.sigmoid(gate) * up
        wd = wd_ref[...]
        if valid < tf:
            a = jnp.where(lax.broadcasted_iota(I32, (1, tf), 1) < valid, a, 0.0)
            wd = jnp.where(lax.broadcasted_iota(I32, (tf, 1), 0) < valid, wd, jnp.zeros_like(wd))
        acc[...] += _dot(a.astype(BF16), wd)

    if last_valid == tf:
        step(tf)
    else:
        pl.when(j < nj - 1)(lambda: step(tf))
        pl.when(j == nj - 1)(lambda: step(last_valid))

    @pl.when(j == nj - 1)
    def _():
        o_ref[...] = x_ref[...] + g2_ref[...] * acc[...].reshape(nb, tt, d)


def _ffn(x, mod, l, g, wg, wu, wd):
    b, t, d = x.shape
    ff = wg.shape[1]
    nb, tt = _tile(b, t, ROWS_FFN)
    tf = min(FF_TILE, ff)
    nj = -(-ff // tf)
    return pl.pallas_call(
        functools.partial(_ffn_kernel, last_valid=ff - (nj - 1) * tf),
        grid=(b // nb, t // tt, nj),
        in_specs=[pl.BlockSpec((nb, tt, d), lambda bi, i, j: (bi, i, 0)),
                  pl.BlockSpec((1, d), lambda bi, i, j: (0, 0)),
                  _mod_spec(nb, d, 4, l), _mod_spec(nb, d, 3, l), _mod_spec(nb, d, 5, l),
                  pl.BlockSpec((d, tf), lambda bi, i, j: (0, j)),
                  pl.BlockSpec((d, tf), lambda bi, i, j: (0, j)),
                  pl.BlockSpec((tf, d), lambda bi, i, j: (j, 0))],
        out_specs=pl.BlockSpec((nb, tt, d), lambda bi, i, j: (bi, i, 0)),
        out_shape=jax.ShapeDtypeStruct((b, t, d), F32),
        scratch_shapes=[pltpu.VMEM((nb * tt, d), BF16), pltpu.VMEM((nb * tt, d), F32)],
        compiler_params=_cparams(("arbitrary", "arbitrary", "arbitrary")),
        name="dense_swiglu",
    )(x, g, mod, mod, mod, wg, wu, wd)


ROUTE_LANES = LANES_V7X
ROUTE_ROWS = SUBLANES_V7X


def _route_kernel(x_ref, g_ref, s_ref, sh_ref, wr_ref, br_ref, cin_ref, h_ref, r_ref, rt_ref, cout_ref, carry):
    nb, tt, d = x_ref.shape
    tm = nb * tt
    first = jnp.logical_and(pl.program_id(0) == 0, pl.program_id(1) == 0)

    @pl.when(first)
    def _():
        carry[...] = cin_ref[...]

    h = _modnorm(x_ref[...], g_ref[...], s_ref[...], sh_ref[...]).reshape(tm, d)
    h_ref[...] = h
    h_hi = h.astype(BF16)
    h_lo = (h - h_hi.astype(F32)).astype(BF16)
    logits = _dot(h_hi, wr_ref[0]) + (_dot(h_hi, wr_ref[1]) + _dot(h_lo, wr_ref[0])) + br_ref[...]
    lane = lax.broadcasted_iota(I32, (tm, ROUTE_LANES), 1)
    neg = jnp.float32(-jnp.inf)
    l1 = jnp.where(lane < N_EXPERTS, logits, neg)
    m1 = jnp.max(l1, axis=-1, keepdims=True)
    i1 = jnp.min(jnp.where(l1 == m1, lane, ROUTE_LANES), axis=-1, keepdims=True)
    l2 = jnp.where(lane == i1, neg, l1)
    m2 = jnp.max(l2, axis=-1, keepdims=True)
    i2 = jnp.min(jnp.where(l2 == m2, lane, ROUTE_LANES), axis=-1, keepdims=True)
    e2 = jnp.exp(m2 - m1)
    den = 1.0 + e2
    p1 = 1.0 / den
    p2 = e2 / den

    onehot = jnp.logical_or(lane == i1, lane == i2).astype(F32)
    rr = lax.broadcasted_iota(I32, (tm, tm), 0)
    cc = lax.broadcasted_iota(I32, (tm, tm), 1)
    before = (cc < rr).astype(BF16)
    rank = _dot(before, onehot.astype(BF16)) + carry[...]
    r1 = jnp.sum(jnp.where(lane == i1, rank, 0.0), axis=-1, keepdims=True)
    r2 = jnp.sum(jnp.where(lane == i2, rank, 0.0), axis=-1, keepdims=True)
    carry[...] += jnp.sum(onehot, axis=0, keepdims=True)

    out = jnp.where(lane == 0, i1.astype(F32), 0.0)
    out = jnp.where(lane == 1, i2.astype(F32), out)
    out = jnp.where(lane == 2, p1, out)
    out = jnp.where(lane == 3, p2, out)
    out = jnp.where(lane == 4, r1, out)
    out = jnp.where(lane == 5, r2, out)
    r_ref[...] = out
    rt_ref[...] = out.T[:ROUTE_ROWS, :]
    cout_ref[...] = carry[...]


def _route(x, mod, l, g, wr_pad, br_pad, counts_in):
    b, t, d = x.shape
    nb, tt = _tile(b, t, ROWS_ROUTE)
    tm = nb * tt
    nt = t // tt
    const2 = lambda bi, i: (0, 0)
    return pl.pallas_call(
        _route_kernel,
        grid=(b // nb, nt),
        in_specs=[pl.BlockSpec((nb, tt, d), lambda bi, i: (bi, i, 0)),
                  pl.BlockSpec((1, d), const2),
                  _mod_spec(nb, d, 4, l), _mod_spec(nb, d, 3, l),
                  pl.BlockSpec((2, d, ROUTE_LANES), lambda bi, i: (0, 0, 0)),
                  pl.BlockSpec((1, ROUTE_LANES), const2),
                  pl.BlockSpec((1, ROUTE_LANES), const2)],
        out_specs=[pl.BlockSpec((tm, d), lambda bi, i: (bi * nt + i, 0)),
                   pl.BlockSpec((tm, ROUTE_LANES), lambda bi, i: (bi * nt + i, 0)),
                   pl.BlockSpec((ROUTE_ROWS, tm), lambda bi, i: (0, bi * nt + i)),
                   pl.BlockSpec((1, ROUTE_LANES), const2)],
        out_shape=[jax.ShapeDtypeStruct((b * t, d), F32),
                   jax.ShapeDtypeStruct((b * t, ROUTE_LANES), F32),
                   jax.ShapeDtypeStruct((ROUTE_ROWS, b * t), F32),
                   jax.ShapeDtypeStruct((1, ROUTE_LANES), F32)],
        scratch_shapes=[pltpu.VMEM((1, ROUTE_LANES), F32)],
        compiler_params=_cparams(("arbitrary", "arbitrary")),
        name="moe_route",
    )(x, g, mod, mod, wr_pad, br_pad, counts_in)


def _row_gather_copy(src_hbm, tok, dst, r, sem):
    return pltpu.make_async_copy(src_hbm.at[pl.ds(tok, 1), :], dst.at[pl.ds(r, 1), :], sem)


def _moe_kernel(te_ref, tr_ref, tsp_ref, src_ref, ha_hbm, hb_hbm, wg_ref, wu_ref, wd_ref, o_hbm,
                xb, acc, stage, wgu, wdb, gsem, osem):
    s = pl.program_id(0)
    j = pl.program_id(1)
    ns = pl.num_programs(0)
    nj = pl.num_programs(1)
    rows = tr_ref[s]
    nsub = (rows + MOE_SUB - 1) // MOE_SUB
    tile_rows = xb.shape[0]

    def out_copy():
        return pltpu.make_async_copy(acc, o_hbm.at[pl.ds(pl.multiple_of(s * tile_rows, MOE_SUB), tile_rows), :], osem)

    @pl.when(j == 0)
    def _():
        @pl.when(rows > 0)
        def _gather():
            def issue(sub, slot):
                base = s * tile_rows + sub * MOE_SUB
                na = jnp.clip(tsp_ref[s] - sub * MOE_SUB, 0, MOE_SUB)

                def start_from(h_hbm):
                    def body(r, carry):
                        _row_gather_copy(h_hbm, src_ref[base + r], stage.at[slot], r, gsem.at[slot]).start()
                        return carry
                    return body

                def start_group_from(h_hbm):
                    def body(g, carry):
                        for k in range(GATHER_UNROLL):
                            r = g * GATHER_UNROLL + k
                            _row_gather_copy(h_hbm, src_ref[base + r], stage.at[slot], r,
                                             gsem.at[slot]).start(priority=k % DMA_PRIORITIES_V7X)
                        return carry
                    return body

                @pl.when(na == MOE_SUB)
                def _():
                    lax.fori_loop(0, MOE_SUB // GATHER_UNROLL, start_group_from(ha_hbm), 0)

                @pl.when(na == 0)
                def _():
                    lax.fori_loop(0, MOE_SUB // GATHER_UNROLL, start_group_from(hb_hbm), 0)

                @pl.when(jnp.logical_and(na > 0, na < MOE_SUB))
                def _():
                    lax.fori_loop(0, na, start_from(ha_hbm), 0)
                    lax.fori_loop(na, MOE_SUB, start_from(hb_hbm), 0)

            issue(0, 0)

            def sub_body(sub, carry):
                slot = sub % 2

                @pl.when(sub + 1 < nsub)
                def _():
                    issue(sub + 1, 1 - slot)
                pltpu.make_async_copy(ha_hbm.at[pl.ds(0, MOE_SUB), :], stage.at[slot], gsem.at[slot]).wait()
                r0 = pl.multiple_of(sub * MOE_SUB, MOE_SUB)
                xb[pl.ds(r0, MOE_SUB), :] = stage[slot].astype(BF16)
                return carry
            lax.fori_loop(0, nsub, sub_body, 0)

        @pl.when(s > 0)
        def _():
            out_copy().wait()
        acc[...] = jnp.zeros_like(acc)

    @pl.when(rows > 0)
    def _():
        tf = wg_ref.shape[1]
        wgu[:, :tf] = wg_ref[...].astype(BF16)
        wgu[:, tf:] = wu_ref[...].astype(BF16)
        wdb[...] = wd_ref[...].astype(BF16)

        def block(r0, m):
            gu = _dot(xb[pl.ds(r0, m), :], wgu[...])
            gate, up = gu[:, :tf], gu[:, tf:]
            a = (gate * jax.nn.sigmoid(gate) * up).astype(BF16)
            acc[pl.ds(r0, m), :] += _dot(a, wdb[...])

        per_big = MOE_BLK_BIG // MOE_SUB
        per_blk = MOE_BLK // MOE_SUB
        nbig = nsub // per_big
        rem_big = nsub - nbig * per_big
        has_blk = rem_big >= per_blk
        rem = rem_big - jnp.where(has_blk, per_blk, 0)
        r_blk = pl.multiple_of(nbig * MOE_BLK_BIG, MOE_BLK)
        r_tail = pl.multiple_of(r_blk + jnp.where(has_blk, MOE_BLK, 0), MOE_SUB)

        def body(blk, carry):
            block(pl.multiple_of(blk * MOE_BLK_BIG, MOE_BLK_BIG), MOE_BLK_BIG)
            return carry
        lax.fori_loop(0, nbig, body, 0)
        pl.when(has_blk)(lambda: block(r_blk, MOE_BLK))
        for k in range(1, per_blk):
            @pl.when(rem == k)
            def _(k=k):
                block(r_tail, k * MOE_SUB)

    @pl.when(j == nj - 1)
    def _():
        out_copy().start()

        @pl.when(s == ns - 1)
        def _():
            out_copy().wait()


def _moe_experts(h_a, h_b, te, tr, tsp, src, wg, wu, wd, n_tiles):
    d = h_a.shape[1]
    assert h_a.shape[0] >= MOE_SUB
    e, _, ff = wg.shape
    tf = min(MOE_FF_TILE, ff)
    assert ff % tf == 0 and MOE_TILE_ROWS % MOE_BLK == 0 and MOE_BLK % MOE_SUB == 0 and MOE_BLK_BIG == 2 * MOE_BLK
    nj = ff // tf
    r = MOE_TILE_ROWS

    def jj(s, j, tr_ref):
        return jnp.where(tr_ref[s] > 0, j, nj - 1)

    grid_spec = pltpu.PrefetchScalarGridSpec(
        num_scalar_prefetch=4,
        grid=(n_tiles, nj),
        in_specs=[pl.BlockSpec(memory_space=pl.ANY), pl.BlockSpec(memory_space=pl.ANY),
                  pl.BlockSpec((None, d, tf), lambda s, j, te_r, tr_r, *_: (te_r[s], 0, jj(s, j, tr_r))),
                  pl.BlockSpec((None, d, tf), lambda s, j, te_r, tr_r, *_: (te_r[s], 0, jj(s, j, tr_r))),
                  pl.BlockSpec((None, tf, d), lambda s, j, te_r, tr_r, *_: (te_r[s], jj(s, j, tr_r), 0))],
        out_specs=pl.BlockSpec(memory_space=pl.ANY),
        scratch_shapes=[pltpu.VMEM((r, d), BF16), pltpu.VMEM((r, d), F32), pltpu.VMEM((2, MOE_SUB, d), F32),
                        pltpu.VMEM((d, 2 * tf), BF16), pltpu.VMEM((tf, d), BF16),
                        pltpu.SemaphoreType.DMA((2,)), pltpu.SemaphoreType.DMA(())],
    )
    return pl.pallas_call(
        _moe_kernel,
        grid_spec=grid_spec,
        out_shape=jax.ShapeDtypeStruct((n_tiles * r, d), F32),
        compiler_params=_cparams(("arbitrary", "arbitrary")),
        name="moe_experts",
    )(te, tr, tsp, src, h_a, h_b, wg, wu, wd)


def _combine_kernel(s1_ref, s2_ref, x_ref, g2_ref, r_ref, gf_ref, e_hbm, o_ref, st1, st2, sem):
    nb, tt, d = x_ref.shape
    tm = nb * tt
    step = pl.program_id(0) * pl.num_programs(1) + pl.program_id(1)
    nsteps = pl.num_programs(0) * pl.num_programs(1)
    slot = step % 2

    def waits(sl):
        pltpu.make_async_copy(e_hbm.at[pl.ds(0, tm), :], st1.at[sl], sem.at[0, sl]).wait()
        pltpu.make_async_copy(e_hbm.at[pl.ds(0, tm), :], st2.at[sl], sem.at[1, sl]).wait()

    def start_row(base, sl, r, parity=0):
        _row_gather_copy(e_hbm, s1_ref[base + r], st1.at[sl], r, sem.at[0, sl]).start(priority=parity)
        _row_gather_copy(e_hbm, s2_ref[base + r], st2.at[sl], r,
                         sem.at[1, sl]).start(priority=DMA_PRIORITIES_V7X - 1 - parity)

    @pl.when(step == 0)
    def _():
        def body(r, carry):
            start_row(0, 0, r)
            return carry
        lax.fori_loop(0, tm, body, 0, unroll=GATHER_UNROLL)

    waits(slot)
    nxt = jnp.minimum(step + 1, nsteps - 1) * tm
    for r in range(tm):
        start_row(nxt, 1 - slot, r, parity=r % DMA_PRIORITIES_V7X)

    f = r_ref[:, 2:3] * st1[slot] + r_ref[:, 3:4] * st2[slot]
    x2 = x_ref[...] + g2_ref[...] * f.reshape(nb, tt, d)
    rs = lax.rsqrt(jnp.mean(x2 * x2, axis=-1, keepdims=True) + EPS)
    o_ref[...] = (x2 * rs) * gf_ref[...]

    @pl.when(step == nsteps - 1)
    def _():
        waits(1 - slot)


def _combine(x, mod, l, route, slot1, slot2, e_out, gf):
    b, t, d = x.shape
    nb, tt = _tile(b, t, ROWS_COMBINE)
    tm = nb * tt
    nt = t // tt
    grid_spec = pltpu.PrefetchScalarGridSpec(
        num_scalar_prefetch=2,
        grid=(b // nb, nt),
        in_specs=[pl.BlockSpec((nb, tt, d), lambda bi, i, a, c: (bi, i, 0)),
                  _mod_spec(nb, d, 5, l),
                  pl.BlockSpec((tm, ROUTE_LANES), lambda bi, i, a, c: (bi * nt + i, 0)),
                  pl.BlockSpec((1, d), lambda bi, i, a, c: (0, 0)),
                  pl.BlockSpec(memory_space=pl.ANY)],
        out_specs=pl.BlockSpec((nb, tt, d), lambda bi, i, a, c: (bi, i, 0)),
        scratch_shapes=[pltpu.VMEM((2, tm, d), F32), pltpu.VMEM((2, tm, d), F32), pltpu.SemaphoreType.DMA((2, 2))],
    )
    return pl.pallas_call(
        _combine_kernel,
        grid_spec=grid_spec,
        out_shape=jax.ShapeDtypeStruct((b, t, d), F32),
        compiler_params=_cparams(("arbitrary", "arbitrary")),
        name="moe_combine_norm",
    )(slot1, slot2, x, mod, route, gf, e_out)


def _lookup(table, idx):
    out = jnp.zeros_like(idx)
    for e in range(N_EXPERTS):
        out = jnp.where(idx == e, table[e], out)
    return out


def _floor_div(a, b):
    q = jnp.floor(a.astype(F32) / b.astype(F32)).astype(I32)
    q = jnp.where((q + 1) * b <= a, q + 1, q)
    return jnp.where(q * b > a, q - 1, q)


def _moe_layout(route_t, counts_f, counts_a_f, n_a):
    n = route_t.shape[1]
    r, sub = MOE_TILE_ROWS, MOE_SUB
    n_tiles = -(-TOP_K * n // r) + N_EXPERTS
    i1 = route_t[0].astype(I32)
    i2 = route_t[1].astype(I32)
    r1 = route_t[4].astype(I32)
    r2 = route_t[5].astype(I32)
    counts = counts_f[0, :N_EXPERTS].astype(I32)
    nt = (counts + r - 1) // r
    ntc = jnp.maximum(nt, 1)
    sz = jnp.maximum(((_floor_div(counts + ntc - 1, ntc) + sub - 1) // sub) * sub, sub)
    cum = jnp.cumsum(nt)
    tstart = cum - nt
    total = cum[-1]
    s_idx = jnp.arange(n_tiles, dtype=I32)
    te_raw = jnp.minimum(jnp.sum((s_idx[:, None] >= cum[None, :]).astype(I32), axis=1), N_EXPERTS - 1)
    used = s_idx < total
    last_e = jnp.max(jnp.where(used, te_raw, 0))
    te = jnp.where(used, te_raw, last_e)
    sz_t = _lookup(sz, te)
    k_in = s_idx - _lookup(tstart, te)
    tr = jnp.where(used, jnp.clip(_lookup(counts, te) - k_in * sz_t, 0, sz_t), 0).astype(I32)
    counts_a = counts_a_f[0, :N_EXPERTS].astype(I32)
    tsp = jnp.clip(_lookup(counts_a, te) - k_in * sz_t, 0, tr).astype(I32)

    def slot(ei, ri):
        sz_i = _lookup(sz, ei)
        k = _floor_div(ri, sz_i)
        return (_lookup(tstart, ei) + k) * r + (ri - k * sz_i)
    slot1, slot2 = slot(i1, r1), slot(i2, r2)
    tok = jnp.arange(n, dtype=I32)
    tok = jnp.where(tok < n_a, tok, tok - n_a)
    src = jnp.zeros((n_tiles * r,), I32).at[jnp.concatenate([slot1, slot2])].set(jnp.concatenate([tok, tok]))
    return te, tr, tsp, src, slot1, slot2, n_tiles


def kernel(x_prompt, x_sample, c_prompt, c_sample, state_pool, state_ssm_re, state_ssm_im, w_ada, b_ada, norm_mix, norm_ffn, w_in, w_pool, pool_scale, ssm_lam_re, ssm_lam_im, ssm_log_step, ssm_b_re, ssm_b_im, ssm_c_re, ssm_c_im, ssm_d, ssm_w_glu, w_out, ffn_w_gate, ffn_w_up, ffn_w_down, moe_w_router, moe_b_router, moe_w_gate, moe_w_up, moe_w_down, norm_final):
    bp, tp, d = x_prompt.shape
    bs, ts, _ = x_sample.shape
    depth = w_ada.shape[0]
    c_pool = w_pool.shape[1] * w_pool.shape[2]
    assert tp % CHUNK == 0 and ts <= CHUNK

    m_rows = -(-(bp + bs) // SUBLANES_V7X) * SUBLANES_V7X
    c_all = jnp.concatenate([c_prompt, c_sample, jnp.zeros((m_rows - bp - bs, d), F32)], axis=0)
    mod_all = _ada(c_all, w_ada, b_ada)

    mod_p = mod_all[:, :bp].reshape(depth, bp, 1, 6 * d)
    mod_s = mod_all[:, bp:bp + bs].reshape(depth, bs, 1, 6 * d)
    w_in_b, w_out_b = w_in.astype(BF16), w_out.astype(BF16)
    wp_b, wglu_b = w_pool.astype(BF16), ssm_w_glu.astype(BF16)
    ps = pool_scale.reshape(depth, 1, c_pool).astype(F32)
    dsk = ssm_d.reshape(depth, 1, d - c_pool).astype(F32)
    k_t, b_t, c_t, p16, q16, p8b, q8b = jax.vmap(lambda *a: _s5_params(*a, ts, bs))(
        ssm_lam_re, ssm_lam_im, ssm_log_step, ssm_b_re, ssm_b_im, ssm_c_re, ssm_c_im)
    x0_t = jnp.concatenate([state_ssm_re, state_ssm_im], axis=-1).astype(F32).transpose(0, 2, 3, 1)
    st_p = jnp.zeros((1, bp, POOL_BUF, c_pool), F32)
    st_s = state_pool.astype(F32)

    xp, xs = x_prompt, x_sample
    pool_p, re_p, im_p, pool_s, re_s, im_s = [], [], [], [], [], []
    y_p = y_s = None
    for l in range(depth):
        gm = norm_mix[l].reshape(1, d)
        gn = norm_ffn[l].reshape(1, d)

        up_all = _proj_in(xp, mod_p, l, gm, w_in_b)
        us_all = _proj_in(xs, mod_s, l, gm, w_in_b)
        y_ssm_p, xend = _s5_prompt(up_all, l, k_t, b_t, c_t, p16, q16, c_pool)
        ys2, x1_t = _s5_sample(us_all.reshape(bs * ts, d), l, x0_t, k_t, b_t, c_t, p8b, q8b, ts, c_pool)
        y_ssm_s = ys2.reshape(bs, ts, d - c_pool)
        ffn_f32 = (ffn_w_gate[l // 2], ffn_w_up[l // 2], ffn_w_down[l // 2]) if l % 2 == 0 else ()
        xp, new_pool_p, ffn_bf16 = _mix(xp, up_all, y_ssm_p, st_p, 0, mod_p, l, wp_b, ps, dsk, wglu_b, w_out_b, 0,
                                        cast=ffn_f32)
        xs, new_pool_s, _ = _mix(xs, us_all, y_ssm_s, st_s, l, mod_s, l, wp_b, ps, dsk, wglu_b, w_out_b, PAST_LEN)
        pool_p.append(new_pool_p)
        pool_s.append(new_pool_s)
        re_p.append(xend[..., :SSM_P])
        im_p.append(xend[..., SSM_P:])
        re_s.append(x1_t[:, :SSM_P, :].transpose(2, 0, 1))
        im_s.append(x1_t[:, SSM_P:, :].transpose(2, 0, 1))

        i = l // 2
        if l % 2 == 0:
            wg, wu, wd = ffn_bf16
            xp = _ffn(xp, mod_p, l, gn, wg, wu, wd)
            xs = _ffn(xs, mod_s, l, gn, wg, wu, wd)
        else:
            ne = moe_w_router.shape[-1]
            wr32 = jnp.pad(moe_w_router[i].astype(F32), ((0, 0), (0, ROUTE_LANES - ne)))
            wr_hi = wr32.astype(BF16)
            wr = jnp.stack([wr_hi, (wr32 - wr_hi.astype(F32)).astype(BF16)])
            br = jnp.pad(moe_b_router[i].astype(F32), (0, ROUTE_LANES - ne)).reshape(1, ROUTE_LANES)
            zero_counts = jnp.zeros((1, ROUTE_LANES), F32)
            h_p, route_p, rt_p, cnt_p = _route(xp, mod_p, l, gn, wr, br, zero_counts)
            h_s, route_s, rt_s, cnt_s = _route(xs, mod_s, l, gn, wr, br, cnt_p)
            n_p = bp * tp
            te, tr, tsp, src, slot1, slot2, n_tiles = _moe_layout(jnp.concatenate([rt_p, rt_s], axis=1), cnt_s, cnt_p,
                                                                  n_p)
            e_out = _moe_experts(h_p, h_s, te, tr, tsp, src, moe_w_gate[i], moe_w_up[i], moe_w_down[i], n_tiles)
            last = l == depth - 1
            assert last, "the combine kernel also applies the final norm"
            gf = norm_final.reshape(1, d)
            y_p = _combine(xp, mod_p, l, route_p, slot1[:n_p], slot2[:n_p], e_out, gf)
            y_s = _combine(xs, mod_s, l, route_s, slot1[n_p:], slot2[n_p:], e_out, gf)

    return (y_p, y_s, jnp.stack(pool_p), jnp.stack(re_p), jnp.stack(im_p),
            jnp.stack(pool_s), jnp.stack(re_s), jnp.stack(im_s))
```

```python
import functools
import math

import jax
import jax.numpy as jnp
import numpy as np
from jax import lax
from jax.experimental import pallas as pl
from jax.experimental.pallas import tpu as pltpu

F32 = jnp.float32
BF16 = jnp.bfloat16
I32 = jnp.int32

EPS = 1e-6
POOL_WINDOWS = (2, 4, 8, 16)
POOL_BUF = max(POOL_WINDOWS) - 1
SUBLANES_V7X = 8
BF16_SUBLANES_V7X = 16
LANES_V7X = 128
HALO = -(-POOL_BUF // SUBLANES_V7X) * SUBLANES_V7X
SSM_H = 16
SSM_P = 64
CHUNK = 16
CW = CHUNK * SSM_H
SW = 2 * SSM_P
N_EXPERTS = 8
TOP_K = 2
PAST_LEN = 16384

VMEM_LIMIT_V7X = 56 * 1024 * 1024
ROWS_IN = 512
ROWS_MIX = 256
ROWS_FFN = 512
FF_TILE = 512
ROWS_ROUTE = 512
MOE_TILE_ROWS = 2560
MOE_SUB = 128
MOE_BLK_BIG = 1024
MOE_BLK = 512
GATHER_UNROLL = 8
DMA_PRIORITIES_V7X = 2
MOE_FF_TILE = 256
ROWS_COMBINE = 512
ADA_TN = 2048


def _cparams(sem):
    return pltpu.CompilerParams(dimension_semantics=sem, vmem_limit_bytes=VMEM_LIMIT_V7X)


def _tile(b, t, rows):
    if t >= rows:
        assert t % rows == 0
        return 1, rows
    nb = max(1, min(b, rows // t))
    assert b % nb == 0
    return nb, t


def _modnorm(x, g, s, sh):
    r = lax.rsqrt(jnp.mean(x * x, axis=-1, keepdims=True) + EPS)
    return (x * r) * g * (1.0 + s) + sh


def _dot(a, b):
    return jnp.dot(a, b, preferred_element_type=F32)


def _mod_spec(nb, d, k, l):
    return pl.BlockSpec((None, nb, 1, d), lambda b, i, *_: (l, b, 0, k))


def _layer_spec(shape, l):
    return pl.BlockSpec((None,) + tuple(shape), lambda *_: (l,) + (0,) * len(shape))


def _ada_kernel(c_ref, w_ref, b_ref, o_ref):
    c = c_ref[...]
    sc = c * jax.nn.sigmoid(c)
    o_ref[0] = _dot(sc.astype(BF16), w_ref[0].astype(BF16)) + b_ref[0]


def _ada(c_all, w_ada, b_ada):
    m, d = c_all.shape
    depth, _, n6 = w_ada.shape
    tn = math.gcd(ADA_TN, n6)
    return pl.pallas_call(
        _ada_kernel,
        grid=(depth, n6 // tn),
        in_specs=[pl.BlockSpec((m, d), lambda l, n: (0, 0)),
                  pl.BlockSpec((1, d, tn), lambda l, n: (l, 0, n)),
                  pl.BlockSpec((1, 1, tn), lambda l, n: (l, 0, n))],
        out_specs=pl.BlockSpec((1, m, tn), lambda l, n: (l, 0, n)),
        out_shape=jax.ShapeDtypeStruct((depth, m, n6), F32),
        compiler_params=_cparams(("arbitrary", "arbitrary")),
        name="adaln_mod",
    )(c_all, w_ada, b_ada.reshape(depth, 1, n6))


def _in_kernel(x_ref, g_ref, s_ref, sh_ref, w_ref, u_ref):
    nb, tt, d = x_ref.shape
    h = _modnorm(x_ref[...], g_ref[...], s_ref[...], sh_ref[...])
    u = _dot(h.reshape(nb * tt, d).astype(BF16), w_ref[...])
    u_ref[...] = u.reshape(nb, tt, u_ref.shape[-1])


def _proj_in(x, mod, l, g, w_bf16):
    b, t, d = x.shape
    n = w_bf16.shape[-1]
    nb, tt = _tile(b, t, ROWS_IN)
    return pl.pallas_call(
        _in_kernel,
        grid=(b // nb, t // tt),
        in_specs=[pl.BlockSpec((nb, tt, d), lambda bi, i: (bi, i, 0)),
                  pl.BlockSpec((1, d), lambda bi, i: (0, 0)),
                  _mod_spec(nb, d, 1, l), _mod_spec(nb, d, 0, l),
                  _layer_spec((d, n), l)],
        out_specs=pl.BlockSpec((nb, tt, n), lambda bi, i: (bi, i, 0)),
        out_shape=jax.ShapeDtypeStruct((b, t, n), F32),
        compiler_params=_cparams(("arbitrary", "arbitrary")),
        name="norm_proj_in",
    )(x, g, mod, mod, w_bf16)


GROUPS_PER_LANE_TILE = LANES_V7X // SSM_H


def _toeplitz_placement():
    toe = np.zeros((CHUNK, CW, CW), np.float32)
    for s in range(CHUNK):
        for h in range(SSM_H):
            for t in range(s, CHUNK):
                toe[t, (t - s) * SSM_H + h, s * SSM_H + h] = 1.0
    return jnp.asarray(toe)


def _s5_params(lam_re, lam_im, log_step, b_re, b_im, c_re, c_im, ts, bs):
    hp = lax.Precision.HIGHEST
    toe = _toeplitz_placement()
    lr, li = lam_re.astype(F32), lam_im.astype(F32)
    step = jnp.exp(log_step.astype(F32))[:, None]
    mag = jnp.exp(lr * step)
    a_re = mag * jnp.cos(li * step)
    a_im = mag * jnp.sin(li * step)
    den = lr * lr + li * li
    nr = a_re - 1.0
    f_re = (nr * lr + a_im * li) / den
    f_im = (a_im * lr - nr * li) / den
    br, bi = b_re.astype(F32), b_im.astype(F32)
    bb_re = f_re[..., None] * br - f_im[..., None] * bi
    bb_im = f_re[..., None] * bi + f_im[..., None] * br
    cr, ci = c_re.astype(F32), c_im.astype(F32)
    g = lr.shape[0]

    pw_re, pw_im = [jnp.ones_like(a_re)], [jnp.zeros_like(a_im)]
    for _ in range(CHUNK):
        pr, pi = pw_re[-1], pw_im[-1]
        pw_re.append(pr * a_re - pi * a_im)
        pw_im.append(pr * a_im + pi * a_re)
    pw_re, pw_im = jnp.stack(pw_re), jnp.stack(pw_im)

    bbt_re, bbt_im = bb_re.transpose(0, 2, 1), bb_im.transpose(0, 2, 1)
    w_re = pw_re[:CHUNK, :, None, :] * bbt_re - pw_im[:CHUNK, :, None, :] * bbt_im
    w_im = pw_re[:CHUNK, :, None, :] * bbt_im + pw_im[:CHUNK, :, None, :] * bbt_re
    m = (jnp.einsum('gop,lgip->goli', cr, w_re, precision=hp)
         - jnp.einsum('gop,lgip->goli', ci, w_im, precision=hp)).reshape(g, SSM_H, CW)
    k_t = jnp.einsum('gok,tkn->gton', m.astype(BF16), toe.astype(BF16),
                     preferred_element_type=F32).reshape(g, CW, CW)

    rev_re, rev_im = pw_re[:CHUNK][::-1], pw_im[:CHUNK][::-1]
    ar = jnp.repeat(rev_re.transpose(1, 2, 0), SSM_H, axis=-1)
    ai = jnp.repeat(rev_im.transpose(1, 2, 0), SSM_H, axis=-1)
    xr = jnp.tile(bb_re, (1, 1, CHUNK))
    xi = jnp.tile(bb_im, (1, 1, CHUNK))
    bt_re = ar * xr - ai * xi
    bt_im = ar * xi + ai * xr
    b_t = jnp.concatenate([bt_re, bt_im, bt_im, bt_re], axis=1)

    qr = pw_re[1:].transpose(1, 0, 2)[:, :, None, :]
    qi = pw_im[1:].transpose(1, 0, 2)[:, :, None, :]
    cm_re = cr[:, None] * qr - ci[:, None] * qi
    cm_im = -(cr[:, None] * qi + ci[:, None] * qr)
    c_t = jnp.concatenate([cm_re, cm_im], axis=-1).reshape(g, CW, SW)

    def coef(k):
        pr, pi = pw_re[k], pw_im[k]
        return (jnp.concatenate([pr, pr, pr, pr], axis=-1),
                jnp.concatenate([-pi, pi, pi, -pi], axis=-1))
    p16, q16 = coef(CHUNK)
    p8, q8 = coef(ts)
    p8b = jnp.broadcast_to(p8[:, :SW, None], (g, SW, bs))
    q8b = jnp.broadcast_to(q8[:, :SW, None], (g, SW, bs))
    return k_t.astype(BF16), b_t.astype(BF16), c_t.astype(BF16), p16, q16, p8b, q8b


def _s5_prompt_kernel(us_ref, kt_ref, bt_ref, ct_ref, p_ref, q_ref, y_ref, xe_ref, ut, yt, scan_x, scan_xs, s_scr):
    gt = ut.shape[0]
    nchunk = ut.shape[-1]
    for t in range(CHUNK):
        slab = us_ref[0, pl.ds(t, nchunk, stride=CHUNK), :]
        ut[:, t] = slab.T.reshape(gt, SSM_H, nchunk).astype(BF16)
    for gi in range(gt):
        buz = _dot(bt_ref[gi], ut[gi].reshape(CW, nchunk))
        buz_t = buz.T
        scan_x[pl.ds(gi, nchunk, stride=gt), :] = buz_t[:, :SW]
        scan_xs[pl.ds(gi, nchunk, stride=gt), :] = buz_t[:, SW:]
    px, pxs = p_ref[:, :SW], p_ref[:, SW:]
    qx, qxs = q_ref[:, :SW], q_ref[:, SW:]
    x = jnp.zeros((gt, SW), F32)
    xs = jnp.zeros((gt, SW), F32)
    for c in range(nchunk):
        lo, hi = c * gt, (c + 1) * gt
        s_scr[lo:hi, :] = x
        x, xs = (px * x + qx * xs + scan_x[lo:hi, :],
                 pxs * xs + qxs * x + scan_xs[lo:hi, :])
    xe_ref[0] = x
    for gi in range(gt):
        st = s_scr[pl.ds(gi, nchunk, stride=gt), :].T.astype(BF16)
        y_t = _dot(kt_ref[gi], ut[gi].reshape(CW, nchunk)) + _dot(ct_ref[gi], st)
        yt[gi] = y_t.reshape(CHUNK, SSM_H, nchunk)
    for t in range(CHUNK):
        y_ref[0, pl.ds(t, nchunk, stride=CHUNK), :] = yt[:, t].reshape(gt * SSM_H, nchunk).T


def _s5_prompt(u, l, k_t, b_t, c_t, p16, q16, c_off):
    b, t, _ = u.shape
    g = k_t.shape[1]
    gt = GROUPS_PER_LANE_TILE
    lt = gt * SSM_H
    nchunk = t // CHUNK
    off = c_off // lt

    def wspec(r, c):
        return pl.BlockSpec((None, gt, r, c), lambda bi, j: (l, j, 0, 0))
    return pl.pallas_call(
        _s5_prompt_kernel,
        grid=(b, g // gt),
        in_specs=[pl.BlockSpec((1, t, lt), lambda bi, j: (bi, 0, off + j)),
                  wspec(CW, CW), wspec(2 * SW, CW), wspec(CW, SW),
                  pl.BlockSpec((None, gt, 2 * SW), lambda bi, j: (l, j, 0)),
                  pl.BlockSpec((None, gt, 2 * SW), lambda bi, j: (l, j, 0))],
        out_specs=[pl.BlockSpec((1, t, lt), lambda bi, j: (bi, 0, j)),
                   pl.BlockSpec((1, gt, SW), lambda bi, j: (bi, j, 0))],
        out_shape=[jax.ShapeDtypeStruct((b, t, g * SSM_H), F32), jax.ShapeDtypeStruct((b, g, SW), F32)],
        scratch_shapes=[pltpu.VMEM((gt, CHUNK, SSM_H, nchunk), BF16), pltpu.VMEM((gt, CHUNK, SSM_H, nchunk), F32),
                        pltpu.VMEM((nchunk * gt, SW), F32), pltpu.VMEM((nchunk * gt, SW), F32),
                        pltpu.VMEM((nchunk * gt, SW), F32)],
        compiler_params=_cparams(("arbitrary", "arbitrary")),
        name="s5_prompt",
    )(u, k_t, b_t, c_t, p16, q16)


def _s5_sample_kernel(us_ref, x0_ref, kt_ref, bt_ref, ct_ref, p_ref, q_ref, y_ref, x1_ref, ut, yt, *, ts):
    gt = ut.shape[0]
    bs = ut.shape[-1]
    w = ts * SSM_H
    for t in range(ts):
        slab = us_ref[pl.ds(t, bs, stride=ts), :]
        ut[:, t] = slab.T.reshape(gt, SSM_H, bs).astype(BF16)
    for gi in range(gt):
        ug = ut[gi].reshape(w, bs)
        x0 = x0_ref[gi]
        x0s = jnp.concatenate([x0[SSM_P:], x0[:SSM_P]], axis=0)
        y_t = _dot(kt_ref[gi, :w, :w], ug) + _dot(ct_ref[gi, :w, :], x0.astype(BF16))
        x1_ref[gi] = p_ref[gi] * x0 + q_ref[gi] * x0s + _dot(bt_ref[gi, :SW, CW - w:], ug)
        yt[gi] = y_t.reshape(ts, SSM_H, bs)
    for t in range(ts):
        y_ref[pl.ds(t, bs, stride=ts), :] = yt[:, t].reshape(gt * SSM_H, bs).T


def _s5_sample(u2, l, x0_t, k_t, b_t, c_t, p8b, q8b, ts, c_off):
    rows, _ = u2.shape
    _, g, _, bs = x0_t.shape
    gt = GROUPS_PER_LANE_TILE
    lt = gt * SSM_H
    off = c_off // lt

    def wspec(r, c):
        return pl.BlockSpec((None, gt, r, c), lambda j: (l, j, 0, 0))
    return pl.pallas_call(
        functools.partial(_s5_sample_kernel, ts=ts),
        grid=(g // gt,),
        in_specs=[pl.BlockSpec((rows, lt), lambda j: (0, off + j)),
                  wspec(SW, bs), wspec(CW, CW), wspec(2 * SW, CW), wspec(CW, SW), wspec(SW, bs), wspec(SW, bs)],
        out_specs=[pl.BlockSpec((rows, lt), lambda j: (0, j)), pl.BlockSpec((gt, SW, bs), lambda j: (j, 0, 0))],
        out_shape=[jax.ShapeDtypeStruct((rows, g * SSM_H), F32), jax.ShapeDtypeStruct((g, SW, bs), F32)],
        scratch_shapes=[pltpu.VMEM((gt, ts, SSM_H, bs), BF16), pltpu.VMEM((gt, ts, SSM_H, bs), F32)],
        compiler_params=_cparams(("arbitrary",)),
        name="s5_sample",
    )(u2, x0_t, k_t, b_t, c_t, p8b, q8b)


def _mix_kernel(x_ref, up_ref, us_ref, ys_ref, st_ref, g1_ref, wp_ref, ps_ref, dsk_ref, wglu_ref,
                wout_ref, *rest, start_pos, carry, ncast):
    cast_in, (o_ref, np_ref), cast_out, ext = rest[:ncast], rest[ncast:ncast + 2], rest[ncast + 2:-1], rest[-1]
    for src, dst in zip(cast_in, cast_out):
        dst[...] = src[...].astype(BF16)
    nb, tt, d = x_ref.shape
    c = up_ref.shape[-1]
    pg = c // len(POOL_WINDOWS)
    rows = nb * tt
    i = pl.program_id(1)

    @pl.when(i == 0)
    def _():
        ext[:, HALO - POOL_BUF:HALO, :] = st_ref[...]
    ext[:, HALO:HALO + tt, :] = up_ref[...]
    np_ref[...] = ext[:, HALO + tt - POOL_BUF:HALO + tt, :]

    pos = start_pos + i * tt + lax.broadcasted_iota(I32, (1, tt, 1), 1)
    outs = []
    for gidx, w in enumerate(POOL_WINDOWS):
        c0 = gidx * pg
        cur = ext[:, HALO:HALO + tt, c0:c0 + pg]
        s = cur
        for k in range(1, w):
            s = s + ext[:, HALO - k:HALO - k + tt, c0:c0 + pg]
        cnt = jnp.minimum(pos + 1, w).astype(F32)
        dd = s / cnt - cur
        yg = _dot(dd.reshape(rows, pg).astype(BF16), wp_ref[gidx])
        outs.append(yg * ps_ref[:, c0:c0 + pg])
    y_pool = jnp.concatenate(outs, axis=-1)

    if carry:
        ext[:, 0:HALO, :] = ext[:, tt:tt + HALO, :]

    us = us_ref[...].reshape(rows, c)
    yf = ys_ref[...].reshape(rows, c) + dsk_ref[...] * us
    gl = jax.nn.gelu(yf, approximate=True)
    o = gl * jax.nn.sigmoid(_dot(gl.astype(BF16), wglu_ref[...]))
    mo = _dot(y_pool.astype(BF16), wout_ref[:c, :]) + _dot(o.astype(BF16), wout_ref[c:, :])
    o_ref[...] = x_ref[...] + g1_ref[...] * mo.reshape(nb, tt, d)


def _mix(x, u, y_ssm, pool_state, l_state, mod, l, wp, ps, dsk, wglu, wout, start_pos, cast=()):
    b, t, d = x.shape
    c = y_ssm.shape[-1]
    nb, tt = _tile(b, t, ROWS_MIX)
    nt = t // tt
    assert nt == 1 or tt >= HALO
    nsteps = (b // nb) * nt
    cast_specs = []
    for a in cast:
        blk = -(-(-(-a.shape[0] // nsteps)) // BF16_SUBLANES_V7X) * BF16_SUBLANES_V7X
        last = -(-a.shape[0] // blk) - 1
        cast_specs.append(pl.BlockSpec((blk, a.shape[1]), lambda bi, i, last=last: (jnp.minimum(bi * nt + i, last), 0)))
    outs = pl.pallas_call(
        functools.partial(_mix_kernel, start_pos=start_pos, carry=nt > 1, ncast=len(cast)),
        grid=(b // nb, nt),
        in_specs=[pl.BlockSpec((nb, tt, d), lambda bi, i: (bi, i, 0)),
                  pl.BlockSpec((nb, tt, c), lambda bi, i: (bi, i, 0)),
                  pl.BlockSpec((nb, tt, c), lambda bi, i: (bi, i, 1)),
                  pl.BlockSpec((nb, tt, c), lambda bi, i: (bi, i, 0)),
                  pl.BlockSpec((None, nb, POOL_BUF, c), lambda bi, i: (l_state, bi, 0, 0)),
                  _mod_spec(nb, d, 2, l),
                  _layer_spec(wp.shape[1:], l), _layer_spec((1, c), l), _layer_spec((1, c), l),
                  _layer_spec((c, c), l), _layer_spec((d, d), l)] + cast_specs,
        out_specs=[pl.BlockSpec((nb, tt, d), lambda bi, i: (bi, i, 0)),
                   pl.BlockSpec((nb, POOL_BUF, c), lambda bi, i: (bi, 0, 0))] + cast_specs,
        out_shape=[jax.ShapeDtypeStruct((b, t, d), F32), jax.ShapeDtypeStruct((b, POOL_BUF, c), F32)]
        + [jax.ShapeDtypeStruct(a.shape, BF16) for a in cast],
        scratch_shapes=[pltpu.VMEM((nb, HALO + tt, c), F32)],
        compiler_params=_cparams(("arbitrary", "arbitrary")),
        name="pool_glu_proj_out",
    )(x, u, u, y_ssm, pool_state, mod, wp, ps, dsk, wglu, wout, *cast)
    return outs[0], outs[1], tuple(outs[2:])


def _ffn_kernel(x_ref, g_ref, s_ref, sh_ref, g2_ref, wg_ref, wu_ref, wd_ref, o_ref, hb, acc, *, last_valid):
    nb, tt, d = x_ref.shape
    tf = wg_ref.shape[1]
    j = pl.program_id(2)
    nj = pl.num_programs(2)

    @pl.when(j == 0)
    def _():
        h = _modnorm(x_ref[...], g_ref[...], s_ref[...], sh_ref[...])
        hb[...] = h.reshape(nb * tt, d).astype(BF16)
        acc[...] = jnp.zeros_like(acc)

    def step(valid):
        h = hb[...]
        gate = _dot(h, wg_ref[...])
        up = _dot(h, wu_ref[...])
        a = gate * jax.nn.sigmoid(gate) * up
        wd = wd_ref[...]
        if valid < tf:
            a = jnp.where(lax.broadcasted_iota(I32, (1, tf), 1) < valid, a, 0.0)
            wd = jnp.where(lax.broadcasted_iota(I32, (tf, 1), 0) < valid, wd, jnp.zeros_like(wd))
        acc[...] += _dot(a.astype(BF16), wd)

    if last_valid == tf:
        step(tf)
    else:
        pl.when(j < nj - 1)(lambda: step(tf))
        pl.when(j == nj - 1)(lambda: step(last_valid))

    @pl.when(j == nj - 1)
    def _():
        o_ref[...] = x_ref[...] + g2_ref[...] * acc[...].reshape(nb, tt, d)


def _ffn(x, mod, l, g, wg, wu, wd):
    b, t, d = x.shape
    ff = wg.shape[1]
    nb, tt = _tile(b, t, ROWS_FFN)
    tf = min(FF_TILE, ff)
    nj = -(-ff // tf)
    return pl.pallas_call(
        functools.partial(_ffn_kernel, last_valid=ff - (nj - 1) * tf),
        grid=(b // nb, t // tt, nj),
        in_specs=[pl.BlockSpec((nb, tt, d), lambda bi, i, j: (bi, i, 0)),
                  pl.BlockSpec((1, d), lambda bi, i, j: (0, 0)),
                  _mod_spec(nb, d, 4, l), _mod_spec(nb, d, 3, l), _mod_spec(nb, d, 5, l),
                  pl.BlockSpec((d, tf), lambda bi, i, j: (0, j)),
                  pl.BlockSpec((d, tf), lambda bi, i, j: (0, j)),
                  pl.BlockSpec((tf, d), lambda bi, i, j: (j, 0))],
        out_specs=pl.BlockSpec((nb, tt, d), lambda bi, i, j: (bi, i, 0)),
        out_shape=jax.ShapeDtypeStruct((b, t, d), F32),
        scratch_shapes=[pltpu.VMEM((nb * tt, d), BF16), pltpu.VMEM((nb * tt, d), F32)],
        compiler_params=_cparams(("arbitrary", "arbitrary", "arbitrary")),
        name="dense_swiglu",
    )(x, g, mod, mod, mod, wg, wu, wd)


ROUTE_LANES = LANES_V7X
ROUTE_ROWS = SUBLANES_V7X


def _route_kernel(x_ref, g_ref, s_ref, sh_ref, wr_ref, br_ref, cin_ref, h_ref, r_ref, rt_ref, cout_ref, carry):
    nb, tt, d = x_ref.shape
    tm = nb * tt
    first = jnp.logical_and(pl.program_id(0) == 0, pl.program_id(1) == 0)

    @pl.when(first)
    def _():
        carry[...] = cin_ref[...]

    h = _modnorm(x_ref[...], g_ref[...], s_ref[...], sh_ref[...]).reshape(tm, d)
    h_ref[...] = h
    h_hi = h.astype(BF16)
    h_lo = (h - h_hi.astype(F32)).astype(BF16)
    logits = _dot(h_hi, wr_ref[0]) + (_dot(h_hi, wr_ref[1]) + _dot(h_lo, wr_ref[0])) + br_ref[...]
    lane = lax.broadcasted_iota(I32, (tm, ROUTE_LANES), 1)
    neg = jnp.float32(-jnp.inf)
    l1 = jnp.where(lane < N_EXPERTS, logits, neg)
    m1 = jnp.max(l1, axis=-1, keepdims=True)
    i1 = jnp.min(jnp.where(l1 == m1, lane, ROUTE_LANES), axis=-1, keepdims=True)
    l2 = jnp.where(lane == i1, neg, l1)
    m2 = jnp.max(l2, axis=-1, keepdims=True)
    i2 = jnp.min(jnp.where(l2 == m2, lane, ROUTE_LANES), axis=-1, keepdims=True)
    e2 = jnp.exp(m2 - m1)
    den = 1.0 + e2
    p1 = 1.0 / den
    p2 = e2 / den

    onehot = jnp.logical_or(lane == i1, lane == i2).astype(F32)
    rr = lax.broadcasted_iota(I32, (tm, tm), 0)
    cc = lax.broadcasted_iota(I32, (tm, tm), 1)
    before = (cc < rr).astype(BF16)
    rank = _dot(before, onehot.astype(BF16)) + carry[...]
    r1 = jnp.sum(jnp.where(lane == i1, rank, 0.0), axis=-1, keepdims=True)
    r2 = jnp.sum(jnp.where(lane == i2, rank, 0.0), axis=-1, keepdims=True)
    carry[...] += jnp.sum(onehot, axis=0, keepdims=True)

    out = jnp.where(lane == 0, i1.astype(F32), 0.0)
    out = jnp.where(lane == 1, i2.astype(F32), out)
    out = jnp.where(lane == 2, p1, out)
    out = jnp.where(lane == 3, p2, out)
    out = jnp.where(lane == 4, r1, out)
    out = jnp.where(lane == 5, r2, out)
    r_ref[...] = out
    rt_ref[...] = out.T[:ROUTE_ROWS, :]
    cout_ref[...] = carry[...]


def _route(x, mod, l, g, wr_pad, br_pad, counts_in):
    b, t, d = x.shape
    nb, tt = _tile(b, t, ROWS_ROUTE)
    tm = nb * tt
    nt = t // tt
    const2 = lambda bi, i: (0, 0)
    return pl.pallas_call(
        _route_kernel,
        grid=(b // nb, nt),
        in_specs=[pl.BlockSpec((nb, tt, d), lambda bi, i: (bi, i, 0)),
                  pl.BlockSpec((1, d), const2),
                  _mod_spec(nb, d, 4, l), _mod_spec(nb, d, 3, l),
                  pl.BlockSpec((2, d, ROUTE_LANES), lambda bi, i: (0, 0, 0)),
                  pl.BlockSpec((1, ROUTE_LANES), const2),
                  pl.BlockSpec((1, ROUTE_LANES), const2)],
        out_specs=[pl.BlockSpec((tm, d), lambda bi, i: (bi * nt + i, 0)),
                   pl.BlockSpec((tm, ROUTE_LANES), lambda bi, i: (bi * nt + i, 0)),
                   pl.BlockSpec((ROUTE_ROWS, tm), lambda bi, i: (0, bi * nt + i)),
                   pl.BlockSpec((1, ROUTE_LANES), const2)],
        out_shape=[jax.ShapeDtypeStruct((b * t, d), F32),
                   jax.ShapeDtypeStruct((b * t, ROUTE_LANES), F32),
                   jax.ShapeDtypeStruct((ROUTE_ROWS, b * t), F32),
                   jax.ShapeDtypeStruct((1, ROUTE_LANES), F32)],
        scratch_shapes=[pltpu.VMEM((1, ROUTE_LANES), F32)],
        compiler_params=_cparams(("arbitrary", "arbitrary")),
        name="moe_route",
    )(x, g, mod, mod, wr_pad, br_pad, counts_in)


def _row_gather_copy(src_hbm, tok, dst, r, sem):
    return pltpu.make_async_copy(src_hbm.at[pl.ds(tok, 1), :], dst.at[pl.ds(r, 1), :], sem)


def _moe_kernel(te_ref, tr_ref, tsp_ref, src_ref, ha_hbm, hb_hbm, wg_ref, wu_ref, wd_ref, o_hbm,
                xb, acc, stage, wgu, wdb, gsem, osem):
    s = pl.program_id(0)
    j = pl.program_id(1)
    ns = pl.num_programs(0)
    nj = pl.num_programs(1)
    rows = tr_ref[s]
    nsub = (rows + MOE_SUB - 1) // MOE_SUB
    tile_rows = xb.shape[0]

    def out_copy():
        return pltpu.make_async_copy(acc, o_hbm.at[pl.ds(pl.multiple_of(s * tile_rows, MOE_SUB), tile_rows), :], osem)

    @pl.when(j == 0)
    def _():
        @pl.when(rows > 0)
        def _gather():
            def issue(sub, slot):
                base = s * tile_rows + sub * MOE_SUB
                na = jnp.clip(tsp_ref[s] - sub * MOE_SUB, 0, MOE_SUB)

                def start_from(h_hbm):
                    def body(r, carry):
                        _row_gather_copy(h_hbm, src_ref[base + r], stage.at[slot], r, gsem.at[slot]).start()
                        return carry
                    return body

                def start_group_from(h_hbm):
                    def body(g, carry):
                        for k in range(GATHER_UNROLL):
                            r = g * GATHER_UNROLL + k
                            _row_gather_copy(h_hbm, src_ref[base + r], stage.at[slot], r,
                                             gsem.at[slot]).start(priority=k % DMA_PRIORITIES_V7X)
                        return carry
                    return body

                @pl.when(na == MOE_SUB)
                def _():
                    lax.fori_loop(0, MOE_SUB // GATHER_UNROLL, start_group_from(ha_hbm), 0)

                @pl.when(na == 0)
                def _():
                    lax.fori_loop(0, MOE_SUB // GATHER_UNROLL, start_group_from(hb_hbm), 0)

                @pl.when(jnp.logical_and(na > 0, na < MOE_SUB))
                def _():
                    lax.fori_loop(0, na, start_from(ha_hbm), 0)
                    lax.fori_loop(na, MOE_SUB, start_from(hb_hbm), 0)

            issue(0, 0)

            def sub_body(sub, carry):
                slot = sub % 2

                @pl.when(sub + 1 < nsub)
                def _():
                    issue(sub + 1, 1 - slot)
                pltpu.make_async_copy(ha_hbm.at[pl.ds(0, MOE_SUB), :], stage.at[slot], gsem.at[slot]).wait()
                r0 = pl.multiple_of(sub * MOE_SUB, MOE_SUB)
                xb[pl.ds(r0, MOE_SUB), :] = stage[slot].astype(BF16)
                return carry
            lax.fori_loop(0, nsub, sub_body, 0)

        @pl.when(s > 0)
        def _():
            out_copy().wait()
        acc[...] = jnp.zeros_like(acc)

    @pl.when(rows > 0)
    def _():
        tf = wg_ref.shape[1]
        wgu[:, :tf] = wg_ref[...].astype(BF16)
        wgu[:, tf:] = wu_ref[...].astype(BF16)
        wdb[...] = wd_ref[...].astype(BF16)

        def block(r0, m):
            gu = _dot(xb[pl.ds(r0, m), :], wgu[...])
            gate, up = gu[:, :tf], gu[:, tf:]
            a = (gate * jax.nn.sigmoid(gate) * up).astype(BF16)
            acc[pl.ds(r0, m), :] += _dot(a, wdb[...])

        per_big = MOE_BLK_BIG // MOE_SUB
        per_blk = MOE_BLK // MOE_SUB
        nbig = nsub // per_big
        rem_big = nsub - nbig * per_big
        has_blk = rem_big >= per_blk
        rem = rem_big - jnp.where(has_blk, per_blk, 0)
        r_blk = pl.multiple_of(nbig * MOE_BLK_BIG, MOE_BLK)
        r_tail = pl.multiple_of(r_blk + jnp.where(has_blk, MOE_BLK, 0), MOE_SUB)

        def body(blk, carry):
            block(pl.multiple_of(blk * MOE_BLK_BIG, MOE_BLK_BIG), MOE_BLK_BIG)
            return carry
        lax.fori_loop(0, nbig, body, 0)
        pl.when(has_blk)(lambda: block(r_blk, MOE_BLK))
        for k in range(1, per_blk):
            @pl.when(rem == k)
            def _(k=k):
                block(r_tail, k * MOE_SUB)

    @pl.when(j == nj - 1)
    def _():
        out_copy().start()

        @pl.when(s == ns - 1)
        def _():
            out_copy().wait()


def _moe_experts(h_a, h_b, te, tr, tsp, src, wg, wu, wd, n_tiles):
    d = h_a.shape[1]
    assert h_a.shape[0] >= MOE_SUB
    e, _, ff = wg.shape
    tf = min(MOE_FF_TILE, ff)
    assert ff % tf == 0 and MOE_TILE_ROWS % MOE_BLK == 0 and MOE_BLK % MOE_SUB == 0 and MOE_BLK_BIG == 2 * MOE_BLK
    nj = ff // tf
    r = MOE_TILE_ROWS

    def jj(s, j, tr_ref):
        return jnp.where(tr_ref[s] > 0, j, nj - 1)

    grid_spec = pltpu.PrefetchScalarGridSpec(
        num_scalar_prefetch=4,
        grid=(n_tiles, nj),
        in_specs=[pl.BlockSpec(memory_space=pl.ANY), pl.BlockSpec(memory_space=pl.ANY),
                  pl.BlockSpec((None, d, tf), lambda s, j, te_r, tr_r, *_: (te_r[s], 0, jj(s, j, tr_r))),
                  pl.BlockSpec((None, d, tf), lambda s, j, te_r, tr_r, *_: (te_r[s], 0, jj(s, j, tr_r))),
                  pl.BlockSpec((None, tf, d), lambda s, j, te_r, tr_r, *_: (te_r[s], jj(s, j, tr_r), 0))],
        out_specs=pl.BlockSpec(memory_space=pl.ANY),
        scratch_shapes=[pltpu.VMEM((r, d), BF16), pltpu.VMEM((r, d), F32), pltpu.VMEM((2, MOE_SUB, d), F32),
                        pltpu.VMEM((d, 2 * tf), BF16), pltpu.VMEM((tf, d), BF16),
                        pltpu.SemaphoreType.DMA((2,)), pltpu.SemaphoreType.DMA(())],
    )
    return pl.pallas_call(
        _moe_kernel,
        grid_spec=grid_spec,
        out_shape=jax.ShapeDtypeStruct((n_tiles * r, d), F32),
        compiler_params=_cparams(("arbitrary", "arbitrary")),
        name="moe_experts",
    )(te, tr, tsp, src, h_a, h_b, wg, wu, wd)


def _combine_kernel(s1_ref, s2_ref, x_ref, g2_ref, r_ref, gf_ref, e_hbm, o_ref, st1, st2, sem):
    nb, tt, d = x_ref.shape
    tm = nb * tt
    step = pl.program_id(0) * pl.num_programs(1) + pl.program_id(1)
    nsteps = pl.num_programs(0) * pl.num_programs(1)
    slot = step % 2

    def waits(sl):
        pltpu.make_async_copy(e_hbm.at[pl.ds(0, tm), :], st1.at[sl], sem.at[0, sl]).wait()
        pltpu.make_async_copy(e_hbm.at[pl.ds(0, tm), :], st2.at[sl], sem.at[1, sl]).wait()

    def start_row(base, sl, r, parity=0):
        _row_gather_copy(e_hbm, s1_ref[base + r], st1.at[sl], r, sem.at[0, sl]).start(priority=parity)
        _row_gather_copy(e_hbm, s2_ref[base + r], st2.at[sl], r,
                         sem.at[1, sl]).start(priority=DMA_PRIORITIES_V7X - 1 - parity)

    @pl.when(step == 0)
    def _():
        def body(r, carry):
            start_row(0, 0, r)
            return carry
        lax.fori_loop(0, tm, body, 0, unroll=GATHER_UNROLL)

    waits(slot)
    nxt = jnp.minimum(step + 1, nsteps - 1) * tm
    for r in range(tm):
        start_row(nxt, 1 - slot, r, parity=r % DMA_PRIORITIES_V7X)

    f = r_ref[:, 2:3] * st1[slot] + r_ref[:, 3:4] * st2[slot]
    x2 = x_ref[...] + g2_ref[...] * f.reshape(nb, tt, d)
    rs = lax.rsqrt(jnp.mean(x2 * x2, axis=-1, keepdims=True) + EPS)
    o_ref[...] = (x2 * rs) * gf_ref[...]

    @pl.when(step == nsteps - 1)
    def _():
        waits(1 - slot)


def _combine(x, mod, l, route, slot1, slot2, e_out, gf):
    b, t, d = x.shape
    nb, tt = _tile(b, t, ROWS_COMBINE)
    tm = nb * tt
    nt = t // tt
    grid_spec = pltpu.PrefetchScalarGridSpec(
        num_scalar_prefetch=2,
        grid=(b // nb, nt),
        in_specs=[pl.BlockSpec((nb, tt, d), lambda bi, i, a, c: (bi, i, 0)),
                  _mod_spec(nb, d, 5, l),
                  pl.BlockSpec((tm, ROUTE_LANES), lambda bi, i, a, c: (bi * nt + i, 0)),
                  pl.BlockSpec((1, d), lambda bi, i, a, c: (0, 0)),
                  pl.BlockSpec(memory_space=pl.ANY)],
        out_specs=pl.BlockSpec((nb, tt, d), lambda bi, i, a, c: (bi, i, 0)),
        scratch_shapes=[pltpu.VMEM((2, tm, d), F32), pltpu.VMEM((2, tm, d), F32), pltpu.SemaphoreType.DMA((2, 2))],
    )
    return pl.pallas_call(
        _combine_kernel,
        grid_spec=grid_spec,
        out_shape=jax.ShapeDtypeStruct((b, t, d), F32),
        compiler_params=_cparams(("arbitrary", "arbitrary")),
        name="moe_combine_norm",
    )(slot1, slot2, x, mod, route, gf, e_out)


def _lookup(table, idx):
    out = jnp.zeros_like(idx)
    for e in range(N_EXPERTS):
        out = jnp.where(idx == e, table[e], out)
    return out


def _floor_div(a, b):
    q = jnp.floor(a.astype(F32) / b.astype(F32)).astype(I32)
    q = jnp.where((q + 1) * b <= a, q + 1, q)
    return jnp.where(q * b > a, q - 1, q)


def _moe_layout(route_t, counts_f, counts_a_f, n_a):
    n = route_t.shape[1]
    r, sub = MOE_TILE_ROWS, MOE_SUB
    n_tiles = -(-TOP_K * n // r) + N_EXPERTS
    i1 = route_t[0].astype(I32)
    i2 = route_t[1].astype(I32)
    r1 = route_t[4].astype(I32)
    r2 = route_t[5].astype(I32)
    counts = counts_f[0, :N_EXPERTS].astype(I32)
    nt = (counts + r - 1) // r
    ntc = jnp.maximum(nt, 1)
    sz = jnp.maximum(((_floor_div(counts + ntc - 1, ntc) + sub - 1) // sub) * sub, sub)
    cum = jnp.cumsum(nt)
    tstart = cum - nt
    total = cum[-1]
    s_idx = jnp.arange(n_tiles, dtype=I32)
    te_raw = jnp.minimum(jnp.sum((s_idx[:, None] >= cum[None, :]).astype(I32), axis=1), N_EXPERTS - 1)
    used = s_idx < total
    last_e = jnp.max(jnp.where(used, te_raw, 0))
    te = jnp.where(used, te_raw, last_e)
    sz_t = _lookup(sz, te)
    k_in = s_idx - _lookup(tstart, te)
    tr = jnp.where(used, jnp.clip(_lookup(counts, te) - k_in * sz_t, 0, sz_t), 0).astype(I32)
    counts_a = counts_a_f[0, :N_EXPERTS].astype(I32)
    tsp = jnp.clip(_lookup(counts_a, te) - k_in * sz_t, 0, tr).astype(I32)

    def slot(ei, ri):
        sz_i = _lookup(sz, ei)
        k = _floor_div(ri, sz_i)
        return (_lookup(tstart, ei) + k) * r + (ri - k * sz_i)
    slot1, slot2 = slot(i1, r1), slot(i2, r2)
    tok = jnp.arange(n, dtype=I32)
    tok = jnp.where(tok < n_a, tok, tok - n_a)
    src = jnp.zeros((n_tiles * r,), I32).at[jnp.concatenate([slot1, slot2])].set(jnp.concatenate([tok, tok]))
    return te, tr, tsp, src, slot1, slot2, n_tiles


def kernel(x_prompt, x_sample, c_prompt, c_sample, state_pool, state_ssm_re, state_ssm_im, w_ada, b_ada, norm_mix, norm_ffn, w_in, w_pool, pool_scale, ssm_lam_re, ssm_lam_im, ssm_log_step, ssm_b_re, ssm_b_im, ssm_c_re, ssm_c_im, ssm_d, ssm_w_glu, w_out, ffn_w_gate, ffn_w_up, ffn_w_down, moe_w_router, moe_b_router, moe_w_gate, moe_w_up, moe_w_down, norm_final):
    bp, tp, d = x_prompt.shape
    bs, ts, _ = x_sample.shape
    depth = w_ada.shape[0]
    c_pool = w_pool.shape[1] * w_pool.shape[2]
    assert tp % CHUNK == 0 and ts <= CHUNK

    m_rows = -(-(bp + bs) // SUBLANES_V7X) * SUBLANES_V7X
    c_all = jnp.concatenate([c_prompt, c_sample, jnp.zeros((m_rows - bp - bs, d), F32)], axis=0)
    mod_all = _ada(c_all, w_ada, b_ada)

    mod_p = mod_all[:, :bp].reshape(depth, bp, 1, 6 * d)
    mod_s = mod_all[:, bp:bp + bs].reshape(depth, bs, 1, 6 * d)
    w_in_b, w_out_b = w_in.astype(BF16), w_out.astype(BF16)
    wp_b, wglu_b = w_pool.astype(BF16), ssm_w_glu.astype(BF16)
    ps = pool_scale.reshape(depth, 1, c_pool).astype(F32)
    dsk = ssm_d.reshape(depth, 1, d - c_pool).astype(F32)
    k_t, b_t, c_t, p16, q16, p8b, q8b = jax.vmap(lambda *a: _s5_params(*a, ts, bs))(
        ssm_lam_re, ssm_lam_im, ssm_log_step, ssm_b_re, ssm_b_im, ssm_c_re, ssm_c_im)
    x0_t = jnp.concatenate([state_ssm_re, state_ssm_im], axis=-1).astype(F32).transpose(0, 2, 3, 1)
    st_p = jnp.zeros((1, bp, POOL_BUF, c_pool), F32)
    st_s = state_pool.astype(F32)

    xp, xs = x_prompt, x_sample
    pool_p, re_p, im_p, pool_s, re_s, im_s = [], [], [], [], [], []
    y_p = y_s = None
    for l in range(depth):
        gm = norm_mix[l].reshape(1, d)
        gn = norm_ffn[l].reshape(1, d)

        up_all = _proj_in(xp, mod_p, l, gm, w_in_b)
        us_all = _proj_in(xs, mod_s, l, gm, w_in_b)
        y_ssm_p, xend = _s5_prompt(up_all, l, k_t, b_t, c_t, p16, q16, c_pool)
        ys2, x1_t = _s5_sample(us_all.reshape(bs * ts, d), l, x0_t, k_t, b_t, c_t, p8b, q8b, ts, c_pool)
        y_ssm_s = ys2.reshape(bs, ts, d - c_pool)
        ffn_f32 = (ffn_w_gate[l // 2], ffn_w_up[l // 2], ffn_w_down[l // 2]) if l % 2 == 0 else ()
        xp, new_pool_p, ffn_bf16 = _mix(xp, up_all, y_ssm_p, st_p, 0, mod_p, l, wp_b, ps, dsk, wglu_b, w_out_b, 0,
                                        cast=ffn_f32)
        xs, new_pool_s, _ = _mix(xs, us_all, y_ssm_s, st_s, l, mod_s, l, wp_b, ps, dsk, wglu_b, w_out_b, PAST_LEN)
        pool_p.append(new_pool_p)
        pool_s.append(new_pool_s)
        re_p.append(xend[..., :SSM_P])
        im_p.append(xend[..., SSM_P:])
        re_s.append(x1_t[:, :SSM_P, :].transpose(2, 0, 1))
        im_s.append(x1_t[:, SSM_P:, :].transpose(2, 0, 1))

        i = l // 2
        if l % 2 == 0:
            wg, wu, wd = ffn_bf16
            xp = _ffn(xp, mod_p, l, gn, wg, wu, wd)
            xs = _ffn(xs, mod_s, l, gn, wg, wu, wd)
        else:
            ne = moe_w_router.shape[-1]
            wr32 = jnp.pad(moe_w_router[i].astype(F32), ((0, 0), (0, ROUTE_LANES - ne)))
            wr_hi = wr32.astype(BF16)
            wr = jnp.stack([wr_hi, (wr32 - wr_hi.astype(F32)).astype(BF16)])
            br = jnp.pad(moe_b_router[i].astype(F32), (0, ROUTE_LANES - ne)).reshape(1, ROUTE_LANES)
            zero_counts = jnp.zeros((1, ROUTE_LANES), F32)
            h_p, route_p, rt_p, cnt_p = _route(xp, mod_p, l, gn, wr, br, zero_counts)
            h_s, route_s, rt_s, cnt_s = _route(xs, mod_s, l, gn, wr, br, cnt_p)
            n_p = bp * tp
            te, tr, tsp, src, slot1, slot2, n_tiles = _moe_layout(jnp.concatenate([rt_p, rt_s], axis=1), cnt_s, cnt_p,
                                                                  n_p)
            e_out = _moe_experts(h_p, h_s, te, tr, tsp, src, moe_w_gate[i], moe_w_up[i], moe_w_down[i], n_tiles)
            last = l == depth - 1
            assert last, "the combine kernel also applies the final norm"
            gf = norm_final.reshape(1, d)
            y_p = _combine(xp, mod_p, l, route_p, slot1[:n_p], slot2[:n_p], e_out, gf)
            y_s = _combine(xs, mod_s, l, route_s, slot1[n_p:], slot2[n_p:], e_out, gf)

    return (y_p, y_s, jnp.stack(pool_p), jnp.stack(re_p), jnp.stack(im_p),
            jnp.stack(pool_s), jnp.stack(re_s), jnp.stack(im_s))
```

```python
import functools
import math

import jax
import jax.numpy as jnp
import numpy as np
from jax import lax
from jax.experimental import pallas as pl
from jax.experimental.pallas import tpu as pltpu

F32 = jnp.float32
BF16 = jnp.bfloat16
I32 = jnp.int32

EPS = 1e-6
POOL_WINDOWS = (2, 4, 8, 16)
POOL_BUF = max(POOL_WINDOWS) - 1
SUBLANES_V7X = 8
BF16_SUBLANES_V7X = 16
LANES_V7X = 128
HALO = -(-POOL_BUF // SUBLANES_V7X) * SUBLANES_V7X
SSM_H = 16
SSM_P = 64
CHUNK = 16
CW = CHUNK * SSM_H
SW = 2 * SSM_P
N_EXPERTS = 8
TOP_K = 2
PAST_LEN = 16384

VMEM_LIMIT_V7X = 56 * 1024 * 1024
ROWS_IN = 512
ROWS_MIX = 256
ROWS_FFN = 512
FF_TILE = 512
ROWS_ROUTE = 512
MOE_TILE_ROWS = 2560
MOE_SUB = 128
MOE_BLK_BIG = 1024
MOE_BLK = 512
GATHER_UNROLL = 8
DMA_PRIORITIES_V7X = 2
MOE_FF_TILE = 256
ROWS_COMBINE = 256
ADA_TN = 1024


def _cparams(sem):
    return pltpu.CompilerParams(dimension_semantics=sem, vmem_limit_bytes=VMEM_LIMIT_V7X)


def _tile(b, t, rows):
    if t >= rows:
        assert t % rows == 0
        return 1, rows
    nb = max(1, min(b, rows // t))
    assert b % nb == 0
    return nb, t


def _modnorm(x, g, s, sh):
    r = lax.rsqrt(jnp.mean(x * x, axis=-1, keepdims=True) + EPS)
    return (x * r) * g * (1.0 + s) + sh


def _dot(a, b):
    return jnp.dot(a, b, preferred_element_type=F32)


def _mod_spec(nb, d, k, l):
    return pl.BlockSpec((None, nb, 1, d), lambda b, i, *_: (l, b, 0, k))


def _layer_spec(shape, l):
    return pl.BlockSpec((None,) + tuple(shape), lambda *_: (l,) + (0,) * len(shape))


def _ada_kernel(c_ref, w_ref, b_ref, o_ref):
    c = c_ref[...]
    sc = c * jax.nn.sigmoid(c)
    o_ref[0] = _dot(sc.astype(BF16), w_ref[0].astype(BF16)) + b_ref[0]


def _ada(c_all, w_ada, b_ada):
    m, d = c_all.shape
    depth, _, n6 = w_ada.shape
    tn = math.gcd(ADA_TN, n6)
    return pl.pallas_call(
        _ada_kernel,
        grid=(depth, n6 // tn),
        in_specs=[pl.BlockSpec((m, d), lambda l, n: (0, 0)),
                  pl.BlockSpec((1, d, tn), lambda l, n: (l, 0, n)),
                  pl.BlockSpec((1, 1, tn), lambda l, n: (l, 0, n))],
        out_specs=pl.BlockSpec((1, m, tn), lambda l, n: (l, 0, n)),
        out_shape=jax.ShapeDtypeStruct((depth, m, n6), F32),
        compiler_params=_cparams(("arbitrary", "arbitrary")),
        name="adaln_mod",
    )(c_all, w_ada, b_ada.reshape(depth, 1, n6))


def _in_kernel(x_ref, g_ref, s_ref, sh_ref, w_ref, u_ref):
    nb, tt, d = x_ref.shape
    h = _modnorm(x_ref[...], g_ref[...], s_ref[...], sh_ref[...])
    u = _dot(h.reshape(nb * tt, d).astype(BF16), w_ref[...])
    u_ref[...] = u.reshape(nb, tt, u_ref.shape[-1])


def _proj_in(x, mod, l, g, w_bf16):
    b, t, d = x.shape
    n = w_bf16.shape[-1]
    nb, tt = _tile(b, t, ROWS_IN)
    return pl.pallas_call(
        _in_kernel,
        grid=(b // nb, t // tt),
        in_specs=[pl.BlockSpec((nb, tt, d), lambda bi, i: (bi, i, 0)),
                  pl.BlockSpec((1, d), lambda bi, i: (0, 0)),
                  _mod_spec(nb, d, 1, l), _mod_spec(nb, d, 0, l),
                  _layer_spec((d, n), l)],
        out_specs=pl.BlockSpec((nb, tt, n), lambda bi, i: (bi, i, 0)),
        out_shape=jax.ShapeDtypeStruct((b, t, n), F32),
        compiler_params=_cparams(("arbitrary", "arbitrary")),
        name="norm_proj_in",
    )(x, g, mod, mod, w_bf16)


GROUPS_PER_LANE_TILE = LANES_V7X // SSM_H


def _toeplitz_placement():
    toe = np.zeros((CHUNK, CW, CW), np.float32)
    for s in range(CHUNK):
        for h in range(SSM_H):
            for t in range(s, CHUNK):
                toe[t, (t - s) * SSM_H + h, s * SSM_H + h] = 1.0
    return jnp.asarray(toe)


def _s5_params(lam_re, lam_im, log_step, b_re, b_im, c_re, c_im, ts, bs):
    hp = lax.Precision.HIGHEST
    toe = _toeplitz_placement()
    lr, li = lam_re.astype(F32), lam_im.astype(F32)
    step = jnp.exp(log_step.astype(F32))[:, None]
    mag = jnp.exp(lr * step)
    a_re = mag * jnp.cos(li * step)
    a_im = mag * jnp.sin(li * step)
    den = lr * lr + li * li
    nr = a_re - 1.0
    f_re = (nr * lr + a_im * li) / den
    f_im = (a_im * lr - nr * li) / den
    br, bi = b_re.astype(F32), b_im.astype(F32)
    bb_re = f_re[..., None] * br - f_im[..., None] * bi
    bb_im = f_re[..., None] * bi + f_im[..., None] * br
    cr, ci = c_re.astype(F32), c_im.astype(F32)
    g = lr.shape[0]

    pw_re, pw_im = [jnp.ones_like(a_re)], [jnp.zeros_like(a_im)]
    for _ in range(CHUNK):
        pr, pi = pw_re[-1], pw_im[-1]
        pw_re.append(pr * a_re - pi * a_im)
        pw_im.append(pr * a_im + pi * a_re)
    pw_re, pw_im = jnp.stack(pw_re), jnp.stack(pw_im)

    bbt_re, bbt_im = bb_re.transpose(0, 2, 1), bb_im.transpose(0, 2, 1)
    w_re = pw_re[:CHUNK, :, None, :] * bbt_re - pw_im[:CHUNK, :, None, :] * bbt_im
    w_im = pw_re[:CHUNK, :, None, :] * bbt_im + pw_im[:CHUNK, :, None, :] * bbt_re
    m = (jnp.einsum('gop,lgip->goli', cr, w_re, precision=hp)
         - jnp.einsum('gop,lgip->goli', ci, w_im, precision=hp)).reshape(g, SSM_H, CW)
    k_t = jnp.einsum('gok,tkn->gton', m.astype(BF16), toe.astype(BF16),
                     preferred_element_type=F32).reshape(g, CW, CW)

    rev_re, rev_im = pw_re[:CHUNK][::-1], pw_im[:CHUNK][::-1]
    ar = jnp.repeat(rev_re.transpose(1, 2, 0), SSM_H, axis=-1)
    ai = jnp.repeat(rev_im.transpose(1, 2, 0), SSM_H, axis=-1)
    xr = jnp.tile(bb_re, (1, 1, CHUNK))
    xi = jnp.tile(bb_im, (1, 1, CHUNK))
    bt_re = ar * xr - ai * xi
    bt_im = ar * xi + ai * xr
    b_t = jnp.concatenate([bt_re, bt_im, bt_im, bt_re], axis=1)

    qr = pw_re[1:].transpose(1, 0, 2)[:, :, None, :]
    qi = pw_im[1:].transpose(1, 0, 2)[:, :, None, :]
    cm_re = cr[:, None] * qr - ci[:, None] * qi
    cm_im = -(cr[:, None] * qi + ci[:, None] * qr)
    c_t = jnp.concatenate([cm_re, cm_im], axis=-1).reshape(g, CW, SW)

    def coef(k):
        pr, pi = pw_re[k], pw_im[k]
        return (jnp.concatenate([pr, pr, pr, pr], axis=-1),
                jnp.concatenate([-pi, pi, pi, -pi], axis=-1))
    p16, q16 = coef(CHUNK)
    p8, q8 = coef(ts)
    p8b = jnp.broadcast_to(p8[:, :SW, None], (g, SW, bs))
    q8b = jnp.broadcast_to(q8[:, :SW, None], (g, SW, bs))
    return k_t.astype(BF16), b_t.astype(BF16), c_t.astype(BF16), p16, q16, p8b, q8b


def _s5_prompt_kernel(us_ref, kt_ref, bt_ref, ct_ref, p_ref, q_ref, y_ref, xe_ref, ut, yt, scan_x, scan_xs, s_scr):
    gt = ut.shape[0]
    nchunk = ut.shape[-1]
    for t in range(CHUNK):
        slab = us_ref[0, pl.ds(t, nchunk, stride=CHUNK), :]
        ut[:, t] = slab.T.reshape(gt, SSM_H, nchunk).astype(BF16)
    for gi in range(gt):
        buz = _dot(bt_ref[gi], ut[gi].reshape(CW, nchunk))
        buz_t = buz.T
        scan_x[pl.ds(gi, nchunk, stride=gt), :] = buz_t[:, :SW]
        scan_xs[pl.ds(gi, nchunk, stride=gt), :] = buz_t[:, SW:]
    px, pxs = p_ref[:, :SW], p_ref[:, SW:]
    qx, qxs = q_ref[:, :SW], q_ref[:, SW:]
    x = jnp.zeros((gt, SW), F32)
    xs = jnp.zeros((gt, SW), F32)
    for c in range(nchunk):
        lo, hi = c * gt, (c + 1) * gt
        s_scr[lo:hi, :] = x
        x, xs = (px * x + qx * xs + scan_x[lo:hi, :],
                 pxs * xs + qxs * x + scan_xs[lo:hi, :])
    xe_ref[0] = x
    for gi in range(gt):
        st = s_scr[pl.ds(gi, nchunk, stride=gt), :].T.astype(BF16)
        y_t = _dot(kt_ref[gi], ut[gi].reshape(CW, nchunk)) + _dot(ct_ref[gi], st)
        yt[gi] = y_t.reshape(CHUNK, SSM_H, nchunk)
    for t in range(CHUNK):
        y_ref[0, pl.ds(t, nchunk, stride=CHUNK), :] = yt[:, t].reshape(gt * SSM_H, nchunk).T


def _s5_prompt(u, l, k_t, b_t, c_t, p16, q16, c_off):
    b, t, _ = u.shape
    g = k_t.shape[1]
    gt = GROUPS_PER_LANE_TILE
    lt = gt * SSM_H
    nchunk = t // CHUNK
    off = c_off // lt

    def wspec(r, c):
        return pl.BlockSpec((None, gt, r, c), lambda bi, j: (l, j, 0, 0))
    return pl.pallas_call(
        _s5_prompt_kernel,
        grid=(b, g // gt),
        in_specs=[pl.BlockSpec((1, t, lt), lambda bi, j: (bi, 0, off + j)),
                  wspec(CW, CW), wspec(2 * SW, CW), wspec(CW, SW),
                  pl.BlockSpec((None, gt, 2 * SW), lambda bi, j: (l, j, 0)),
                  pl.BlockSpec((None, gt, 2 * SW), lambda bi, j: (l, j, 0))],
        out_specs=[pl.BlockSpec((1, t, lt), lambda bi, j: (bi, 0, j)),
                   pl.BlockSpec((1, gt, SW), lambda bi, j: (bi, j, 0))],
        out_shape=[jax.ShapeDtypeStruct((b, t, g * SSM_H), F32), jax.ShapeDtypeStruct((b, g, SW), F32)],
        scratch_shapes=[pltpu.VMEM((gt, CHUNK, SSM_H, nchunk), BF16), pltpu.VMEM((gt, CHUNK, SSM_H, nchunk), F32),
                        pltpu.VMEM((nchunk * gt, SW), F32), pltpu.VMEM((nchunk * gt, SW), F32),
                        pltpu.VMEM((nchunk * gt, SW), F32)],
        compiler_params=_cparams(("arbitrary", "arbitrary")),
        name="s5_prompt",
    )(u, k_t, b_t, c_t, p16, q16)


def _s5_sample_kernel(us_ref, x0_ref, kt_ref, bt_ref, ct_ref, p_ref, q_ref, y_ref, x1_ref, ut, yt, *, ts):
    gt = ut.shape[0]
    bs = ut.shape[-1]
    w = ts * SSM_H
    for t in range(ts):
        slab = us_ref[pl.ds(t, bs, stride=ts), :]
        ut[:, t] = slab.T.reshape(gt, SSM_H, bs).astype(BF16)
    for gi in range(gt):
        ug = ut[gi].reshape(w, bs)
        x0 = x0_ref[gi]
        x0s = jnp.concatenate([x0[SSM_P:], x0[:SSM_P]], axis=0)
        y_t = _dot(kt_ref[gi, :w, :w], ug) + _dot(ct_ref[gi, :w, :], x0.astype(BF16))
        x1_ref[gi] = p_ref[gi] * x0 + q_ref[gi] * x0s + _dot(bt_ref[gi, :SW, CW - w:], ug)
        yt[gi] = y_t.reshape(ts, SSM_H, bs)
    for t in range(ts):
        y_ref[pl.ds(t, bs, stride=ts), :] = yt[:, t].reshape(gt * SSM_H, bs).T


def _s5_sample(u2, l, x0_t, k_t, b_t, c_t, p8b, q8b, ts, c_off):
    rows, _ = u2.shape
    _, g, _, bs = x0_t.shape
    gt = GROUPS_PER_LANE_TILE
    lt = gt * SSM_H
    off = c_off // lt

    def wspec(r, c):
        return pl.BlockSpec((None, gt, r, c), lambda j: (l, j, 0, 0))
    return pl.pallas_call(
        functools.partial(_s5_sample_kernel, ts=ts),
        grid=(g // gt,),
        in_specs=[pl.BlockSpec((rows, lt), lambda j: (0, off + j)),
                  wspec(SW, bs), wspec(CW, CW), wspec(2 * SW, CW), wspec(CW, SW), wspec(SW, bs), wspec(SW, bs)],
        out_specs=[pl.BlockSpec((rows, lt), lambda j: (0, j)), pl.BlockSpec((gt, SW, bs), lambda j: (j, 0, 0))],
        out_shape=[jax.ShapeDtypeStruct((rows, g * SSM_H), F32), jax.ShapeDtypeStruct((g, SW, bs), F32)],
        scratch_shapes=[pltpu.VMEM((gt, ts, SSM_H, bs), BF16), pltpu.VMEM((gt, ts, SSM_H, bs), F32)],
        compiler_params=_cparams(("arbitrary",)),
        name="s5_sample",
    )(u2, x0_t, k_t, b_t, c_t, p8b, q8b)


def _mix_kernel(x_ref, up_ref, us_ref, ys_ref, st_ref, g1_ref, wp_ref, ps_ref, dsk_ref, wglu_ref,
                wout_ref, *rest, start_pos, carry, ncast):
    cast_in, (o_ref, np_ref), cast_out, ext = rest[:ncast], rest[ncast:ncast + 2], rest[ncast + 2:-1], rest[-1]
    for src, dst in zip(cast_in, cast_out):
        dst[...] = src[...].astype(BF16)
    nb, tt, d = x_ref.shape
    c = up_ref.shape[-1]
    pg = c // len(POOL_WINDOWS)
    rows = nb * tt
    i = pl.program_id(1)

    @pl.when(i == 0)
    def _():
        ext[:, HALO - POOL_BUF:HALO, :] = st_ref[...]
    ext[:, HALO:HALO + tt, :] = up_ref[...]
    np_ref[...] = ext[:, HALO + tt - POOL_BUF:HALO + tt, :]

    pos = start_pos + i * tt + lax.broadcasted_iota(I32, (1, tt, 1), 1)
    outs = []
    for gidx, w in enumerate(POOL_WINDOWS):
        c0 = gidx * pg
        cur = ext[:, HALO:HALO + tt, c0:c0 + pg]
        s = cur
        for k in range(1, w):
            s = s + ext[:, HALO - k:HALO - k + tt, c0:c0 + pg]
        cnt = jnp.minimum(pos + 1, w).astype(F32)
        dd = s / cnt - cur
        yg = _dot(dd.reshape(rows, pg).astype(BF16), wp_ref[gidx])
        outs.append(yg * ps_ref[:, c0:c0 + pg])
    y_pool = jnp.concatenate(outs, axis=-1)

    if carry:
        ext[:, 0:HALO, :] = ext[:, tt:tt + HALO, :]

    us = us_ref[...].reshape(rows, c)
    yf = ys_ref[...].reshape(rows, c) + dsk_ref[...] * us
    gl = jax.nn.gelu(yf, approximate=True)
    o = gl * jax.nn.sigmoid(_dot(gl.astype(BF16), wglu_ref[...]))
    mo = _dot(y_pool.astype(BF16), wout_ref[:c, :]) + _dot(o.astype(BF16), wout_ref[c:, :])
    o_ref[...] = x_ref[...] + g1_ref[...] * mo.reshape(nb, tt, d)


def _mix(x, u, y_ssm, pool_state, l_state, mod, l, wp, ps, dsk, wglu, wout, start_pos, cast=()):
    b, t, d = x.shape
    c = y_ssm.shape[-1]
    nb, tt = _tile(b, t, ROWS_MIX)
    nt = t // tt
    assert nt == 1 or tt >= HALO
    nsteps = (b // nb) * nt
    cast_specs = []
    for a in cast:
        blk = -(-(-(-a.shape[0] // nsteps)) // BF16_SUBLANES_V7X) * BF16_SUBLANES_V7X
        last = -(-a.shape[0] // blk) - 1
        cast_specs.append(pl.BlockSpec((blk, a.shape[1]), lambda bi, i, last=last: (jnp.minimum(bi * nt + i, last), 0)))
    outs = pl.pallas_call(
        functools.partial(_mix_kernel, start_pos=start_pos, carry=nt > 1, ncast=len(cast)),
        grid=(b // nb, nt),
        in_specs=[pl.BlockSpec((nb, tt, d), lambda bi, i: (bi, i, 0)),
                  pl.BlockSpec((nb, tt, c), lambda bi, i: (bi, i, 0)),
                  pl.BlockSpec((nb, tt, c), lambda bi, i: (bi, i, 1)),
                  pl.BlockSpec((nb, tt, c), lambda bi, i: (bi, i, 0)),
                  pl.BlockSpec((None, nb, POOL_BUF, c), lambda bi, i: (l_state, bi, 0, 0)),
                  _mod_spec(nb, d, 2, l),
                  _layer_spec(wp.shape[1:], l), _layer_spec((1, c), l), _layer_spec((1, c), l),
                  _layer_spec((c, c), l), _layer_spec((d, d), l)] + cast_specs,
        out_specs=[pl.BlockSpec((nb, tt, d), lambda bi, i: (bi, i, 0)),
                   pl.BlockSpec((nb, POOL_BUF, c), lambda bi, i: (bi, 0, 0))] + cast_specs,
        out_shape=[jax.ShapeDtypeStruct((b, t, d), F32), jax.ShapeDtypeStruct((b, POOL_BUF, c), F32)]
        + [jax.ShapeDtypeStruct(a.shape, BF16) for a in cast],
        scratch_shapes=[pltpu.VMEM((nb, HALO + tt, c), F32)],
        compiler_params=_cparams(("arbitrary", "arbitrary")),
        name="pool_glu_proj_out",
    )(x, u, u, y_ssm, pool_state, mod, wp, ps, dsk, wglu, wout, *cast)
    return outs[0], outs[1], tuple(outs[2:])


def _ffn_kernel(x_ref, g_ref, s_ref, sh_ref, g2_ref, wg_ref, wu_ref, wd_ref, o_ref, hb, acc, *, last_valid):
    nb, tt, d = x_ref.shape
    tf = wg_ref.shape[1]
    j = pl.program_id(2)
    nj = pl.num_programs(2)

    @pl.when(j == 0)
    def _():
        h = _modnorm(x_ref[...], g_ref[...], s_ref[...], sh_ref[...])
        hb[...] = h.reshape(nb * tt, d).astype(BF16)
        acc[...] = jnp.zeros_like(acc)

    def step(valid):
        h = hb[...]
        gate = _dot(h, wg_ref[...])
        up = _dot(h, wu_ref[...])
        a = gate * jax.nn.sigmoid(gate) * up
        wd = wd_ref[...]
        if valid < tf:
            a = jnp.where(lax.broadcasted_iota(I32, (1, tf), 1) < valid, a, 0.0)
            wd = jnp.where(lax.broadcasted_iota(I32, (tf, 1), 0) < valid, wd, jnp.zeros_like(wd))
        acc[...] += _dot(a.astype(BF16), wd)

    if last_valid == tf:
        step(tf)
    else:
        pl.when(j < nj - 1)(lambda: step(tf))
        pl.when(j == nj - 1)(lambda: step(last_valid))

    @pl.when(j == nj - 1)
    def _():
        o_ref[...] = x_ref[...] + g2_ref[...] * acc[...].reshape(nb, tt, d)


def _ffn(x, mod, l, g, wg, wu, wd):
    b, t, d = x.shape
    ff = wg.shape[1]
    nb, tt = _tile(b, t, ROWS_FFN)
    tf = min(FF_TILE, ff)
    nj = -(-ff // tf)
    return pl.pallas_call(
        functools.partial(_ffn_kernel, last_valid=ff - (nj - 1) * tf),
        grid=(b // nb, t // tt, nj),
        in_specs=[pl.BlockSpec((nb, tt, d), lambda bi, i, j: (bi, i, 0)),
                  pl.BlockSpec((1, d), lambda bi, i, j: (0, 0)),
                  _mod_spec(nb, d, 4, l), _mod_spec(nb, d, 3, l), _mod_spec(nb, d, 5, l),
                  pl.BlockSpec((d, tf), lambda bi, i, j: (0, j)),
                  pl.BlockSpec((d, tf), lambda bi, i, j: (0, j)),
                  pl.BlockSpec((tf, d), lambda bi, i, j: (j, 0))],
        out_specs=pl.BlockSpec((nb, tt, d), lambda bi, i, j: (bi, i, 0)),
        out_shape=jax.ShapeDtypeStruct((b, t, d), F32),
        scratch_shapes=[pltpu.VMEM((nb * tt, d), BF16), pltpu.VMEM((nb * tt, d), F32)],
        compiler_params=_cparams(("arbitrary", "arbitrary", "arbitrary")),
        name="dense_swiglu",
    )(x, g, mod, mod, mod, wg, wu, wd)


ROUTE_LANES = LANES_V7X
ROUTE_ROWS = SUBLANES_V7X


def _route_kernel(x_ref, g_ref, s_ref, sh_ref, wr_ref, br_ref, cin_ref, h_ref, r_ref, rt_ref, cout_ref, carry):
    nb, tt, d = x_ref.shape
    tm = nb * tt
    first = jnp.logical_and(pl.program_id(0) == 0, pl.program_id(1) == 0)

    @pl.when(first)
    def _():
        carry[...] = cin_ref[...]

    h = _modnorm(x_ref[...], g_ref[...], s_ref[...], sh_ref[...]).reshape(tm, d)
    h_ref[...] = h
    h_hi = h.astype(BF16)
    h_lo = (h - h_hi.astype(F32)).astype(BF16)
    logits = _dot(h_hi, wr_ref[0]) + (_dot(h_hi, wr_ref[1]) + _dot(h_lo, wr_ref[0])) + br_ref[...]
    lane = lax.broadcasted_iota(I32, (tm, ROUTE_LANES), 1)
    neg = jnp.float32(-jnp.inf)
    l1 = jnp.where(lane < N_EXPERTS, logits, neg)
    m1 = jnp.max(l1, axis=-1, keepdims=True)
    i1 = jnp.min(jnp.where(l1 == m1, lane, ROUTE_LANES), axis=-1, keepdims=True)
    l2 = jnp.where(lane == i1, neg, l1)
    m2 = jnp.max(l2, axis=-1, keepdims=True)
    i2 = jnp.min(jnp.where(l2 == m2, lane, ROUTE_LANES), axis=-1, keepdims=True)
    e2 = jnp.exp(m2 - m1)
    den = 1.0 + e2
    p1 = 1.0 / den
    p2 = e2 / den

    onehot = jnp.logical_or(lane == i1, lane == i2).astype(F32)
    rr = lax.broadcasted_iota(I32, (tm, tm), 0)
    cc = lax.broadcasted_iota(I32, (tm, tm), 1)
    before = (cc < rr).astype(BF16)
    rank = _dot(before, onehot.astype(BF16)) + carry[...]
    r1 = jnp.sum(jnp.where(lane == i1, rank, 0.0), axis=-1, keepdims=True)
    r2 = jnp.sum(jnp.where(lane == i2, rank, 0.0), axis=-1, keepdims=True)
    carry[...] += jnp.sum(onehot, axis=0, keepdims=True)

    out = jnp.where(lane == 0, i1.astype(F32), 0.0)
    out = jnp.where(lane == 1, i2.astype(F32), out)
    out = jnp.where(lane == 2, p1, out)
    out = jnp.where(lane == 3, p2, out)
    out = jnp.where(lane == 4, r1, out)
    out = jnp.where(lane == 5, r2, out)
    r_ref[...] = out
    rt_ref[...] = out.T[:ROUTE_ROWS, :]
    cout_ref[...] = carry[...]


def _route(x, mod, l, g, wr_pad, br_pad, counts_in):
    b, t, d = x.shape
    nb, tt = _tile(b, t, ROWS_ROUTE)
    tm = nb * tt
    nt = t // tt
    const2 = lambda bi, i: (0, 0)
    return pl.pallas_call(
        _route_kernel,
        grid=(b // nb, nt),
        in_specs=[pl.BlockSpec((nb, tt, d), lambda bi, i: (bi, i, 0)),
                  pl.BlockSpec((1, d), const2),
                  _mod_spec(nb, d, 4, l), _mod_spec(nb, d, 3, l),
                  pl.BlockSpec((2, d, ROUTE_LANES), lambda bi, i: (0, 0, 0)),
                  pl.BlockSpec((1, ROUTE_LANES), const2),
                  pl.BlockSpec((1, ROUTE_LANES), const2)],
        out_specs=[pl.BlockSpec((tm, d), lambda bi, i: (bi * nt + i, 0)),
                   pl.BlockSpec((tm, ROUTE_LANES), lambda bi, i: (bi * nt + i, 0)),
                   pl.BlockSpec((ROUTE_ROWS, tm), lambda bi, i: (0, bi * nt + i)),
                   pl.BlockSpec((1, ROUTE_LANES), const2)],
        out_shape=[jax.ShapeDtypeStruct((b * t, d), F32),
                   jax.ShapeDtypeStruct((b * t, ROUTE_LANES), F32),
                   jax.ShapeDtypeStruct((ROUTE_ROWS, b * t), F32),
                   jax.ShapeDtypeStruct((1, ROUTE_LANES), F32)],
        scratch_shapes=[pltpu.VMEM((1, ROUTE_LANES), F32)],
        compiler_params=_cparams(("arbitrary", "arbitrary")),
        name="moe_route",
    )(x, g, mod, mod, wr_pad, br_pad, counts_in)


def _row_gather_copy(src_hbm, tok, dst, r, sem):
    return pltpu.make_async_copy(src_hbm.at[pl.ds(tok, 1), :], dst.at[pl.ds(r, 1), :], sem)


def _moe_kernel(te_ref, tr_ref, tsp_ref, src_ref, ha_hbm, hb_hbm, wg_ref, wu_ref, wd_ref, o_hbm,
                xb, acc, stage, wgu, wdb, gsem, osem):
    s = pl.program_id(0)
    j = pl.program_id(1)
    ns = pl.num_programs(0)
    nj = pl.num_programs(1)
    rows = tr_ref[s]
    nsub = (rows + MOE_SUB - 1) // MOE_SUB
    tile_rows = xb.shape[0]

    def out_copy():
        return pltpu.make_async_copy(acc, o_hbm.at[pl.ds(pl.multiple_of(s * tile_rows, MOE_SUB), tile_rows), :], osem)

    @pl.when(j == 0)
    def _():
        @pl.when(rows > 0)
        def _gather():
            def issue(sub, slot):
                base = s * tile_rows + sub * MOE_SUB
                na = jnp.clip(tsp_ref[s] - sub * MOE_SUB, 0, MOE_SUB)

                def start_from(h_hbm):
                    def body(r, carry):
                        _row_gather_copy(h_hbm, src_ref[base + r], stage.at[slot], r, gsem.at[slot]).start()
                        return carry
                    return body

                def start_group_from(h_hbm):
                    def body(g, carry):
                        for k in range(GATHER_UNROLL):
                            r = g * GATHER_UNROLL + k
                            _row_gather_copy(h_hbm, src_ref[base + r], stage.at[slot], r,
                                             gsem.at[slot]).start(priority=DMA_PRIORITIES_V7X - 1)
                        return carry
                    return body

                @pl.when(na == MOE_SUB)
                def _():
                    lax.fori_loop(0, MOE_SUB // GATHER_UNROLL, start_group_from(ha_hbm), 0)

                @pl.when(na == 0)
                def _():
                    lax.fori_loop(0, MOE_SUB // GATHER_UNROLL, start_group_from(hb_hbm), 0)

                @pl.when(jnp.logical_and(na > 0, na < MOE_SUB))
                def _():
                    lax.fori_loop(0, na, start_from(ha_hbm), 0)
                    lax.fori_loop(na, MOE_SUB, start_from(hb_hbm), 0)

            issue(0, 0)

            def sub_body(sub, carry):
                slot = sub % 2

                @pl.when(sub + 1 < nsub)
                def _():
                    issue(sub + 1, 1 - slot)
                pltpu.make_async_copy(ha_hbm.at[pl.ds(0, MOE_SUB), :], stage.at[slot], gsem.at[slot]).wait()
                r0 = pl.multiple_of(sub * MOE_SUB, MOE_SUB)
                xb[pl.ds(r0, MOE_SUB), :] = stage[slot].astype(BF16)
                return carry
            lax.fori_loop(0, nsub, sub_body, 0)

        @pl.when(s > 0)
        def _():
            out_copy().wait()
        acc[...] = jnp.zeros_like(acc)

    @pl.when(rows > 0)
    def _():
        tf = wg_ref.shape[1]
        wgu[:, :tf] = wg_ref[...].astype(BF16)
        wgu[:, tf:] = wu_ref[...].astype(BF16)
        wdb[...] = wd_ref[...].astype(BF16)

        def block(r0, m):
            gu = _dot(xb[pl.ds(r0, m), :], wgu[...])
            gate, up = gu[:, :tf], gu[:, tf:]
            a = (gate * jax.nn.sigmoid(gate) * up).astype(BF16)
            acc[pl.ds(r0, m), :] += _dot(a, wdb[...])

        per_big = MOE_BLK_BIG // MOE_SUB
        per_blk = MOE_BLK // MOE_SUB
        nbig = nsub // per_big
        rem_big = nsub - nbig * per_big
        has_blk = rem_big >= per_blk
        rem = rem_big - jnp.where(has_blk, per_blk, 0)
        r_blk = pl.multiple_of(nbig * MOE_BLK_BIG, MOE_BLK)
        r_tail = pl.multiple_of(r_blk + jnp.where(has_blk, MOE_BLK, 0), MOE_SUB)

        def body(blk, carry):
            block(pl.multiple_of(blk * MOE_BLK_BIG, MOE_BLK_BIG), MOE_BLK_BIG)
            return carry
        lax.fori_loop(0, nbig, body, 0)
        pl.when(has_blk)(lambda: block(r_blk, MOE_BLK))
        for k in range(1, per_blk):
            @pl.when(rem == k)
            def _(k=k):
                block(r_tail, k * MOE_SUB)

    @pl.when(j == nj - 1)
    def _():
        out_copy().start()

        @pl.when(s == ns - 1)
        def _():
            out_copy().wait()


def _moe_experts(h_a, h_b, te, tr, tsp, src, wg, wu, wd, n_tiles):
    d = h_a.shape[1]
    assert h_a.shape[0] >= MOE_SUB
    e, _, ff = wg.shape
    tf = min(MOE_FF_TILE, ff)
    assert ff % tf == 0 and MOE_TILE_ROWS % MOE_BLK == 0 and MOE_BLK % MOE_SUB == 0 and MOE_BLK_BIG == 2 * MOE_BLK
    nj = ff // tf
    r = MOE_TILE_ROWS

    def jj(s, j, tr_ref):
        return jnp.where(tr_ref[s] > 0, j, nj - 1)

    grid_spec = pltpu.PrefetchScalarGridSpec(
        num_scalar_prefetch=4,
        grid=(n_tiles, nj),
        in_specs=[pl.BlockSpec(memory_space=pl.ANY), pl.BlockSpec(memory_space=pl.ANY),
                  pl.BlockSpec((None, d, tf), lambda s, j, te_r, tr_r, *_: (te_r[s], 0, jj(s, j, tr_r))),
                  pl.BlockSpec((None, d, tf), lambda s, j, te_r, tr_r, *_: (te_r[s], 0, jj(s, j, tr_r))),
                  pl.BlockSpec((None, tf, d), lambda s, j, te_r, tr_r, *_: (te_r[s], jj(s, j, tr_r), 0))],
        out_specs=pl.BlockSpec(memory_space=pl.ANY),
        scratch_shapes=[pltpu.VMEM((r, d), BF16), pltpu.VMEM((r, d), F32), pltpu.VMEM((2, MOE_SUB, d), F32),
                        pltpu.VMEM((d, 2 * tf), BF16), pltpu.VMEM((tf, d), BF16),
                        pltpu.SemaphoreType.DMA((2,)), pltpu.SemaphoreType.DMA(())],
    )
    return pl.pallas_call(
        _moe_kernel,
        grid_spec=grid_spec,
        out_shape=jax.ShapeDtypeStruct((n_tiles * r, d), F32),
        compiler_params=_cparams(("arbitrary", "arbitrary")),
        name="moe_experts",
    )(te, tr, tsp, src, h_a, h_b, wg, wu, wd)


def _combine_kernel(s1_ref, s2_ref, x_ref, g2_ref, r_ref, gf_ref, e_hbm, o_ref, st1, st2, sem):
    nb, tt, d = x_ref.shape
    tm = nb * tt
    step = pl.program_id(0) * pl.num_programs(1) + pl.program_id(1)
    nsteps = pl.num_programs(0) * pl.num_programs(1)
    slot = step % 2

    def waits(sl):
        pltpu.make_async_copy(e_hbm.at[pl.ds(0, tm), :], st1.at[sl], sem.at[0, sl]).wait()
        pltpu.make_async_copy(e_hbm.at[pl.ds(0, tm), :], st2.at[sl], sem.at[1, sl]).wait()

    def start_row(base, sl, r, parity=0):
        _row_gather_copy(e_hbm, s1_ref[base + r], st1.at[sl], r, sem.at[0, sl]).start(priority=DMA_PRIORITIES_V7X - 1)
        _row_gather_copy(e_hbm, s2_ref[base + r], st2.at[sl], r,
                         sem.at[1, sl]).start(priority=DMA_PRIORITIES_V7X - 1)

    @pl.when(step == 0)
    def _():
        def body(r, carry):
            start_row(0, 0, r)
            return carry
        lax.fori_loop(0, tm, body, 0, unroll=GATHER_UNROLL)

    waits(slot)
    nxt = jnp.minimum(step + 1, nsteps - 1) * tm
    for r in range(tm):
        start_row(nxt, 1 - slot, r, parity=r % DMA_PRIORITIES_V7X)

    f = r_ref[:, 2:3] * st1[slot] + r_ref[:, 3:4] * st2[slot]
    x2 = x_ref[...] + g2_ref[...] * f.reshape(nb, tt, d)
    rs = lax.rsqrt(jnp.mean(x2 * x2, axis=-1, keepdims=True) + EPS)
    o_ref[...] = (x2 * rs) * gf_ref[...]

    @pl.when(step == nsteps - 1)
    def _():
        waits(1 - slot)


def _combine(x, mod, l, route, slot1, slot2, e_out, gf):
    b, t, d = x.shape
    nb, tt = _tile(b, t, ROWS_COMBINE)
    tm = nb * tt
    nt = t // tt
    grid_spec = pltpu.PrefetchScalarGridSpec(
        num_scalar_prefetch=2,
        grid=(b // nb, nt),
        in_specs=[pl.BlockSpec((nb, tt, d), lambda bi, i, a, c: (bi, i, 0)),
                  _mod_spec(nb, d, 5, l),
                  pl.BlockSpec((tm, ROUTE_LANES), lambda bi, i, a, c: (bi * nt + i, 0)),
                  pl.BlockSpec((1, d), lambda bi, i, a, c: (0, 0)),
                  pl.BlockSpec(memory_space=pl.ANY)],
        out_specs=pl.BlockSpec((nb, tt, d), lambda bi, i, a, c: (bi, i, 0)),
        scratch_shapes=[pltpu.VMEM((2, tm, d), F32), pltpu.VMEM((2, tm, d), F32), pltpu.SemaphoreType.DMA((2, 2))],
    )
    return pl.pallas_call(
        _combine_kernel,
        grid_spec=grid_spec,
        out_shape=jax.ShapeDtypeStruct((b, t, d), F32),
        compiler_params=_cparams(("arbitrary", "arbitrary")),
        name="moe_combine_norm",
    )(slot1, slot2, x, mod, route, gf, e_out)


def _lookup(table, idx):
    out = jnp.zeros_like(idx)
    for e in range(N_EXPERTS):
        out = jnp.where(idx == e, table[e], out)
    return out


def _floor_div(a, b):
    q = jnp.floor(a.astype(F32) / b.astype(F32)).astype(I32)
    q = jnp.where((q + 1) * b <= a, q + 1, q)
    return jnp.where(q * b > a, q - 1, q)


def _moe_layout(route_t, counts_f, counts_a_f, n_a):
    n = route_t.shape[1]
    r, sub = MOE_TILE_ROWS, MOE_SUB
    n_tiles = -(-TOP_K * n // r) + N_EXPERTS
    i1 = route_t[0].astype(I32)
    i2 = route_t[1].astype(I32)
    r1 = route_t[4].astype(I32)
    r2 = route_t[5].astype(I32)
    counts = counts_f[0, :N_EXPERTS].astype(I32)
    nt = (counts + r - 1) // r
    ntc = jnp.maximum(nt, 1)
    sz = jnp.maximum(((_floor_div(counts + ntc - 1, ntc) + sub - 1) // sub) * sub, sub)
    cum = jnp.cumsum(nt)
    tstart = cum - nt
    total = cum[-1]
    s_idx = jnp.arange(n_tiles, dtype=I32)
    te_raw = jnp.minimum(jnp.sum((s_idx[:, None] >= cum[None, :]).astype(I32), axis=1), N_EXPERTS - 1)
    used = s_idx < total
    last_e = jnp.max(jnp.where(used, te_raw, 0))
    te = jnp.where(used, te_raw, last_e)
    sz_t = _lookup(sz, te)
    k_in = s_idx - _lookup(tstart, te)
    tr = jnp.where(used, jnp.clip(_lookup(counts, te) - k_in * sz_t, 0, sz_t), 0).astype(I32)
    counts_a = counts_a_f[0, :N_EXPERTS].astype(I32)
    tsp = jnp.clip(_lookup(counts_a, te) - k_in * sz_t, 0, tr).astype(I32)

    def slot(ei, ri):
        sz_i = _lookup(sz, ei)
        k = _floor_div(ri, sz_i)
        return (_lookup(tstart, ei) + k) * r + (ri - k * sz_i)
    slot1, slot2 = slot(i1, r1), slot(i2, r2)
    tok = jnp.arange(n, dtype=I32)
    tok = jnp.where(tok < n_a, tok, tok - n_a)
    src = jnp.zeros((n_tiles * r,), I32).at[jnp.concatenate([slot1, slot2])].set(jnp.concatenate([tok, tok]))
    return te, tr, tsp, src, slot1, slot2, n_tiles


def kernel(x_prompt, x_sample, c_prompt, c_sample, state_pool, state_ssm_re, state_ssm_im, w_ada, b_ada, norm_mix, norm_ffn, w_in, w_pool, pool_scale, ssm_lam_re, ssm_lam_im, ssm_log_step, ssm_b_re, ssm_b_im, ssm_c_re, ssm_c_im, ssm_d, ssm_w_glu, w_out, ffn_w_gate, ffn_w_up, ffn_w_down, moe_w_router, moe_b_router, moe_w_gate, moe_w_up, moe_w_down, norm_final):
    bp, tp, d = x_prompt.shape
    bs, ts, _ = x_sample.shape
    depth = w_ada.shape[0]
    c_pool = w_pool.shape[1] * w_pool.shape[2]
    assert tp % CHUNK == 0 and ts <= CHUNK

    m_rows = -(-(bp + bs) // SUBLANES_V7X) * SUBLANES_V7X
    c_all = jnp.concatenate([c_prompt, c_sample, jnp.zeros((m_rows - bp - bs, d), F32)], axis=0)
    mod_all = _ada(c_all, w_ada, b_ada)

    mod_p = mod_all[:, :bp].reshape(depth, bp, 1, 6 * d)
    mod_s = mod_all[:, bp:bp + bs].reshape(depth, bs, 1, 6 * d)
    w_in_b, w_out_b = w_in.astype(BF16), w_out.astype(BF16)
    wp_b, wglu_b = w_pool.astype(BF16), ssm_w_glu.astype(BF16)
    ps = pool_scale.reshape(depth, 1, c_pool).astype(F32)
    dsk = ssm_d.reshape(depth, 1, d - c_pool).astype(F32)
    k_t, b_t, c_t, p16, q16, p8b, q8b = jax.vmap(lambda *a: _s5_params(*a, ts, bs))(
        ssm_lam_re, ssm_lam_im, ssm_log_step, ssm_b_re, ssm_b_im, ssm_c_re, ssm_c_im)
    x0_t = jnp.concatenate([state_ssm_re, state_ssm_im], axis=-1).astype(F32).transpose(0, 2, 3, 1)
    st_p = jnp.zeros((1, bp, POOL_BUF, c_pool), F32)
    st_s = state_pool.astype(F32)

    xp, xs = x_prompt, x_sample
    pool_p, re_p, im_p, pool_s, re_s, im_s = [], [], [], [], [], []
    y_p = y_s = None
    for l in range(depth):
        gm = norm_mix[l].reshape(1, d)
        gn = norm_ffn[l].reshape(1, d)

        up_all = _proj_in(xp, mod_p, l, gm, w_in_b)
        us_all = _proj_in(xs, mod_s, l, gm, w_in_b)
        y_ssm_p, xend = _s5_prompt(up_all, l, k_t, b_t, c_t, p16, q16, c_pool)
        ys2, x1_t = _s5_sample(us_all.reshape(bs * ts, d), l, x0_t, k_t, b_t, c_t, p8b, q8b, ts, c_pool)
        y_ssm_s = ys2.reshape(bs, ts, d - c_pool)
        ffn_f32 = (ffn_w_gate[l // 2], ffn_w_up[l // 2], ffn_w_down[l // 2]) if l % 2 == 0 else ()
        xp, new_pool_p, ffn_bf16 = _mix(xp, up_all, y_ssm_p, st_p, 0, mod_p, l, wp_b, ps, dsk, wglu_b, w_out_b, 0,
                                        cast=ffn_f32)
        xs, new_pool_s, _ = _mix(xs, us_all, y_ssm_s, st_s, l, mod_s, l, wp_b, ps, dsk, wglu_b, w_out_b, PAST_LEN)
        pool_p.append(new_pool_p)
        pool_s.append(new_pool_s)
        re_p.append(xend[..., :SSM_P])
        im_p.append(xend[..., SSM_P:])
        re_s.append(x1_t[:, :SSM_P, :].transpose(2, 0, 1))
        im_s.append(x1_t[:, SSM_P:, :].transpose(2, 0, 1))

        i = l // 2
        if l % 2 == 0:
            wg, wu, wd = ffn_bf16
            xp = _ffn(xp, mod_p, l, gn, wg, wu, wd)
            xs = _ffn(xs, mod_s, l, gn, wg, wu, wd)
        else:
            ne = moe_w_router.shape[-1]
            wr32 = jnp.pad(moe_w_router[i].astype(F32), ((0, 0), (0, ROUTE_LANES - ne)))
            wr_hi = wr32.astype(BF16)
            wr = jnp.stack([wr_hi, (wr32 - wr_hi.astype(F32)).astype(BF16)])
            br = jnp.pad(moe_b_router[i].astype(F32), (0, ROUTE_LANES - ne)).reshape(1, ROUTE_LANES)
            zero_counts = jnp.zeros((1, ROUTE_LANES), F32)
            h_p, route_p, rt_p, cnt_p = _route(xp, mod_p, l, gn, wr, br, zero_counts)
            h_s, route_s, rt_s, cnt_s = _route(xs, mod_s, l, gn, wr, br, cnt_p)
            n_p = bp * tp
            te, tr, tsp, src, slot1, slot2, n_tiles = _moe_layout(jnp.concatenate([rt_p, rt_s], axis=1), cnt_s, cnt_p,
                                                                  n_p)
            e_out = _moe_experts(h_p, h_s, te, tr, tsp, src, moe_w_gate[i], moe_w_up[i], moe_w_down[i], n_tiles)
            last = l == depth - 1
            assert last, "the combine kernel also applies the final norm"
            gf = norm_final.reshape(1, d)
            y_p = _combine(xp, mod_p, l, route_p, slot1[:n_p], slot2[:n_p], e_out, gf)
            y_s = _combine(xs, mod_s, l, route_s, slot1[n_p:], slot2[n_p:], e_out, gf)

    return (y_p, y_s, jnp.stack(pool_p), jnp.stack(re_p), jnp.stack(im_p),
            jnp.stack(pool_s), jnp.stack(re_s), jnp.stack(im_s))
```
